```python
import math
import jax, jax.numpy as jnp
from jax import lax
import numpy as np

D_MODEL = 1024
BATCH = 2
SEQ = 16384
DEPTH = 2
DEC_BATCH = 32
DEC_SEQ = 2048
PAST_LEN = 128

ATT_HEADS = 4
ATT_HEAD_DIM = 64
ATT_V_DIM = 2 * ATT_HEAD_DIM
ATT_QK_WIDTH = ATT_HEADS * 2 * ATT_HEAD_DIM
ATT_WIDTH = ATT_HEADS * ATT_V_DIM
Q_BLOCK = 128
SC_WIDTH = D_MODEL // 4
SC_KSIZE = 3
HG_HEADS = 4
HG_WIDTH = D_MODEL // 4
HG_KEY_DIM = HG_WIDTH // HG_HEADS
HG_VAL_DIM = HG_WIDTH // HG_HEADS
HG_CHUNK = 64
N_BRANCH = 3
D_FF = 2816
N_EXPERTS = 8
TOP_K = 2
D_FF_EXPERT = 3584
N_DENSE = (DEPTH + 1) // 2
N_MOE = DEPTH // 2
EPS = 1e-6

IN_SIZES = (ATT_QK_WIDTH, ATT_QK_WIDTH, ATT_WIDTH, 3 * SC_WIDTH, 5 * HG_WIDTH, N_BRANCH * D_MODEL)
IN_COLS = sum(IN_SIZES)
SPLIT_POINTS = tuple(int(v) for v in np.cumsum(IN_SIZES)[:-1])

kernel_name = "hybrid_diffattn_shortconv_hgrn2_encoder"


def rms_norm(x, gain):
    xf = x.astype(jnp.float32)
    y = xf * lax.rsqrt(jnp.mean(xf * xf, axis=-1, keepdims=True) + EPS)
    return (y * gain.astype(jnp.float32)).astype(x.dtype)


def alibi_slopes(n):
    return jnp.exp2(-8.0 * (jnp.arange(n, dtype=jnp.float32) + 1.0) / n)


def diff_attention(q, k, v, q_gain, k_gain, lam, sub_gain, lam_init):
    B, T = q.shape[0], q.shape[1]
    qh = (rms_norm(q, q_gain).astype(jnp.float32) * (ATT_HEAD_DIM ** -0.5)).transpose(0, 2, 3, 1, 4)
    kh = rms_norm(k, k_gain).astype(jnp.float32).transpose(0, 2, 3, 1, 4)
    vh = v.astype(jnp.float32).transpose(0, 2, 1, 3)
    slopes = alibi_slopes(ATT_HEADS)
    key_pos = jnp.arange(T)

    def block(start):
        qb = lax.dynamic_slice_in_dim(qh, start, Q_BLOCK, axis=3)
        s = jnp.einsum('bhcqd,bhckd->bhcqk', qb, kh)
        dist = jnp.abs((start + jnp.arange(Q_BLOCK))[:, None] - key_pos[None, :]).astype(jnp.float32)
        s = s - (slopes[:, None, None] * dist)[None, :, None]
        p = jax.nn.softmax(s, axis=-1)
        a = p[:, :, 0] - lam * p[:, :, 1]
        return jnp.einsum('bhqk,bhkv->bhqv', a, vh)

    starts = jnp.arange(T // Q_BLOCK) * Q_BLOCK
    o = lax.map(block, starts)
    o = o.transpose(1, 0, 3, 2, 4).reshape(B, T, ATT_HEADS, ATT_V_DIM)
    o = rms_norm(o, sub_gain) * (1.0 - lam_init)
    return o.reshape(B, T, ATT_WIDTH)


def short_conv(b_gate, c_gate, h_in, w):
    u = c_gate * h_in
    up = jnp.pad(u, ((0, 0), (1, 1), (0, 0)))
    conv = w[0] * up[:, :-2] + w[1] * up[:, 1:-1] + w[2] * up[:, 2:]
    return b_gate * conv


def chunk_recurrence(q, k, v, logf):
    B, T, H, Dk = q.shape
    Dv = v.shape[-1]
    N = T // HG_CHUNK

    def chunks(t):
        return t.reshape(B, N, HG_CHUNK, H, t.shape[-1]).transpose(1, 0, 3, 2, 4)

    causal = jnp.tril(jnp.ones((HG_CHUNK, HG_CHUNK), dtype=bool))

    def step(S, inp):
        qc, kc, vc, gc = inp
        b = jnp.cumsum(gc, axis=2)
        inter = jnp.einsum('bhtk,bhkv->bhtv', qc * jnp.exp(b), S)
        diff = b[:, :, :, None, :] - b[:, :, None, :, :]
        decay = jnp.exp(jnp.where(causal[:, :, None], diff, -jnp.inf))
        A = jnp.einsum('bhtk,bhsk,bhtsk->bhts', qc, kc, decay)
        intra = jnp.einsum('bhts,bhsv->bhtv', A, vc)
        b_last = b[:, :, -1:, :]
        S_new = jnp.exp(b_last[:, :, 0, :])[..., None] * S + jnp.einsum('bhsk,bhsv->bhkv', kc * jnp.exp(b_last - b), vc)
        return S_new, inter + intra

    S0 = jnp.zeros((B, H, Dk, Dv), jnp.float32)
    _, o = lax.scan(step, S0, (chunks(q), chunks(k), chunks(v), chunks(logf)))
    return o.transpose(1, 0, 3, 2, 4).reshape(B, T, H, Dv)


def hgrn2(q, i, f_fwd, f_bwd, g, lb, norm_gain):
    B, T = q.shape[0], q.shape[1]

    def heads(t):
        return t.reshape(B, T, HG_HEADS, -1).astype(jnp.float32)

    qh = heads(jax.nn.silu(q.astype(jnp.float32)))
    vh = heads(i)

    def gates(f_logit, lb_d):
        f = lb_d + (1.0 - lb_d) * jax.nn.sigmoid(f_logit.astype(jnp.float32))
        return heads(1.0 - f), heads(jnp.log(f))

    k_f, g_f = gates(f_fwd, lb[0])
    k_b, g_b = gates(f_bwd, lb[1])
    o_f = chunk_recurrence(qh, k_f, vh, g_f)
    flip = lambda t: jnp.flip(t, axis=1)
    o_b = flip(chunk_recurrence(flip(qh), flip(k_b), flip(vh), flip(g_b)))
    o = rms_norm(o_f + o_b, norm_gain.reshape(HG_HEADS, HG_VAL_DIM))
    return o.reshape(B, T, HG_WIDTH) * jax.nn.silu(g.astype(jnp.float32))


def mixer_layer(x, norm_mix, w_in, q_norm, k_norm, lam, sub_norm, lam_init, conv_w, lb, hg_norm,
                w_up_a, w_up_b, w_up_c, w_out):
    B, T = x.shape[0], x.shape[1]
    h = rms_norm(x, norm_mix)
    proj = h @ w_in
    q_a, k_a, v_a, sc, hg, gate = jnp.split(proj, SPLIT_POINTS, axis=-1)
    y_a = diff_attention(q_a.reshape(B, T, ATT_HEADS, 2, ATT_HEAD_DIM),
                         k_a.reshape(B, T, ATT_HEADS, 2, ATT_HEAD_DIM),
                         v_a.reshape(B, T, ATT_HEADS, ATT_V_DIM),
                         q_norm, k_norm, lam, sub_norm, lam_init).astype(x.dtype) @ w_up_a
    sb, scg, sh = jnp.split(sc, 3, axis=-1)
    y_b = short_conv(sb, scg, sh, conv_w) @ w_up_b
    hq, hi, hff, hfb, hgt = jnp.split(hg, 5, axis=-1)
    y_c = hgrn2(hq, hi, hff, hfb, hgt, lb, hg_norm).astype(x.dtype) @ w_up_c
    gates = jax.nn.sigmoid(gate.reshape(B, T, N_BRANCH, D_MODEL))
    merged = gates[:, :, 0] * y_a + gates[:, :, 1] * y_b + gates[:, :, 2] * y_c
    return x + merged @ w_out


def swiglu(h, wg, wu, wd):
    return (jax.nn.silu(h @ wg) * (h @ wu)) @ wd


def moe_swiglu(h, w_router, wg, wu, wd):
    logits = (h @ w_router).astype(jnp.float32)
    top_val, top_idx = lax.top_k(logits, TOP_K)
    top_w = jax.nn.softmax(top_val, axis=-1)
    combine = jnp.sum(jax.nn.one_hot(top_idx, N_EXPERTS, dtype=jnp.float32) * top_w[..., None], axis=-2)
    combine = combine.astype(h.dtype)
    y = jnp.zeros_like(h)
    for e in range(N_EXPERTS):
        y = y + combine[..., e:e + 1] * swiglu(h, wg[e], wu[e], wd[e])
    return y


def setup_inputs(seed: int = 0) -> dict:
    key = jax.random.key(seed)
    ks = jax.random.split(key, 26)

    def nrm(k, shape, scale):
        return jax.random.normal(k, shape, jnp.float32) * scale

    def gain(k, shape):
        return 1.0 + 0.02 * jax.random.normal(k, shape, jnp.float32)

    return {
        "x_prompt": nrm(ks[0], (BATCH, SEQ, D_MODEL), 1.0),
        "x_sample": nrm(ks[1], (DEC_BATCH, DEC_SEQ, D_MODEL), 1.0),
        "norm_mix": gain(ks[2], (DEPTH, D_MODEL)),
        "w_in": nrm(ks[3], (DEPTH, D_MODEL, IN_COLS), D_MODEL ** -0.5),
        "q_norm": gain(ks[4], (DEPTH, ATT_HEAD_DIM)),
        "k_norm": gain(ks[5], (DEPTH, ATT_HEAD_DIM)),
        "lambda_q1": nrm(ks[6], (DEPTH, ATT_HEAD_DIM), 0.1),
        "lambda_k1": nrm(ks[7], (DEPTH, ATT_HEAD_DIM), 0.1),
        "lambda_q2": nrm(ks[8], (DEPTH, ATT_HEAD_DIM), 0.1),
        "lambda_k2": nrm(ks[9], (DEPTH, ATT_HEAD_DIM), 0.1),
        "sub_norm": gain(ks[10], (DEPTH, ATT_V_DIM)),
        "conv_w": nrm(ks[11], (DEPTH, SC_KSIZE, SC_WIDTH), SC_KSIZE ** -0.5),
        "hg_lower": nrm(ks[12], (DEPTH, 2, HG_WIDTH), 0.5),
        "hg_norm": gain(ks[13], (DEPTH, HG_WIDTH)),
        "w_up_a": nrm(ks[14], (DEPTH, ATT_WIDTH, D_MODEL), ATT_WIDTH ** -0.5),
        "w_up_b": nrm(ks[15], (DEPTH, SC_WIDTH, D_MODEL), SC_WIDTH ** -0.5),
        "w_up_c": nrm(ks[16], (DEPTH, HG_WIDTH, D_MODEL), HG_WIDTH ** -0.5),
        "w_out": nrm(ks[17], (DEPTH, D_MODEL, D_MODEL), D_MODEL ** -0.5),
        "norm_ffn": gain(ks[18], (DEPTH, D_MODEL)),
        "w_gate_dense": nrm(ks[19], (N_DENSE, D_MODEL, D_FF), D_MODEL ** -0.5),
        "w_up_dense": nrm(ks[20], (N_DENSE, D_MODEL, D_FF), D_MODEL ** -0.5),
        "w_down_dense": nrm(ks[21], (N_DENSE, D_FF, D_MODEL), D_FF ** -0.5),
        "w_router": nrm(ks[22], (N_MOE, D_MODEL, N_EXPERTS), D_MODEL ** -0.5),
        "w_gate_moe": nrm(ks[23], (N_MOE, N_EXPERTS, D_MODEL, D_FF_EXPERT), D_MODEL ** -0.5),
        "w_up_moe": nrm(ks[24], (N_MOE, N_EXPERTS, D_MODEL, D_FF_EXPERT), D_MODEL ** -0.5),
        "w_down_moe": nrm(ks[25], (N_MOE, N_EXPERTS, D_FF_EXPERT, D_MODEL), D_FF_EXPERT ** -0.5),
    }


def reference(x_prompt, x_sample, norm_mix, w_in, q_norm, k_norm, lambda_q1, lambda_k1, lambda_q2,
              lambda_k2, sub_norm, conv_w, hg_lower, hg_norm, w_up_a, w_up_b, w_up_c, w_out, norm_ffn,
              w_gate_dense, w_up_dense, w_down_dense, w_router, w_gate_moe, w_up_moe, w_down_moe):
    lb_all = jnp.cumsum(jax.nn.softmax(hg_lower.astype(jnp.float32), axis=0), axis=0)
    lb_all = lb_all - lb_all[0]

    def trunk(x):
        for l in range(DEPTH):
            lam_init = 0.8 - 0.6 * math.exp(-0.3 * l)
            lam = (jnp.exp(jnp.sum(lambda_q1[l].astype(jnp.float32) * lambda_k1[l].astype(jnp.float32)))
                   - jnp.exp(jnp.sum(lambda_q2[l].astype(jnp.float32) * lambda_k2[l].astype(jnp.float32)))
                   + lam_init)
            x = mixer_layer(x, norm_mix[l], w_in[l], q_norm[l], k_norm[l], lam, sub_norm[l], lam_init,
                            conv_w[l], lb_all[l], hg_norm[l], w_up_a[l], w_up_b[l], w_up_c[l], w_out[l])
            h = rms_norm(x, norm_ffn[l])
            if l % 2 == 0:
                j = l // 2
                x = x + swiglu(h, w_gate_dense[j], w_up_dense[j], w_down_dense[j])
            else:
                j = l // 2
                x = x + moe_swiglu(h, w_router[j], w_gate_moe[j], w_up_moe[j], w_down_moe[j])
        return x

    y_prompt = trunk(x_prompt)
    y_sample = trunk(x_sample)
    return (y_prompt, y_sample)
```

```python
import functools
import math

import numpy as np
import jax
import jax.numpy as jnp
from jax import lax
from jax.experimental import pallas as pl
from jax.experimental.pallas import tpu as pltpu

F32 = jnp.float32
BF16 = jnp.bfloat16

D_MODEL = 1024
ATT_HEADS = 4
ATT_HEAD_DIM = 64
ATT_V_DIM = 128
ATT_QK_WIDTH = 512
ATT_WIDTH = 512
SC_WIDTH = 256
HG_WIDTH = 256
HG_HEAD_DIM = 64
HG_CHUNK = 64
N_BRANCH = 3
N_EXPERTS = 8
EPS = 1e-6

LANES = 128
VMEM_LIMIT = 56 * 1024 * 1024
NEG_BIG = -1e30

_NT = (((1,), (1,)), ((), ()))
_TN = (((0,), (0,)), ((), ()))


def _const_spec(shape):
    nd = len(shape)
    return pl.BlockSpec(shape, lambda *_: (0,) * nd, pipeline_mode=pl.Buffered(1))


def _params(sem):
    return pltpu.CompilerParams(dimension_semantics=sem, vmem_limit_bytes=VMEM_LIMIT)


def _split_bf16(x):
    hi = x.astype(BF16)
    lo = (x - hi.astype(F32)).astype(BF16)
    return hi, lo


def _group_sum(x, gmat):
    hi, lo = _split_bf16(x)
    return (jnp.dot(hi, gmat, preferred_element_type=F32) + jnp.dot(lo, gmat, preferred_element_type=F32))


def _block_ones(width, group):
    idx = np.arange(width) // group
    return jnp.asarray((idx[:, None] == idx[None, :]).astype(np.float32), dtype=BF16)


def _inproj_kernel(x_ref, g_ref, wm_ref, wvt_ref, qg_ref, kg_ref, gsum_ref, lb_ref,
                   q_ref, k_ref, vt_ref, cv_ref, hp_ref, lg_ref, sg_ref, gate_ref):
    x = x_ref[...]
    ms = jnp.mean(x * x, axis=-1, keepdims=True)
    h = (x * lax.rsqrt(ms + EPS) * g_ref[...]).astype(BF16)

    vt_ref[...] = lax.dot_general(wvt_ref[...], h, _NT, preferred_element_type=F32).astype(BF16)

    gsum = gsum_ref[...]

    def head_norm(raw, gain):
        ss = _group_sum(raw * raw, gsum)
        return (raw * lax.rsqrt(ss * (1.0 / ATT_HEAD_DIM) + EPS) * gain).astype(BF16)

    q_ref[...] = head_norm(jnp.dot(h, wm_ref[:, 0:512], preferred_element_type=F32), qg_ref[...])
    k_ref[...] = head_norm(jnp.dot(h, wm_ref[:, 512:1024], preferred_element_type=F32), kg_ref[...])

    sc = jnp.dot(h, wm_ref[:, 1024:1792], preferred_element_type=F32)
    cv_ref[:, 0:256] = sc[:, 0:256].astype(BF16)
    cv_ref[:, 256:512] = (sc[:, 256:512] * sc[:, 512:768]).astype(BF16)

    hg = jnp.dot(h, wm_ref[:, 1792:3072], preferred_element_type=F32)
    hp_ref[:, 0:256] = jax.nn.silu(hg[:, 0:256]).astype(BF16)
    hp_ref[:, 256:512] = hg[:, 256:512].astype(BF16)
    for d in range(2):
        z = hg[:, 512 + 256 * d:768 + 256 * d]
        lb = lb_ref[d:d + 1, :]
        f = lb + (1.0 - lb) * jax.nn.sigmoid(z)
        hp_ref[:, 512 + 256 * d:768 + 256 * d] = ((1.0 - lb) * jax.nn.sigmoid(-z)).astype(BF16)
        lg_ref[:, 256 * d:256 * d + 256] = jnp.log(f)
    sg_ref[...] = jax.nn.silu(hg[:, 1024:1280]).astype(BF16)

    gate_ref[...] = jax.nn.sigmoid(
        jnp.dot(h, wm_ref[:, 3072:6144], preferred_element_type=F32)).astype(BF16)


def _inproj(x, gain, w_main, w_vt, qg, kg, lb, tm):
    n = x.shape[0]
    row = lambda w: pl.BlockSpec((tm, w), lambda i: (i, 0))
    out_shape = (
        jax.ShapeDtypeStruct((n, 512), BF16),
        jax.ShapeDtypeStruct((n, 512), BF16),
        jax.ShapeDtypeStruct((512, n), BF16),
        jax.ShapeDtypeStruct((n, 512), BF16),
        jax.ShapeDtypeStruct((n, 1024), BF16),
        jax.ShapeDtypeStruct((n, 512), F32),
        jax.ShapeDtypeStruct((n, 256), BF16),
        jax.ShapeDtypeStruct((n, 3072), BF16),
    )
    return pl.pallas_call(
        _inproj_kernel,
        grid=(n // tm,),
        in_specs=[row(D_MODEL), _const_spec((1, D_MODEL)), _const_spec(w_main.shape), _const_spec(w_vt.shape),
                  _const_spec((1, 512)), _const_spec((1, 512)), _const_spec((512, 512)), _const_spec((2, 256))],
        out_specs=(row(512), row(512), pl.BlockSpec((512, tm), lambda i: (0, i)), row(512), row(1024),
                   row(512), row(256), row(3072)),
        out_shape=out_shape,
        compiler_params=_params(("parallel",)),
        name="inproj",
    )(x, gain, w_main, w_vt, qg, kg, _block_ones(512, ATT_HEAD_DIM), lb)


def _attn_kernel(sc_ref, q_ref, k_ref, vt_ref, subg_ref, o_ref, qm_ref, m_ref, l_ref, acc_ref, *, tq, tk, nk):
    h = pl.program_id(1)
    qi = pl.program_id(2)
    ki = pl.program_id(3)

    @pl.when(ki == 0)
    def _():
        q = q_ref[...]
        lane = lax.broadcasted_iota(jnp.int32, q.shape, 1)
        zero = jnp.zeros_like(q)
        qm_ref[0] = jnp.where(lane < ATT_HEAD_DIM, q, zero)
        qm_ref[1] = jnp.where(lane >= ATT_HEAD_DIM, q, zero)
        m_ref[...] = jnp.full(m_ref.shape, NEG_BIG, F32)
        l_ref[...] = jnp.zeros(l_ref.shape, F32)
        acc_ref[...] = jnp.zeros(acc_ref.shape, F32)

    slope = sc_ref[h]
    kpos = ki * tk + lax.broadcasted_iota(jnp.int32, (tk, tq), 0)
    qpos = qi * tq + lax.broadcasted_iota(jnp.int32, (tk, tq), 1)
    bias = slope * jnp.abs(kpos - qpos).astype(F32)
    k = k_ref[...]
    vt = vt_ref[...]
    for c in range(2):
        s = lax.dot_general(k, qm_ref[c], _NT, preferred_element_type=F32) - bias
        m_prev = m_ref[c]
        m_new = jnp.maximum(m_prev, jnp.max(s, axis=0, keepdims=True))
        alpha = jnp.exp(m_prev - m_new)
        p = jnp.exp(s - m_new)
        l_ref[c] = alpha * l_ref[c] + jnp.sum(p, axis=0, keepdims=True)
        acc_ref[c] = alpha * acc_ref[c] + jnp.dot(vt, p.astype(BF16), preferred_element_type=F32)
        m_ref[c] = m_new

    @pl.when(ki == nk - 1)
    def _():
        lam = sc_ref[ATT_HEADS]
        o = acc_ref[0] / l_ref[0] - lam * (acc_ref[1] / l_ref[1])
        ms = jnp.mean(o * o, axis=0, keepdims=True)
        o = o * lax.rsqrt(ms + EPS) * subg_ref[...]
        o_ref[...] = o.T.astype(BF16)


def _attention(q, k, vt, scal, subg, b, t, tq, tk):
    nq, nk = t // tq, t // tk
    kern = functools.partial(_attn_kernel, tq=tq, tk=tk, nk=nk)
    return pl.pallas_call(
        kern,
        grid=(b, ATT_HEADS, nq, nk),
        in_specs=[
            pl.BlockSpec(memory_space=pltpu.SMEM),
            pl.BlockSpec((None, tq, LANES), lambda bi, h, qi, ki: (bi, qi, h)),
            pl.BlockSpec((None, tk, LANES), lambda bi, h, qi, ki: (bi, ki, h)),
            pl.BlockSpec((LANES, tk), lambda bi, h, qi, ki: (h, bi * nk + ki)),
            _const_spec((ATT_V_DIM, 1)),
        ],
        out_specs=pl.BlockSpec((None, tq, LANES), lambda bi, h, qi, ki: (bi, qi, h)),
        out_shape=jax.ShapeDtypeStruct((b, t, ATT_WIDTH), BF16),
        scratch_shapes=[
            pltpu.VMEM((2, tq, LANES), BF16),
            pltpu.VMEM((2, 1, tq), F32),
            pltpu.VMEM((2, 1, tq), F32),
            pltpu.VMEM((2, ATT_V_DIM, tq), F32),
        ],
        compiler_params=_params(("parallel", "parallel", "parallel", "arbitrary")),
        name="diff_attn",
    )(scal, q, k, vt, subg)


N_LEVELS = 6


def _hgrn_constants(tt):
    c = HG_CHUNK
    t = np.arange(tt)
    same_chunk = (t[:, None] // c) == (t[None, :] // c)
    mats, masks = [], []
    masks.append(np.eye(tt, dtype=np.float32))
    for j in range(N_LEVELS):
        m = 1 << j
        blk = t // m
        same_blk = blk[:, None] == blk[None, :]
        odd = (blk % 2) == 1
        incl = same_blk & (t[None, :] <= t[:, None])
        excl_rev = same_blk & (t[None, :] > t[:, None])
        mats.append(np.where(odd[:, None], incl, excl_rev))
        same_2m = (t[:, None] // (2 * m)) == (t[None, :] // (2 * m))
        masks.append((same_2m & odd[:, None] & (~odd)[None, :]).astype(np.float32))
    mats.append(same_chunk & (t[None, :] <= t[:, None]))
    mats.append(same_chunk & (t[None, :] > t[:, None]))
    mats = np.stack(mats).astype(np.float32)
    masks = np.stack(masks)
    mats_b = mats[:, ::-1, ::-1]
    masks_b = masks[:, ::-1, ::-1]
    mst = np.concatenate([mats.reshape(-1, tt), mats_b.reshape(-1, tt)], axis=0)
    amask = np.concatenate([masks, masks_b], axis=0)
    return jnp.asarray(mst, dtype=BF16), jnp.asarray(amask, dtype=F32)


def _hgrn_kernel(hpf_ref, hpb_ref, lgf_ref, lgb_ref, mst_ref, amask_ref, of_ref, ob_ref, s_ref, *, tt):
    i = pl.program_id(1)
    nlev = N_LEVELS + 1
    nmat = N_LEVELS + 2
    nchunk = tt // HG_CHUNK

    @pl.when(i == 0)
    def _():
        s_ref[...] = jnp.zeros(s_ref.shape, F32)

    lane256 = lax.broadcasted_iota(jnp.int32, (1, 256), 1)
    lane128 = lax.broadcasted_iota(jnp.int32, (1, LANES), 1)
    r128 = lax.broadcasted_iota(jnp.int32, (LANES, LANES), 0)
    c128 = lax.broadcasted_iota(jnp.int32, (LANES, LANES), 1)
    bdmask = ((r128 // HG_HEAD_DIM) == (c128 // HG_HEAD_DIM)).astype(F32)

    def direction(d, hp_ref, lg_ref, o_ref):
        qs = hp_ref[:, 0:256].astype(F32)
        v = hp_ref[:, 256:512]
        kk = hp_ref[:, 512 + 256 * d:768 + 256 * d].astype(F32)
        hi, lo = _split_bf16(lg_ref[...])
        hl = jnp.concatenate([hi, lo], axis=1)
        eall = jnp.dot(mst_ref[d * nmat * tt:(d + 1) * nmat * tt, :], hl, preferred_element_type=F32)

        def expo(j):
            r = eall[j * tt:(j + 1) * tt]
            return jnp.exp(r[:, 0:256] + r[:, 256:512])

        a = [None] * 4
        for lev in range(nlev):
            if lev == 0:
                qt, kt = qs.astype(BF16), kk.astype(BF16)
            else:
                xx = expo(lev - 1)
                qt, kt = (qs * xx).astype(BF16), (kk * xx).astype(BF16)
            am = amask_ref[d * nlev + lev]
            for pr in range(2):
                qp = qt[:, LANES * pr:LANES * (pr + 1)]
                kp = kt[:, LANES * pr:LANES * (pr + 1)]
                for hh in range(2):
                    sel = (lane128 // HG_HEAD_DIM) == hh
                    qm = jnp.where(sel, qp, jnp.zeros_like(qp))
                    p = lax.dot_general(qm, kp, _NT, preferred_element_type=F32) * am
                    idx = 2 * pr + hh
                    a[idx] = p if a[idx] is None else a[idx] + p

        o = None
        for idx in range(4):
            vm = jnp.where((lane256 // HG_HEAD_DIM) == idx, v, jnp.zeros_like(v))
            contrib = jnp.dot(a[idx].astype(BF16), vm, preferred_element_type=F32)
            o = contrib if o is None else o + contrib
        o_ref[...] = o

        xq = expo(N_LEVELS)
        xk = expo(N_LEVELS + 1)
        qc = (qs * xq).astype(BF16)
        kc = (kk * xk).astype(BF16)
        order = range(nchunk) if d == 0 else range(nchunk - 1, -1, -1)
        for c in order:
            r0 = c * HG_CHUNK
            rows = slice(r0, r0 + HG_CHUNK)
            drow = r0 + HG_CHUNK - 1 if d == 0 else r0
            for pr in range(2):
                cols = slice(LANES * pr, LANES * (pr + 1))
                st = s_ref[d, pr]
                inter = lax.dot_general(qc[rows, cols], st.astype(BF16), _NT, preferred_element_type=F32)
                o_ref[rows, cols] += inter
                ut = lax.dot_general(v[rows, cols], kc[rows, cols], _TN, preferred_element_type=F32)
                dec = xq[drow:drow + 1, cols]
                s_ref[d, pr] = st * dec + ut * bdmask

    direction(0, hpf_ref, lgf_ref, of_ref)
    direction(1, hpb_ref, lgb_ref, ob_ref)


def _hgrn(hp, lg, b, t, tt):
    nt = t // tt
    mst, amask = _hgrn_constants(tt)
    kern = functools.partial(_hgrn_kernel, tt=tt)
    fwd = lambda w, cb: pl.BlockSpec((None, tt, w), lambda bi, i: (bi, i, cb))
    bwd = lambda w, cb: pl.BlockSpec((None, tt, w), lambda bi, i: (bi, nt - 1 - i, cb))
    return pl.pallas_call(
        kern,
        grid=(b, nt),
        in_specs=[fwd(1024, 0), bwd(1024, 0), fwd(256, 0), bwd(256, 1),
                  _const_spec(mst.shape), _const_spec(amask.shape)],
        out_specs=(fwd(256, 0), bwd(256, 0)),
        out_shape=(jax.ShapeDtypeStruct((b, t, HG_WIDTH), F32), jax.ShapeDtypeStruct((b, t, HG_WIDTH), F32)),
        scratch_shapes=[pltpu.VMEM((2, 2, LANES, LANES), F32)],
        compiler_params=_params(("parallel", "arbitrary")),
        name="hgrn2",
    )(hp, hp, lg, lg, mst, amask)


def _merge_kernel(*refs, tm, seq, with_router):
    (x_ref, oa_ref, cv_ref, cvp_ref, cvn_ref, of_ref, ob_ref, sg_ref, gate_ref, cw_ref, hgn_ref, g256_ref,
     wa_ref, wb_ref, wc_ref, wo_ref, g2_ref) = refs[:17]
    if with_router:
        wr_ref, xo_ref, h2_ref, lo_ref = refs[17:]
    else:
        xo_ref, h2_ref = refs[17:]
    i = pl.program_id(0)
    tile_start = (i * tm) % seq

    u = cv_ref[:, 256:512].astype(F32)
    row = lax.broadcasted_iota(jnp.int32, (tm, 1), 0)
    prev_row = jnp.where(tile_start == 0, 0.0, cvp_ref[7:8, 256:512].astype(F32))
    next_row = jnp.where(tile_start + tm == seq, 0.0, cvn_ref[0:1, 256:512].astype(F32))
    u_m1 = jnp.where(row == 0, prev_row, pltpu.roll(u, 1, axis=0))
    u_p1 = jnp.where(row == tm - 1, next_row, pltpu.roll(u, tm - 1, axis=0))
    conv = cw_ref[0:1, :] * u_m1 + cw_ref[1:2, :] * u + cw_ref[2:3, :] * u_p1
    yb_in = (cv_ref[:, 0:256].astype(F32) * conv).astype(BF16)

    o = of_ref[...] + ob_ref[...]
    ss = _group_sum(o * o, g256_ref[...])
    yc_in = (o * lax.rsqrt(ss * (1.0 / HG_HEAD_DIM) + EPS) * hgn_ref[...] * sg_ref[...].astype(F32)).astype(BF16)

    ya = jnp.dot(oa_ref[...], wa_ref[...], preferred_element_type=F32)
    yb = jnp.dot(yb_in, wb_ref[...], preferred_element_type=F32)
    yc = jnp.dot(yc_in, wc_ref[...], preferred_element_type=F32)
    merged = (gate_ref[:, 0:1024].astype(F32) * ya + gate_ref[:, 1024:2048].astype(F32) * yb
              + gate_ref[:, 2048:3072].astype(F32) * yc)
    xn = x_ref[...] + jnp.dot(merged.astype(BF16), wo_ref[...], preferred_element_type=F32)
    xo_ref[...] = xn

    ms = jnp.mean(xn * xn, axis=-1, keepdims=True)
    h2 = xn * lax.rsqrt(ms + EPS) * g2_ref[...]
    h2_ref[...] = h2.astype(BF16)
    if with_router:
        lo_ref[...] = jnp.dot(h2, wr_ref[...], preferred_element_type=F32, precision=lax.Precision.HIGHEST)


def _merge(x, oa, cv, of, ob, sg, gates, conv_w, hg_norm, wa, wb, wc, wo, g2, w_router, seq, tm):
    n = x.shape[0]
    nb8 = n // 8
    step8 = tm // 8
    row = lambda w: pl.BlockSpec((tm, w), lambda i: (i, 0))
    in_specs = [
        row(D_MODEL), row(512), row(512),
        pl.BlockSpec((8, 512), lambda i: (jnp.maximum(i * step8 - 1, 0), 0)),
        pl.BlockSpec((8, 512), lambda i: (jnp.minimum((i + 1) * step8, nb8 - 1), 0)),
        row(256), row(256), row(256), row(3072),
        _const_spec((8, 256)), _const_spec((1, 256)), _const_spec((256, 256)),
        _const_spec(wa.shape), _const_spec(wb.shape), _const_spec(wc.shape), _const_spec(wo.shape),
        _const_spec((1, D_MODEL)),
    ]
    args = [x, oa, cv, cv, cv, of, ob, sg, gates, conv_w, hg_norm, _block_ones(256, HG_HEAD_DIM),
            wa, wb, wc, wo, g2]
    out_specs = [row(D_MODEL), row(D_MODEL)]
    out_shape = [jax.ShapeDtypeStruct((n, D_MODEL), F32), jax.ShapeDtypeStruct((n, D_MODEL), BF16)]
    with_router = w_router is not None
    if with_router:
        in_specs.append(_const_spec(w_router.shape))
        args.append(w_router)
        out_specs.append(row(LANES))
        out_shape.append(jax.ShapeDtypeStruct((n, LANES), F32))
    kern = functools.partial(_merge_kernel, tm=tm, seq=seq, with_router=with_router)
    return pl.pallas_call(
        kern,
        grid=(n // tm,),
        in_specs=in_specs,
        out_specs=tuple(out_specs),
        out_shape=tuple(out_shape),
        compiler_params=_params(("parallel",)),
        name="merge_router" if with_router else "merge",
    )(*args)


def _ffn_kernel(x_ref, h_ref, wg_ref, wu_ref, wd_ref, o_ref):
    h = h_ref[...]
    g = jnp.dot(h, wg_ref[...], preferred_element_type=F32)
    u = jnp.dot(h, wu_ref[...], preferred_element_type=F32)
    act = (jax.nn.silu(g) * u).astype(BF16)
    o_ref[...] = x_ref[...] + jnp.dot(act, wd_ref[...], preferred_element_type=F32)


def _ffn(x, h2, wg, wu, wd, tm):
    n = x.shape[0]
    row = lambda: pl.BlockSpec((tm, D_MODEL), lambda i: (i, 0))
    return pl.pallas_call(
        _ffn_kernel,
        grid=(n // tm,),
        in_specs=[row(), row(), _const_spec(wg.shape), _const_spec(wu.shape), _const_spec(wd.shape)],
        out_specs=row(),
        out_shape=jax.ShapeDtypeStruct((n, D_MODEL), F32),
        compiler_params=_params(("parallel",)),
        name="ffn_dense",
    )(x, h2, wg, wu, wd)


def _moe_kernel(x_ref, h_ref, lo_ref, wg_ref, wu_ref, wd_ref, o_ref, comb_ref):
    e = pl.program_id(1)
    f = pl.program_id(2)
    lane = lax.broadcasted_iota(jnp.int32, comb_ref.shape, 1)

    @pl.when((e == 0) & (f == 0))
    def _():
        lg = jnp.where(lane < N_EXPERTS, lo_ref[...], NEG_BIG)
        m1 = jnp.max(lg, axis=-1, keepdims=True)
        i1 = jnp.min(jnp.where(lg == m1, lane, LANES), axis=-1, keepdims=True)
        lg2 = jnp.where(lane == i1, NEG_BIG, lg)
        m2 = jnp.max(lg2, axis=-1, keepdims=True)
        i2 = jnp.min(jnp.where(lg2 == m2, lane, LANES), axis=-1, keepdims=True)
        e2 = jnp.exp(m2 - m1)
        w1 = 1.0 / (1.0 + e2)
        w2 = e2 / (1.0 + e2)
        comb_ref[...] = jnp.where(lane == i1, w1, 0.0) + jnp.where(lane == i2, w2, 0.0)
        o_ref[...] = x_ref[...]

    h = h_ref[...]
    g = jnp.dot(h, wg_ref[...], preferred_element_type=F32)
    u = jnp.dot(h, wu_ref[...], preferred_element_type=F32)
    act = (jax.nn.silu(g) * u).astype(BF16)
    y = jnp.dot(act, wd_ref[...], preferred_element_type=F32)
    ce = jnp.sum(jnp.where(lane == e, comb_ref[...], 0.0), axis=-1, keepdims=True)
    o_ref[...] += ce * y


def _moe(x, h2, logits, wg, wu, wd, tm, tf):
    n = x.shape[0]
    dff = wg.shape[-1]
    row = lambda w: pl.BlockSpec((tm, w), lambda i, e, f: (i, 0))
    return pl.pallas_call(
        _moe_kernel,
        grid=(n // tm, N_EXPERTS, dff // tf),
        in_specs=[row(D_MODEL), row(D_MODEL), row(LANES),
                  pl.BlockSpec((None, D_MODEL, tf), lambda i, e, f: (e, 0, f)),
                  pl.BlockSpec((None, D_MODEL, tf), lambda i, e, f: (e, 0, f)),
                  pl.BlockSpec((None, tf, D_MODEL), lambda i, e, f: (e, f, 0))],
        out_specs=row(D_MODEL),
        out_shape=jax.ShapeDtypeStruct((n, D_MODEL), F32),
        scratch_shapes=[pltpu.VMEM((tm, LANES), F32)],
        compiler_params=_params(("parallel", "arbitrary", "arbitrary")),
        name="moe_dense",
    )(x, h2, logits, wg, wu, wd)


def _pick(limit, total):
    return min(limit, total)


def kernel(x_prompt, x_sample, norm_mix, w_in, q_norm, k_norm, lambda_q1, lambda_k1, lambda_q2, lambda_k2, sub_norm, conv_w, hg_lower, hg_norm, w_up_a, w_up_b, w_up_c, w_out, norm_ffn, w_gate_dense, w_up_dense, w_down_dense, w_router, w_gate_moe, w_up_moe, w_down_moe):
    depth = norm_mix.shape[0]
    lb_all = jnp.cumsum(jax.nn.softmax(hg_lower.astype(F32), axis=0), axis=0)
    lb_all = lb_all - lb_all[0]
    slopes = jnp.exp2(-8.0 * (jnp.arange(ATT_HEADS, dtype=F32) + 1.0) / ATT_HEADS)

    layers = []
    for l in range(depth):
        lam_init = 0.8 - 0.6 * math.exp(-0.3 * l)
        lam = (jnp.exp(jnp.sum(lambda_q1[l].astype(F32) * lambda_k1[l].astype(F32)))
               - jnp.exp(jnp.sum(lambda_q2[l].astype(F32) * lambda_k2[l].astype(F32))) + lam_init)
        wl = w_in[l]
        p = dict(
            gain=norm_mix[l].reshape(1, D_MODEL),
            w_main=jnp.concatenate([wl[:, 0:1024], wl[:, 1536:]], axis=1).astype(BF16),
            w_vt=wl[:, 1024:1536].T.astype(BF16),
            qg=(jnp.tile(q_norm[l], 8) * (ATT_HEAD_DIM ** -0.5)).reshape(1, 512),
            kg=jnp.tile(k_norm[l], 8).reshape(1, 512),
            lb=lb_all[l],
            scal=jnp.concatenate([slopes, lam.reshape(1), jnp.zeros((3,), F32)]),
            subg=(sub_norm[l] * (1.0 - lam_init)).reshape(ATT_V_DIM, 1),
            conv_w=jnp.concatenate([conv_w[l], jnp.zeros((5, SC_WIDTH), F32)], axis=0),
            hg_norm=hg_norm[l].reshape(1, HG_WIDTH),
            wa=w_up_a[l].astype(BF16), wb=w_up_b[l].astype(BF16), wc=w_up_c[l].astype(BF16),
            wo=w_out[l].astype(BF16),
            g2=norm_ffn[l].reshape(1, D_MODEL),
        )
        j = l // 2
        if l % 2 == 0:
            p.update(wg=w_gate_dense[j].astype(BF16), wu=w_up_dense[j].astype(BF16),
                     wd=w_down_dense[j].astype(BF16))
        else:
            p.update(w_router=jnp.pad(w_router[j], ((0, 0), (0, LANES - N_EXPERTS))),
                     wg=w_gate_moe[j].astype(BF16), wu=w_up_moe[j].astype(BF16), wd=w_down_moe[j].astype(BF16))
        layers.append(p)

    def trunk(x3):
        b, t, _ = x3.shape
        n = b * t
        tm = _pick(512, t)
        tq = _pick(512, t)
        tt = _pick(256, t)
        x = x3.reshape(n, D_MODEL)
        for l, p in enumerate(layers):
            q, k, vt, cv, hp, lg, sg, gates = _inproj(x, p["gain"], p["w_main"], p["w_vt"], p["qg"], p["kg"],
                                                      p["lb"], tm)
            oa = _attention(q.reshape(b, t, 512), k.reshape(b, t, 512), vt, p["scal"], p["subg"], b, t, tq, tq)
            of, ob = _hgrn(hp.reshape(b, t, 1024), lg.reshape(b, t, 512), b, t, tt)
            res = _merge(x, oa.reshape(n, 512), cv, of.reshape(n, 256), ob.reshape(n, 256), sg, gates,
                         p["conv_w"], p["hg_norm"], p["wa"], p["wb"], p["wc"], p["wo"], p["g2"],
                         p.get("w_router"), t, tm)
            if l % 2 == 0:
                xn, h2 = res
                x = _ffn(xn, h2, p["wg"], p["wu"], p["wd"], tm)
            else:
                xn, h2, logits = res
                x = _moe(xn, h2, logits, p["wg"], p["wu"], p["wd"], _pick(1024, t), p["wg"].shape[-1] // 2)
        return x.reshape(b, t, D_MODEL)

    return (trunk(x_prompt), trunk(x_sample))
```

```python
import functools
import math

import numpy as np
import jax
import jax.numpy as jnp
from jax import lax
from jax.experimental import pallas as pl
from jax.experimental.pallas import tpu as pltpu

F32 = jnp.float32
BF16 = jnp.bfloat16

D_MODEL = 1024
ATT_HEADS = 4
ATT_HEAD_DIM = 64
ATT_V_DIM = 128
ATT_QK_WIDTH = 512
ATT_WIDTH = 512
SC_WIDTH = 256
HG_WIDTH = 256
HG_HEAD_DIM = 64
HG_CHUNK = 64
N_BRANCH = 3
N_EXPERTS = 8
EPS = 1e-6

LOG2E = math.log2(math.e)
N_SLOPE_PIECES = 3
SCORE_BOUND_LIMIT = 48.0
EXP2_ZERO_ARG = 151.0

LANES = 128
POS_SHIFT = 7
VMEM_LIMIT = 56 * 1024 * 1024
NEG_BIG = -1e30

_NT = (((1,), (1,)), ((), ()))
_TN = (((0,), (0,)), ((), ()))


def _const_spec(shape):
    nd = len(shape)
    return pl.BlockSpec(shape, lambda *_: (0,) * nd, pipeline_mode=pl.Buffered(1))


def _params(sem):
    return pltpu.CompilerParams(dimension_semantics=sem, vmem_limit_bytes=VMEM_LIMIT)


def _split_bf16(x):
    hi = x.astype(BF16)
    lo = (x - hi.astype(F32)).astype(BF16)
    return hi, lo


def _group_sum(x, gmat):
    hi, lo = _split_bf16(x)
    return (jnp.dot(hi, gmat, preferred_element_type=F32) + jnp.dot(lo, gmat, preferred_element_type=F32))


def _block_ones(width, group):
    idx = np.arange(width) // group
    return jnp.asarray((idx[:, None] == idx[None, :]).astype(np.float32), dtype=BF16)


def _inproj_kernel(x_ref, g_ref, wm_ref, wvt_ref, wq_ref, wk_ref, qtab_ref, ktab_ref, gsum_ref, lb_ref,
                   q_ref, k_ref, vt_ref, cv_ref, hp_ref, lg_ref, sg_ref, gate_ref, *, tm, seq):
    x = x_ref[...]
    ms = jnp.mean(x * x, axis=-1, keepdims=True)
    h = (x * lax.rsqrt(ms + EPS) * g_ref[...]).astype(BF16)

    vt_ref[...] = lax.dot_general(wvt_ref[...], h, _NT, preferred_element_type=F32).astype(BF16)

    pos = (pl.program_id(0) * tm) % seq + lax.broadcasted_iota(jnp.int32, (tm, 1), 0)
    pos_lo = (pos & (LANES - 1)).astype(F32)
    pos_hi = (pos >> POS_SHIFT).astype(F32)
    gsum = gsum_ref[...]

    def head_norm(raw, tab_ref):
        ss = jnp.dot((raw * raw).astype(BF16), gsum, preferred_element_type=F32)
        aux = tab_ref[1:2, :] + tab_ref[2:3, :] * pos_lo + tab_ref[3:4, :] * pos_hi
        return (raw * lax.rsqrt(ss * (1.0 / ATT_HEAD_DIM) + EPS) * tab_ref[0:1, :] + aux).astype(BF16)

    q_ref[...] = head_norm(jnp.dot(h, wq_ref[...], preferred_element_type=F32), qtab_ref)
    k_ref[...] = head_norm(jnp.dot(h, wk_ref[...], preferred_element_type=F32), ktab_ref)

    sc = jnp.dot(h, wm_ref[:, 0:768], preferred_element_type=F32)
    cv_ref[:, 0:256] = sc[:, 0:256].astype(BF16)
    cv_ref[:, 256:512] = (sc[:, 256:512] * sc[:, 512:768]).astype(BF16)

    hg = jnp.dot(h, wm_ref[:, 768:2048], preferred_element_type=F32)
    hp_ref[:, 0:256] = jax.nn.silu(hg[:, 0:256]).astype(BF16)
    hp_ref[:, 256:512] = hg[:, 256:512].astype(BF16)
    for d in range(2):
        z = hg[:, 512 + 256 * d:768 + 256 * d]
        lb = lb_ref[d:d + 1, :]
        f = lb + (1.0 - lb) * jax.nn.sigmoid(z)
        hp_ref[:, 512 + 256 * d:768 + 256 * d] = ((1.0 - lb) * jax.nn.sigmoid(-z)).astype(BF16)
        lg_ref[:, 256 * d:256 * d + 256] = jnp.log(f)
    sg_ref[...] = jax.nn.silu(hg[:, 1024:1280]).astype(BF16)

    gate_ref[...] = jax.nn.sigmoid(
        jnp.dot(h, wm_ref[:, 2048:5120], preferred_element_type=F32)).astype(BF16)


def _inproj(x, gain, w_main, w_vt, w_qx, w_kx, qtab, ktab, lb, seq, tm):
    n = x.shape[0]
    row = lambda w: pl.BlockSpec((tm, w), lambda i: (i, 0))
    out_shape = (
        jax.ShapeDtypeStruct((n, 1024), BF16),
        jax.ShapeDtypeStruct((n, 1024), BF16),
        jax.ShapeDtypeStruct((512, n), BF16),
        jax.ShapeDtypeStruct((n, 512), BF16),
        jax.ShapeDtypeStruct((n, 1024), BF16),
        jax.ShapeDtypeStruct((n, 512), F32),
        jax.ShapeDtypeStruct((n, 256), BF16),
        jax.ShapeDtypeStruct((n, 3072), BF16),
    )
    kern = functools.partial(_inproj_kernel, tm=tm, seq=seq)
    return pl.pallas_call(
        kern,
        grid=(n // tm,),
        in_specs=[row(D_MODEL), _const_spec((1, D_MODEL)), _const_spec(w_main.shape), _const_spec(w_vt.shape),
                  _const_spec(w_qx.shape), _const_spec(w_kx.shape), _const_spec((8, 1024)), _const_spec((8, 1024)),
                  _const_spec((1024, 1024)), _const_spec((2, 256))],
        out_specs=(row(1024), row(1024), pl.BlockSpec((512, tm), lambda i: (0, i)), row(512), row(1024),
                   row(512), row(256), row(3072)),
        out_shape=out_shape,
        compiler_params=_params(("parallel",)),
        name="inproj",
    )(x, gain, w_main, w_vt, w_qx, w_kx, qtab, ktab, _block_ones(1024, LANES), lb)


def _alibi_tables(q_gain, k_gain):
    slopes = np.exp2(-8.0 * (np.arange(ATT_HEADS, dtype=np.float64) + 1.0) / ATT_HEADS) * LOG2E
    qc = np.zeros((8, 1024), np.float32)
    kc = np.zeros((8, 1024), np.float32)
    for h in range(ATT_HEADS):
        rest = np.float32(slopes[h])
        pieces = []
        for _ in range(N_SLOPE_PIECES):
            piece = np.asarray(rest, dtype=BF16).astype(np.float32)
            pieces.append(float(piece))
            rest = np.float32(rest - piece)
        for c in range(2):
            base = LANES * (2 * h + c) + ATT_HEAD_DIM
            for i, a in enumerate(pieces):
                qc[1, base + 2 * i] = -a
                qc[1, base + 2 * i + 1] = -a * LANES
                kc[2, base + 2 * i] = 1.0
                kc[3, base + 2 * i + 1] = 1.0
                off = base + 2 * N_SLOPE_PIECES
                qc[2, off + 2 * i] = 1.0
                qc[3, off + 2 * i + 1] = 1.0
                kc[1, off + 2 * i] = a
                kc[1, off + 2 * i + 1] = a * LANES
    lane = np.arange(1024) % LANES
    lane_is_data = lane < ATT_HEAD_DIM
    qg = jnp.where(lane_is_data, jnp.tile(q_gain, 16) * (LOG2E * ATT_HEAD_DIM ** -0.5), 0.0)
    kg = jnp.where(lane_is_data, jnp.tile(k_gain, 16), 0.0)
    bound = 1.02 * LOG2E * ATT_HEAD_DIM ** 0.5 * jnp.max(jnp.abs(q_gain)) * jnp.max(jnp.abs(k_gain))
    qconst = jnp.asarray(qc[1])
    rest = bound.astype(F32)
    bound_lane0 = ATT_HEAD_DIM + 4 * N_SLOPE_PIECES
    for i in range(N_SLOPE_PIECES):
        piece = rest.astype(BF16).astype(F32)
        rest = rest - piece
        qconst = jnp.where(lane == bound_lane0 + i, -piece, qconst)
        kc[1, lane == bound_lane0 + i] = 1.0
    qtab = jnp.asarray(qc).at[0].set(qg).at[1].set(qconst)
    ktab = jnp.asarray(kc).at[0].set(kg)
    return qtab, ktab, bound <= SCORE_BOUND_LIMIT


def _expand_heads(w):
    d = w.shape[0]
    w = w.reshape(d, 8, ATT_HEAD_DIM)
    return jnp.concatenate([w, jnp.zeros_like(w)], axis=-1).reshape(d, 1024)


def _attn_finish(sc_ref, subg_ref, o_ref, l0, l1, acc_ref):
    lam = sc_ref[ATT_HEADS]
    o = acc_ref[0] / l0 - lam * (acc_ref[1] / l1)
    ms = jnp.mean(o * o, axis=0, keepdims=True)
    o = o * lax.rsqrt(ms + EPS) * subg_ref[...]
    o_ref[...] = o.T.astype(BF16)


def _attn_fast_kernel(sc_ref, q_ref, k_ref, vt_ref, absd_ref, subg_ref, o_ref, qv_ref, l_ref, acc_ref,
                      *, tq, tk, nk, window):
    h = pl.program_id(1)
    qi = pl.program_id(2)
    st = pl.program_id(3)
    after = qi + st < nk
    tile_dist = jnp.where(after, st, nk - st)
    reach = jnp.int32(window[ATT_HEADS - 1])
    for hh in range(ATT_HEADS - 2, -1, -1):
        reach = jnp.where(h == hh, jnp.int32(window[hh]), reach)

    @pl.when(st == 0)
    def _():
        lane = lax.broadcasted_iota(jnp.int32, (tq, LANES), 1)
        alibi = (lane >= ATT_HEAD_DIM) & (lane < ATT_HEAD_DIM + 4 * N_SLOPE_PIECES)
        for c in range(2):
            qc = q_ref[:, LANES * c:LANES * (c + 1)]
            qv_ref[c] = qc
            qv_ref[2 + c] = jnp.where(alibi, -qc, qc)
            qv_ref[4 + c] = jnp.where(alibi, jnp.zeros_like(qc), qc)
        l_ref[...] = jnp.zeros(l_ref.shape, F32)
        acc_ref[...] = jnp.zeros(acc_ref.shape, F32)

    def step(diag):
        vt = vt_ref[...]
        for c in range(2):
            kc = k_ref[:, LANES * c:LANES * (c + 1)]
            if diag:
                qsel = qv_ref[4 + c]
            else:
                qsel = qv_ref[jnp.where(after, 0, 2) + c]
            s = lax.dot_general(kc, qsel, _NT, preferred_element_type=F32)
            if diag:
                s = s - sc_ref[h] * absd_ref[...]
            p = jnp.exp2(s)
            l_ref[c] += jnp.sum(p.reshape(tk // 8, 8, tq), axis=0)
            acc_ref[c] += jnp.dot(vt, p.astype(BF16), preferred_element_type=F32)

    @pl.when(st == 0)
    def _():
        step(True)

    @pl.when((st != 0) & (tile_dist <= reach))
    def _():
        step(False)

    @pl.when(st == nk - 1)
    def _():
        _attn_finish(sc_ref, subg_ref, o_ref, jnp.sum(l_ref[0], axis=0, keepdims=True),
                     jnp.sum(l_ref[1], axis=0, keepdims=True), acc_ref)


def _attn_safe_kernel(sc_ref, q_ref, k_ref, vt_ref, subg_ref, o_ref, qm_ref, m_ref, l_ref, acc_ref, *, tq, tk, nk):
    h = pl.program_id(1)
    qi = pl.program_id(2)
    ki = pl.program_id(3)

    @pl.when(ki == 0)
    def _():
        lane = lax.broadcasted_iota(jnp.int32, (tq, LANES), 1)
        for c in range(2):
            qc = q_ref[:, LANES * c:LANES * (c + 1)]
            qm_ref[c] = jnp.where(lane < ATT_HEAD_DIM, qc, jnp.zeros_like(qc))
        m_ref[...] = jnp.full(m_ref.shape, NEG_BIG, F32)
        l_ref[...] = jnp.zeros(l_ref.shape, F32)
        acc_ref[...] = jnp.zeros(acc_ref.shape, F32)

    kpos = ki * tk + lax.broadcasted_iota(jnp.int32, (tk, tq), 0)
    qpos = qi * tq + lax.broadcasted_iota(jnp.int32, (tk, tq), 1)
    bias = sc_ref[h] * jnp.abs(kpos - qpos).astype(F32)
    vt = vt_ref[...]
    for c in range(2):
        kc = k_ref[:, LANES * c:LANES * (c + 1)]
        s = lax.dot_general(kc, qm_ref[c], _NT, preferred_element_type=F32) - bias
        m_prev = m_ref[c]
        m_new = jnp.maximum(m_prev, jnp.max(s, axis=0, keepdims=True))
        alpha = jnp.exp2(m_prev - m_new)
        p = jnp.exp2(s - m_new)
        l_ref[c] = alpha * l_ref[c] + jnp.sum(p, axis=0, keepdims=True)
        acc_ref[c] = alpha * acc_ref[c] + jnp.dot(vt, p.astype(BF16), preferred_element_type=F32)
        m_ref[c] = m_new

    @pl.when(ki == nk - 1)
    def _():
        _attn_finish(sc_ref, subg_ref, o_ref, l_ref[0], l_ref[1], acc_ref)


def _attention(q, k, vt, scal, subg, fast_ok, b, t, tq, tk):
    nq, nk = t // tq, t // tk
    blk = 2 * LANES
    common = dict(
        grid=(b, ATT_HEADS, nq, nk),
        out_specs=pl.BlockSpec((None, tq, LANES), lambda bi, h, qi, ki: (bi, qi, h)),
        out_shape=jax.ShapeDtypeStruct((b, t, ATT_WIDTH), BF16),
        compiler_params=_params(("parallel", "parallel", "parallel", "arbitrary")),
    )
    qkv_specs = [
        pl.BlockSpec((None, tq, blk), lambda bi, h, qi, ki: (bi, qi, h)),
        pl.BlockSpec((None, tk, blk), lambda bi, h, qi, ki: (bi, ki, h)),
        pl.BlockSpec((LANES, tk), lambda bi, h, qi, ki: (h, bi * nk + ki)),
    ]
    smem = pl.BlockSpec(memory_space=pltpu.SMEM)
    idx = np.arange(tk)[:, None] - np.arange(tq)[None, :]
    absd = jnp.asarray(np.abs(idx), dtype=F32)
    slopes2 = np.exp2(-8.0 * (np.arange(ATT_HEADS) + 1.0) / ATT_HEADS) * LOG2E
    window = tuple(int((EXP2_ZERO_ARG / s - 1.0) // tk) + 1 for s in slopes2)
    rot_specs = [
        pl.BlockSpec((None, tq, blk), lambda bi, h, qi, st: (bi, qi, h)),
        pl.BlockSpec((None, tk, blk), lambda bi, h, qi, st: (bi, (qi + st) % nk, h)),
        pl.BlockSpec((LANES, tk), lambda bi, h, qi, st: (h, bi * nk + (qi + st) % nk)),
    ]

    def fast(q, k, vt, scal, subg):
        return pl.pallas_call(
            functools.partial(_attn_fast_kernel, tq=tq, tk=tk, nk=nk, window=window),
            in_specs=[smem] + rot_specs + [_const_spec((tk, tq)), _const_spec((ATT_V_DIM, 1))],
            scratch_shapes=[
                pltpu.VMEM((6, tq, LANES), BF16),
                pltpu.VMEM((2, 8, tq), F32),
                pltpu.VMEM((2, ATT_V_DIM, tq), F32),
            ],
            name="diff_attn",
            **common,
        )(scal, q, k, vt, absd, subg)

    def safe(q, k, vt, scal, subg):
        return pl.pallas_call(
            functools.partial(_attn_safe_kernel, tq=tq, tk=tk, nk=nk),
            in_specs=[smem] + qkv_specs + [_const_spec((ATT_V_DIM, 1))],
            scratch_shapes=[
                pltpu.VMEM((2, tq, LANES), BF16),
                pltpu.VMEM((2, 1, tq), F32),
                pltpu.VMEM((2, 1, tq), F32),
                pltpu.VMEM((2, ATT_V_DIM, tq), F32),
            ],
            name="diff_attn_safe",
            **common,
        )(scal, q, k, vt, subg)

    return lax.cond(fast_ok, fast, safe, q, k, vt, scal, subg)


N_LEVELS = 6


def _hgrn_constants(tt):
    c = HG_CHUNK
    t = np.arange(tt)
    same_chunk = (t[:, None] // c) == (t[None, :] // c)
    mats, masks = [], []
    masks.append(np.eye(tt, dtype=np.float32))
    for j in range(N_LEVELS):
        m = 1 << j
        blk = t // m
        same_blk = blk[:, None] == blk[None, :]
        odd = (blk % 2) == 1
        incl = same_blk & (t[None, :] <= t[:, None])
        excl_rev = same_blk & (t[None, :] > t[:, None])
        mats.append(np.where(odd[:, None], incl, excl_rev))
        same_2m = (t[:, None] // (2 * m)) == (t[None, :] // (2 * m))
        masks.append((same_2m & odd[:, None] & (~odd)[None, :]).astype(np.float32))
    mats.append(same_chunk & (t[None, :] <= t[:, None]))
    mats.append(same_chunk & (t[None, :] > t[:, None]))
    mats = np.stack(mats).astype(np.float32)
    masks = np.stack(masks)
    mats_b = mats[:, ::-1, ::-1]
    masks_b = masks[:, ::-1, ::-1]
    mst = np.concatenate([mats.reshape(-1, tt), mats_b.reshape(-1, tt)], axis=0)
    amask = np.concatenate([masks, masks_b], axis=0)
    return jnp.asarray(mst, dtype=BF16), jnp.asarray(amask, dtype=F32)


def _hgrn_kernel(hpf_ref, hpb_ref, lgf_ref, lgb_ref, mst_ref, amask_ref, of_ref, ob_ref, s_ref, *, tt):
    i = pl.program_id(1)
    nlev = N_LEVELS + 1
    nmat = N_LEVELS + 2
    nchunk = tt // HG_CHUNK

    @pl.when(i == 0)
    def _():
        s_ref[...] = jnp.zeros(s_ref.shape, F32)

    lane256 = lax.broadcasted_iota(jnp.int32, (1, 256), 1)
    lane128 = lax.broadcasted_iota(jnp.int32, (1, LANES), 1)
    r128 = lax.broadcasted_iota(jnp.int32, (LANES, LANES), 0)
    c128 = lax.broadcasted_iota(jnp.int32, (LANES, LANES), 1)
    bdmask = ((r128 // HG_HEAD_DIM) == (c128 // HG_HEAD_DIM)).astype(F32)

    def direction(d, hp_ref, lg_ref, o_ref):
        qs = hp_ref[:, 0:256].astype(F32)
        v = hp_ref[:, 256:512]
        kk = hp_ref[:, 512 + 256 * d:768 + 256 * d].astype(F32)
        hi, lo = _split_bf16(lg_ref[...])
        hl = jnp.concatenate([hi, lo], axis=1)
        eall = jnp.dot(mst_ref[d * nmat * tt:(d + 1) * nmat * tt, :], hl, preferred_element_type=F32)

        def expo(j):
            r = eall[j * tt:(j + 1) * tt]
            return jnp.exp(r[:, 0:256] + r[:, 256:512])

        a = [None] * 4
        for lev in range(nlev):
            if lev == 0:
                qt, kt = qs.astype(BF16), kk.astype(BF16)
            else:
                xx = expo(lev - 1)
                qt, kt = (qs * xx).astype(BF16), (kk * xx).astype(BF16)
            am = amask_ref[d * nlev + lev]
            for pr in range(2):
                qp = qt[:, LANES * pr:LANES * (pr + 1)]
                kp = kt[:, LANES * pr:LANES * (pr + 1)]
                for hh in range(2):
                    sel = (lane128 // HG_HEAD_DIM) == hh
                    qm = jnp.where(sel, qp, jnp.zeros_like(qp))
                    p = lax.dot_general(qm, kp, _NT, preferred_element_type=F32) * am
                    idx = 2 * pr + hh
                    a[idx] = p if a[idx] is None else a[idx] + p

        o = None
        for idx in range(4):
            vm = jnp.where((lane256 // HG_HEAD_DIM) == idx, v, jnp.zeros_like(v))
            contrib = jnp.dot(a[idx].astype(BF16), vm, preferred_element_type=F32)
            o = contrib if o is None else o + contrib
        o_ref[...] = o

        xq = expo(N_LEVELS)
        xk = expo(N_LEVELS + 1)
        qc = (qs * xq).astype(BF16)
        kc = (kk * xk).astype(BF16)
        order = range(nchunk) if d == 0 else range(nchunk - 1, -1, -1)
        for c in order:
            r0 = c * HG_CHUNK
            rows = slice(r0, r0 + HG_CHUNK)
            drow = r0 + HG_CHUNK - 1 if d == 0 else r0
            for pr in range(2):
                cols = slice(LANES * pr, LANES * (pr + 1))
                st = s_ref[d, pr]
                inter = lax.dot_general(qc[rows, cols], st.astype(BF16), _NT, preferred_element_type=F32)
                o_ref[rows, cols] += inter
                ut = lax.dot_general(v[rows, cols], kc[rows, cols], _TN, preferred_element_type=F32)
                dec = xq[drow:drow + 1, cols]
                s_ref[d, pr] = st * dec + ut * bdmask

    direction(0, hpf_ref, lgf_ref, of_ref)
    direction(1, hpb_ref, lgb_ref, ob_ref)


def _hgrn(hp, lg, b, t, tt):
    nt = t // tt
    mst, amask = _hgrn_constants(tt)
    kern = functools.partial(_hgrn_kernel, tt=tt)
    fwd = lambda w, cb: pl.BlockSpec((None, tt, w), lambda bi, i: (bi, i, cb))
    bwd = lambda w, cb: pl.BlockSpec((None, tt, w), lambda bi, i: (bi, nt - 1 - i, cb))
    return pl.pallas_call(
        kern,
        grid=(b, nt),
        in_specs=[fwd(1024, 0), bwd(1024, 0), fwd(256, 0), bwd(256, 1),
                  _const_spec(mst.shape), _const_spec(amask.shape)],
        out_specs=(fwd(256, 0), bwd(256, 0)),
        out_shape=(jax.ShapeDtypeStruct((b, t, HG_WIDTH), F32), jax.ShapeDtypeStruct((b, t, HG_WIDTH), F32)),
        scratch_shapes=[pltpu.VMEM((2, 2, LANES, LANES), F32)],
        compiler_params=_params(("parallel", "arbitrary")),
        name="hgrn2",
    )(hp, hp, lg, lg, mst, amask)


def _merge_kernel(*refs, tm, seq, with_router):
    (x_ref, oa_ref, cv_ref, cvp_ref, cvn_ref, of_ref, ob_ref, sg_ref, gate_ref, cw_ref, hgn_ref, g256_ref,
     wa_ref, wb_ref, wc_ref, wo_ref, g2_ref) = refs[:17]
    if with_router:
        wr_ref, xo_ref, h2_ref, lo_ref = refs[17:]
    else:
        xo_ref, h2_ref = refs[17:]
    i = pl.program_id(0)
    tile_start = (i * tm) % seq

    u = cv_ref[:, 256:512].astype(F32)
    row = lax.broadcasted_iota(jnp.int32, (tm, 1), 0)
    prev_row = jnp.where(tile_start == 0, 0.0, cvp_ref[7:8, 256:512].astype(F32))
    next_row = jnp.where(tile_start + tm == seq, 0.0, cvn_ref[0:1, 256:512].astype(F32))
    u_m1 = jnp.where(row == 0, prev_row, pltpu.roll(u, 1, axis=0))
    u_p1 = jnp.where(row == tm - 1, next_row, pltpu.roll(u, tm - 1, axis=0))
    conv = cw_ref[0:1, :] * u_m1 + cw_ref[1:2, :] * u + cw_ref[2:3, :] * u_p1
    yb_in = (cv_ref[:, 0:256].astype(F32) * conv).astype(BF16)

    o = of_ref[...] + ob_ref[...]
    ss = _group_sum(o * o, g256_ref[...])
    yc_in = (o * lax.rsqrt(ss * (1.0 / HG_HEAD_DIM) + EPS) * hgn_ref[...] * sg_ref[...].astype(F32)).astype(BF16)

    ya = jnp.dot(oa_ref[...], wa_ref[...], preferred_element_type=F32)
    yb = jnp.dot(yb_in, wb_ref[...], preferred_element_type=F32)
    yc = jnp.dot(yc_in, wc_ref[...], preferred_element_type=F32)
    merged = (gate_ref[:, 0:1024].astype(F32) * ya + gate_ref[:, 1024:2048].astype(F32) * yb
              + gate_ref[:, 2048:3072].astype(F32) * yc)
    xn = x_ref[...] + jnp.dot(merged.astype(BF16), wo_ref[...], preferred_element_type=F32)
    xo_ref[...] = xn

    ms = jnp.mean(xn * xn, axis=-1, keepdims=True)
    h2 = xn * lax.rsqrt(ms + EPS) * g2_ref[...]
    h2_ref[...] = h2.astype(BF16)
    if with_router:
        lo_ref[...] = lax.dot_general(wr_ref[...], h2, _NT, preferred_element_type=F32,
                                      precision=lax.Precision.HIGHEST)


def _merge(x, oa, cv, of, ob, sg, gates, conv_w, hg_norm, wa, wb, wc, wo, g2, w_router, seq, tm):
    n = x.shape[0]
    nb8 = n // 8
    step8 = tm // 8
    row = lambda w: pl.BlockSpec((tm, w), lambda i: (i, 0))
    in_specs = [
        row(D_MODEL), row(512), row(512),
        pl.BlockSpec((8, 512), lambda i: (jnp.maximum(i * step8 - 1, 0), 0)),
        pl.BlockSpec((8, 512), lambda i: (jnp.minimum((i + 1) * step8, nb8 - 1), 0)),
        row(256), row(256), row(256), row(3072),
        _const_spec((8, 256)), _const_spec((1, 256)), _const_spec((256, 256)),
        _const_spec(wa.shape), _const_spec(wb.shape), _const_spec(wc.shape), _const_spec(wo.shape),
        _const_spec((1, D_MODEL)),
    ]
    args = [x, oa, cv, cv, cv, of, ob, sg, gates, conv_w, hg_norm, _block_ones(256, HG_HEAD_DIM),
            wa, wb, wc, wo, g2]
    out_specs = [row(D_MODEL), row(D_MODEL)]
    out_shape = [jax.ShapeDtypeStruct((n, D_MODEL), F32), jax.ShapeDtypeStruct((n, D_MODEL), BF16)]
    with_router = w_router is not None
    if with_router:
        in_specs.append(_const_spec(w_router.shape))
        args.append(w_router)
        out_specs.append(pl.BlockSpec((N_EXPERTS, tm), lambda i: (0, i)))
        out_shape.append(jax.ShapeDtypeStruct((N_EXPERTS, n), F32))
    kern = functools.partial(_merge_kernel, tm=tm, seq=seq, with_router=with_router)
    return pl.pallas_call(
        kern,
        grid=(n // tm,),
        in_specs=in_specs,
        out_specs=tuple(out_specs),
        out_shape=tuple(out_shape),
        compiler_params=_params(("parallel",)),
        name="merge_router" if with_router else "merge",
    )(*args)


def _ffn_kernel(x_ref, h_ref, wg_ref, wu_ref, wd_ref, o_ref):
    h = h_ref[...]
    g = jnp.dot(h, wg_ref[...], preferred_element_type=F32)
    u = jnp.dot(h, wu_ref[...], preferred_element_type=F32)
    act = (jax.nn.silu(g) * u).astype(BF16)
    o_ref[...] = x_ref[...] + jnp.dot(act, wd_ref[...], preferred_element_type=F32)


def _ffn(x, h2, wg, wu, wd, tm):
    n = x.shape[0]
    row = lambda: pl.BlockSpec((tm, D_MODEL), lambda i: (i, 0))
    return pl.pallas_call(
        _ffn_kernel,
        grid=(n // tm,),
        in_specs=[row(), row(), _const_spec(wg.shape), _const_spec(wu.shape), _const_spec(wd.shape)],
        out_specs=row(),
        out_shape=jax.ShapeDtypeStruct((n, D_MODEL), F32),
        compiler_params=_params(("parallel",)),
        name="ffn_dense",
    )(x, h2, wg, wu, wd)


MOE_ROWS = 128


def _moe_expert_kernel(lo_ref, h_ref, y_ref, tri_ref, wg_ref, wu_ref, wd_ref, o_ref, *, expert, tm):
    lg = lo_ref[...]
    sub = lax.broadcasted_iota(jnp.int32, lg.shape, 0)
    m1 = jnp.max(lg, axis=0, keepdims=True)
    i1 = jnp.min(jnp.where(lg == m1, sub, N_EXPERTS), axis=0, keepdims=True)
    lg2 = jnp.where(sub == i1, NEG_BIG, lg)
    m2 = jnp.max(lg2, axis=0, keepdims=True)
    i2 = jnp.min(jnp.where(lg2 == m2, sub, N_EXPERTS), axis=0, keepdims=True)
    e2 = jnp.exp(m2 - m1)
    w_row = jnp.where(i1 == expert, 1.0 / (1.0 + e2), 0.0) + jnp.where(i2 == expert, e2 / (1.0 + e2), 0.0)
    routed = (i1 == expert) | (i2 == expert)
    routed_f = routed.astype(F32)
    csum = jnp.dot(jnp.broadcast_to(routed_f, (8, tm)).astype(BF16), tri_ref[...], preferred_element_type=F32)
    rank = csum[0:1, :].astype(jnp.int32) - 1
    count = jnp.sum(routed_f).astype(jnp.int32)

    o_ref[...] = y_ref[...]
    h = h_ref[...]

    def body(blk, carry):
        rows = blk * MOE_ROWS + lax.broadcasted_iota(jnp.int32, (MOE_ROWS, tm), 0)
        sel = (rank == rows) & routed
        sel_b = sel.astype(F32).astype(BF16)
        xe = jnp.dot(sel_b, h, preferred_element_type=F32).astype(BF16)
        g = jnp.dot(xe, wg_ref[...], preferred_element_type=F32)
        u = jnp.dot(xe, wu_ref[...], preferred_element_type=F32)
        act = (jax.nn.silu(g) * u).astype(BF16)
        ye = jnp.dot(act, wd_ref[...], preferred_element_type=F32)
        w_sel = jnp.sum(jnp.where(sel, w_row, 0.0), axis=1, keepdims=True)
        ye = (ye * w_sel).astype(BF16)
        o_ref[...] += lax.dot_general(sel_b, ye, _TN, preferred_element_type=F32)
        return carry

    lax.fori_loop(0, (count + MOE_ROWS - 1) // MOE_ROWS, body, 0)


def _moe(x, h2, logits_t, wg, wu, wd, tm):
    n = x.shape[0]
    row = lambda: pl.BlockSpec((tm, D_MODEL), lambda i: (i, 0))
    t = np.arange(tm)
    tri = jnp.asarray((t[:, None] <= t[None, :]).astype(np.float32), dtype=BF16)
    expert_w = lambda w, e: pl.BlockSpec((None,) + w.shape[1:], lambda i: (e, 0, 0), pipeline_mode=pl.Buffered(1))
    y = x
    for e in range(N_EXPERTS):
        y = pl.pallas_call(
            functools.partial(_moe_expert_kernel, expert=e, tm=tm),
            grid=(n // tm,),
            in_specs=[pl.BlockSpec((N_EXPERTS, tm), lambda i: (0, i)), row(), row(), _const_spec((tm, tm)),
                      expert_w(wg, e), expert_w(wu, e), expert_w(wd, e)],
            out_specs=row(),
            out_shape=jax.ShapeDtypeStruct((n, D_MODEL), F32),
            input_output_aliases={2: 0},
            compiler_params=_params(("parallel",)),
            name="moe_expert",
        )(logits_t, h2, y, tri, wg, wu, wd)
    return y


def _pick(limit, total):
    return min(limit, total)


def kernel(x_prompt, x_sample, norm_mix, w_in, q_norm, k_norm, lambda_q1, lambda_k1, lambda_q2, lambda_k2, sub_norm, conv_w, hg_lower, hg_norm, w_up_a, w_up_b, w_up_c, w_out, norm_ffn, w_gate_dense, w_up_dense, w_down_dense, w_router, w_gate_moe, w_up_moe, w_down_moe):
    depth = norm_mix.shape[0]
    lb_all = jnp.cumsum(jax.nn.softmax(hg_lower.astype(F32), axis=0), axis=0)
    lb_all = lb_all - lb_all[0]
    slopes = jnp.exp2(-8.0 * (jnp.arange(ATT_HEADS, dtype=F32) + 1.0) / ATT_HEADS)

    layers = []
    for l in range(depth):
        lam_init = 0.8 - 0.6 * math.exp(-0.3 * l)
        lam = (jnp.exp(jnp.sum(lambda_q1[l].astype(F32) * lambda_k1[l].astype(F32)))
               - jnp.exp(jnp.sum(lambda_q2[l].astype(F32) * lambda_k2[l].astype(F32))) + lam_init)
        wl = w_in[l]
        qtab, ktab, fast_ok = _alibi_tables(q_norm[l].astype(F32), k_norm[l].astype(F32))
        p = dict(
            gain=norm_mix[l].reshape(1, D_MODEL),
            w_main=wl[:, 1536:].astype(BF16),
            w_vt=wl[:, 1024:1536].T.astype(BF16),
            w_qx=_expand_heads(wl[:, 0:512]).astype(BF16),
            w_kx=_expand_heads(wl[:, 512:1024]).astype(BF16),
            qtab=qtab, ktab=ktab, fast_ok=fast_ok,
            lb=lb_all[l],
            scal=jnp.concatenate([slopes * LOG2E, lam.reshape(1), jnp.zeros((3,), F32)]),
            subg=(sub_norm[l] * (1.0 - lam_init)).reshape(ATT_V_DIM, 1),
            conv_w=jnp.concatenate([conv_w[l], jnp.zeros((5, SC_WIDTH), F32)], axis=0),
            hg_norm=hg_norm[l].reshape(1, HG_WIDTH),
            wa=w_up_a[l].astype(BF16), wb=w_up_b[l].astype(BF16), wc=w_up_c[l].astype(BF16),
            wo=w_out[l].astype(BF16),
            g2=norm_ffn[l].reshape(1, D_MODEL),
        )
        j = l // 2
        if l % 2 == 0:
            p.update(wg=w_gate_dense[j].astype(BF16), wu=w_up_dense[j].astype(BF16),
                     wd=w_down_dense[j].astype(BF16))
        else:
            p.update(w_router=w_router[j].T.astype(F32),
                     wg=w_gate_moe[j].astype(BF16), wu=w_up_moe[j].astype(BF16), wd=w_down_moe[j].astype(BF16))
        layers.append(p)

    def trunk(x3):
        b, t, _ = x3.shape
        n = b * t
        tm = _pick(512, t)
        tq = _pick(1024, t)
        tt = _pick(256, t)
        x = x3.reshape(n, D_MODEL)
        for l, p in enumerate(layers):
            q, k, vt, cv, hp, lg, sg, gates = _inproj(x, p["gain"], p["w_main"], p["w_vt"], p["w_qx"], p["w_kx"],
                                                      p["qtab"], p["ktab"], p["lb"], t, tm)
            oa = _attention(q.reshape(b, t, 1024), k.reshape(b, t, 1024), vt, p["scal"], p["subg"], p["fast_ok"],
                            b, t, tq, tq)
            of, ob = _hgrn(hp.reshape(b, t, 1024), lg.reshape(b, t, 512), b, t, tt)
            res = _merge(x, oa.reshape(n, 512), cv, of.reshape(n, 256), ob.reshape(n, 256), sg, gates,
                         p["conv_w"], p["hg_norm"], p["wa"], p["wb"], p["wc"], p["wo"], p["g2"],
                         p.get("w_router"), t, tm)
            if l % 2 == 0:
                xn, h2 = res
                x = _ffn(xn, h2, p["wg"], p["wu"], p["wd"], tm)
            else:
                xn, h2, logits = res
                x = _moe(xn, h2, logits, p["wg"], p["wu"], p["wd"], _pick(1024, t))
        return x.reshape(b, t, D_MODEL)

    return (trunk(x_prompt), trunk(x_sample))
```

```python
import functools
import math

import numpy as np
import jax
import jax.numpy as jnp
from jax import lax
from jax.experimental import pallas as pl
from jax.experimental.pallas import tpu as pltpu

F32 = jnp.float32
BF16 = jnp.bfloat16

D_MODEL = 1024
ATT_HEADS = 4
ATT_HEAD_DIM = 64
ATT_V_DIM = 128
ATT_QK_WIDTH = 512
ATT_WIDTH = 512
SC_WIDTH = 256
HG_WIDTH = 256
HG_HEAD_DIM = 64
HG_CHUNK = 64
N_BRANCH = 3
N_EXPERTS = 8
EPS = 1e-6

LOG2E = math.log2(math.e)
N_SLOPE_PIECES = 3
SCORE_BOUND_LIMIT = 48.0
EXP2_ZERO_ARG = 151.0

LANES = 128
POS_SHIFT = 7
VMEM_LIMIT = 56 * 1024 * 1024
NEG_BIG = -1e30

_NT = (((1,), (1,)), ((), ()))
_TN = (((0,), (0,)), ((), ()))


def _const_spec(shape):
    nd = len(shape)
    return pl.BlockSpec(shape, lambda *_: (0,) * nd, pipeline_mode=pl.Buffered(1))


def _params(sem):
    return pltpu.CompilerParams(dimension_semantics=sem, vmem_limit_bytes=VMEM_LIMIT)


def _split_bf16(x):
    hi = x.astype(BF16)
    lo = (x - hi.astype(F32)).astype(BF16)
    return hi, lo


def _group_sum(x, gmat):
    hi, lo = _split_bf16(x)
    return (jnp.dot(hi, gmat, preferred_element_type=F32) + jnp.dot(lo, gmat, preferred_element_type=F32))


def _block_ones(width, group):
    idx = np.arange(width) // group
    return jnp.asarray((idx[:, None] == idx[None, :]).astype(np.float32), dtype=BF16)


def _inproj_kernel(x_ref, g_ref, wm_ref, wvt_ref, wq_ref, wk_ref, qtab_ref, ktab_ref, lb_ref,
                   q_ref, k_ref, vt_ref, cv_ref, hp_ref, lg_ref, sg_ref, gate_ref, *, tm, seq):
    x = x_ref[...]
    ms = jnp.mean(x * x, axis=-1, keepdims=True)
    h = (x * lax.rsqrt(ms + EPS) * g_ref[...]).astype(BF16)

    vt_ref[...] = lax.dot_general(wvt_ref[...], h, _NT, preferred_element_type=F32).astype(BF16)

    pos = (pl.program_id(0) * tm) % seq + lax.broadcasted_iota(jnp.int32, (tm, 1), 0)
    pos_lo = (pos & (LANES - 1)).astype(F32)
    pos_hi = (pos >> POS_SHIFT).astype(F32)

    def head_norm(raw, tab_ref, out_ref):
        for j in range(2 * ATT_HEADS):
            cols = slice(LANES * j, LANES * (j + 1))
            blk = raw[:, cols]
            ss = jnp.sum(blk * blk, axis=-1, keepdims=True)
            aux = tab_ref[1:2, cols] + tab_ref[2:3, cols] * pos_lo + tab_ref[3:4, cols] * pos_hi
            out_ref[:, cols] = (blk * lax.rsqrt(ss * (1.0 / ATT_HEAD_DIM) + EPS) * tab_ref[0:1, cols]
                                + aux).astype(BF16)

    head_norm(jnp.dot(h, wq_ref[...], preferred_element_type=F32), qtab_ref, q_ref)
    head_norm(jnp.dot(h, wk_ref[...], preferred_element_type=F32), ktab_ref, k_ref)

    sc = jnp.dot(h, wm_ref[:, 0:768], preferred_element_type=F32)
    cv_ref[:, 0:256] = sc[:, 0:256].astype(BF16)
    cv_ref[:, 256:512] = (sc[:, 256:512] * sc[:, 512:768]).astype(BF16)

    hg = jnp.dot(h, wm_ref[:, 768:2048], preferred_element_type=F32)
    hp_ref[:, 0:256] = jax.nn.silu(hg[:, 0:256]).astype(BF16)
    hp_ref[:, 256:512] = hg[:, 256:512].astype(BF16)
    for d in range(2):
        z = hg[:, 512 + 256 * d:768 + 256 * d]
        lb = lb_ref[d:d + 1, :]
        f = lb + (1.0 - lb) * jax.nn.sigmoid(z)
        hp_ref[:, 512 + 256 * d:768 + 256 * d] = ((1.0 - lb) * jax.nn.sigmoid(-z)).astype(BF16)
        lg_ref[:, 256 * d:256 * d + 256] = jnp.log(f)
    sg_ref[...] = jax.nn.silu(hg[:, 1024:1280]).astype(BF16)

    gate_ref[...] = jax.nn.sigmoid(
        jnp.dot(h, wm_ref[:, 2048:5120], preferred_element_type=F32)).astype(BF16)


def _inproj(x, gain, w_main, w_vt, w_qx, w_kx, qtab, ktab, lb, seq, tm):
    n = x.shape[0]
    row = lambda w: pl.BlockSpec((tm, w), lambda i: (i, 0))
    out_shape = (
        jax.ShapeDtypeStruct((n, 1024), BF16),
        jax.ShapeDtypeStruct((n, 1024), BF16),
        jax.ShapeDtypeStruct((512, n), BF16),
        jax.ShapeDtypeStruct((n, 512), BF16),
        jax.ShapeDtypeStruct((n, 1024), BF16),
        jax.ShapeDtypeStruct((n, 512), F32),
        jax.ShapeDtypeStruct((n, 256), BF16),
        jax.ShapeDtypeStruct((n, 3072), BF16),
    )
    kern = functools.partial(_inproj_kernel, tm=tm, seq=seq)
    return pl.pallas_call(
        kern,
        grid=(n // tm,),
        in_specs=[row(D_MODEL), _const_spec((1, D_MODEL)), _const_spec(w_main.shape), _const_spec(w_vt.shape),
                  _const_spec(w_qx.shape), _const_spec(w_kx.shape), _const_spec((8, 1024)), _const_spec((8, 1024)),
                  _const_spec((2, 256))],
        out_specs=(row(1024), row(1024), pl.BlockSpec((512, tm), lambda i: (0, i)), row(512), row(1024),
                   row(512), row(256), row(3072)),
        out_shape=out_shape,
        compiler_params=_params(("parallel",)),
        name="inproj",
    )(x, gain, w_main, w_vt, w_qx, w_kx, qtab, ktab, lb)


def _alibi_tables(q_gain, k_gain):
    slopes = np.exp2(-8.0 * (np.arange(ATT_HEADS, dtype=np.float64) + 1.0) / ATT_HEADS) * LOG2E
    qc = np.zeros((8, 1024), np.float32)
    kc = np.zeros((8, 1024), np.float32)
    for h in range(ATT_HEADS):
        rest = np.float32(slopes[h])
        pieces = []
        for _ in range(N_SLOPE_PIECES):
            piece = np.asarray(rest, dtype=BF16).astype(np.float32)
            pieces.append(float(piece))
            rest = np.float32(rest - piece)
        for c in range(2):
            base = LANES * (2 * h + c) + ATT_HEAD_DIM
            for i, a in enumerate(pieces):
                qc[1, base + 2 * i] = -a
                qc[1, base + 2 * i + 1] = -a * LANES
                kc[2, base + 2 * i] = 1.0
                kc[3, base + 2 * i + 1] = 1.0
                off = base + 2 * N_SLOPE_PIECES
                qc[2, off + 2 * i] = 1.0
                qc[3, off + 2 * i + 1] = 1.0
                kc[1, off + 2 * i] = a
                kc[1, off + 2 * i + 1] = a * LANES
    lane = np.arange(1024) % LANES
    lane_is_data = lane < ATT_HEAD_DIM
    qg = jnp.where(lane_is_data, jnp.tile(q_gain, 16) * (LOG2E * ATT_HEAD_DIM ** -0.5), 0.0)
    kg = jnp.where(lane_is_data, jnp.tile(k_gain, 16), 0.0)
    bound = 1.02 * LOG2E * ATT_HEAD_DIM ** 0.5 * jnp.max(jnp.abs(q_gain)) * jnp.max(jnp.abs(k_gain))
    qconst = jnp.asarray(qc[1])
    rest = bound.astype(F32)
    bound_lane0 = ATT_HEAD_DIM + 4 * N_SLOPE_PIECES
    for i in range(N_SLOPE_PIECES):
        piece = rest.astype(BF16).astype(F32)
        rest = rest - piece
        qconst = jnp.where(lane == bound_lane0 + i, -piece, qconst)
        kc[1, lane == bound_lane0 + i] = 1.0
    qtab = jnp.asarray(qc).at[0].set(qg).at[1].set(qconst)
    ktab = jnp.asarray(kc).at[0].set(kg)
    return qtab, ktab, bound <= SCORE_BOUND_LIMIT


def _expand_heads(w):
    d = w.shape[0]
    w = w.reshape(d, 8, ATT_HEAD_DIM)
    return jnp.concatenate([w, jnp.zeros_like(w)], axis=-1).reshape(d, 1024)


def _attn_finish(sc_ref, subg_ref, o_ref, l0, l1, acc_ref):
    lam = sc_ref[ATT_HEADS]
    o = acc_ref[0] / l0 - lam * (acc_ref[1] / l1)
    ms = jnp.mean(o * o, axis=0, keepdims=True)
    o = o * lax.rsqrt(ms + EPS) * subg_ref[...]
    o_ref[...] = o.T.astype(BF16)


def _attn_fast_kernel(sc_ref, q_ref, k_ref, vt_ref, absd_ref, subg_ref, o_ref, qv_ref, l_ref, acc_ref,
                      *, tq, tk, nk, window):
    h = pl.program_id(1)
    qi = pl.program_id(2)
    st = pl.program_id(3)
    after = qi + st < nk
    tile_dist = jnp.where(after, st, nk - st)
    reach = jnp.int32(window[ATT_HEADS - 1])
    for hh in range(ATT_HEADS - 2, -1, -1):
        reach = jnp.where(h == hh, jnp.int32(window[hh]), reach)

    @pl.when(st == 0)
    def _():
        lane = lax.broadcasted_iota(jnp.int32, (tq, LANES), 1)
        alibi = (lane >= ATT_HEAD_DIM) & (lane < ATT_HEAD_DIM + 4 * N_SLOPE_PIECES)
        for c in range(2):
            qc = q_ref[:, LANES * c:LANES * (c + 1)]
            qv_ref[c] = qc
            qv_ref[2 + c] = jnp.where(alibi, -qc, qc)
            qv_ref[4 + c] = jnp.where(alibi, jnp.zeros_like(qc), qc)
        l_ref[...] = jnp.zeros(l_ref.shape, F32)
        acc_ref[...] = jnp.zeros(acc_ref.shape, F32)

    def step(diag):
        vt = vt_ref[...]
        for c in range(2):
            kc = k_ref[:, LANES * c:LANES * (c + 1)]
            if diag:
                qsel = qv_ref[4 + c]
            else:
                qsel = qv_ref[jnp.where(after, 0, 2) + c]
            s = lax.dot_general(kc, qsel, _NT, preferred_element_type=F32)
            if diag:
                s = s - sc_ref[h] * absd_ref[...]
            p = jnp.exp2(s)
            l_ref[c] += jnp.sum(p.reshape(tk // 8, 8, tq), axis=0)
            acc_ref[c] += jnp.dot(vt, p.astype(BF16), preferred_element_type=F32)

    @pl.when(st == 0)
    def _():
        step(True)

    @pl.when((st != 0) & (tile_dist <= reach))
    def _():
        step(False)

    @pl.when(st == nk - 1)
    def _():
        _attn_finish(sc_ref, subg_ref, o_ref, jnp.sum(l_ref[0], axis=0, keepdims=True),
                     jnp.sum(l_ref[1], axis=0, keepdims=True), acc_ref)


def _attn_safe_kernel(sc_ref, q_ref, k_ref, vt_ref, subg_ref, o_ref, qm_ref, m_ref, l_ref, acc_ref, *, tq, tk, nk):
    h = pl.program_id(1)
    qi = pl.program_id(2)
    ki = pl.program_id(3)

    @pl.when(ki == 0)
    def _():
        lane = lax.broadcasted_iota(jnp.int32, (tq, LANES), 1)
        for c in range(2):
            qc = q_ref[:, LANES * c:LANES * (c + 1)]
            qm_ref[c] = jnp.where(lane < ATT_HEAD_DIM, qc, jnp.zeros_like(qc))
        m_ref[...] = jnp.full(m_ref.shape, NEG_BIG, F32)
        l_ref[...] = jnp.zeros(l_ref.shape, F32)
        acc_ref[...] = jnp.zeros(acc_ref.shape, F32)

    kpos = ki * tk + lax.broadcasted_iota(jnp.int32, (tk, tq), 0)
    qpos = qi * tq + lax.broadcasted_iota(jnp.int32, (tk, tq), 1)
    bias = sc_ref[h] * jnp.abs(kpos - qpos).astype(F32)
    vt = vt_ref[...]
    for c in range(2):
        kc = k_ref[:, LANES * c:LANES * (c + 1)]
        s = lax.dot_general(kc, qm_ref[c], _NT, preferred_element_type=F32) - bias
        m_prev = m_ref[c]
        m_new = jnp.maximum(m_prev, jnp.max(s, axis=0, keepdims=True))
        alpha = jnp.exp2(m_prev - m_new)
        p = jnp.exp2(s - m_new)
        l_ref[c] = alpha * l_ref[c] + jnp.sum(p, axis=0, keepdims=True)
        acc_ref[c] = alpha * acc_ref[c] + jnp.dot(vt, p.astype(BF16), preferred_element_type=F32)
        m_ref[c] = m_new

    @pl.when(ki == nk - 1)
    def _():
        _attn_finish(sc_ref, subg_ref, o_ref, l_ref[0], l_ref[1], acc_ref)


def _attention(q, k, vt, scal, subg, fast_ok, b, t, tq, tk):
    nq, nk = t // tq, t // tk
    blk = 2 * LANES
    common = dict(
        grid=(b, ATT_HEADS, nq, nk),
        out_specs=pl.BlockSpec((None, tq, LANES), lambda bi, h, qi, ki: (bi, qi, h)),
        out_shape=jax.ShapeDtypeStruct((b, t, ATT_WIDTH), BF16),
        compiler_params=_params(("parallel", "parallel", "parallel", "arbitrary")),
    )
    qkv_specs = [
        pl.BlockSpec((None, tq, blk), lambda bi, h, qi, ki: (bi, qi, h)),
        pl.BlockSpec((None, tk, blk), lambda bi, h, qi, ki: (bi, ki, h)),
        pl.BlockSpec((LANES, tk), lambda bi, h, qi, ki: (h, bi * nk + ki)),
    ]
    smem = pl.BlockSpec(memory_space=pltpu.SMEM)
    idx = np.arange(tk)[:, None] - np.arange(tq)[None, :]
    absd = jnp.asarray(np.abs(idx), dtype=F32)
    slopes2 = np.exp2(-8.0 * (np.arange(ATT_HEADS) + 1.0) / ATT_HEADS) * LOG2E
    window = tuple(int((EXP2_ZERO_ARG / s - 1.0) // tk) + 1 for s in slopes2)
    def key_tile(h, qi, st):
        reach = jnp.int32(window[ATT_HEADS - 1])
        for hh in range(ATT_HEADS - 2, -1, -1):
            reach = jnp.where(h == hh, jnp.int32(window[hh]), reach)
        held = jnp.where((st > reach) & (st < nk - reach), jnp.minimum(reach, nk - 1), st)
        return (qi + held) % nk

    rot_specs = [
        pl.BlockSpec((None, tq, blk), lambda bi, h, qi, st: (bi, qi, h)),
        pl.BlockSpec((None, tk, blk), lambda bi, h, qi, st: (bi, key_tile(h, qi, st), h)),
        pl.BlockSpec((LANES, tk), lambda bi, h, qi, st: (h, bi * nk + key_tile(h, qi, st))),
    ]

    def fast(q, k, vt, scal, subg):
        return pl.pallas_call(
            functools.partial(_attn_fast_kernel, tq=tq, tk=tk, nk=nk, window=window),
            in_specs=[smem] + rot_specs + [_const_spec((tk, tq)), _const_spec((ATT_V_DIM, 1))],
            scratch_shapes=[
                pltpu.VMEM((6, tq, LANES), BF16),
                pltpu.VMEM((2, 8, tq), F32),
                pltpu.VMEM((2, ATT_V_DIM, tq), F32),
            ],
            name="diff_attn",
            **common,
        )(scal, q, k, vt, absd, subg)

    def safe(q, k, vt, scal, subg):
        return pl.pallas_call(
            functools.partial(_attn_safe_kernel, tq=tq, tk=tk, nk=nk),
            in_specs=[smem] + qkv_specs + [_const_spec((ATT_V_DIM, 1))],
            scratch_shapes=[
                pltpu.VMEM((2, tq, LANES), BF16),
                pltpu.VMEM((2, 1, tq), F32),
                pltpu.VMEM((2, 1, tq), F32),
                pltpu.VMEM((2, ATT_V_DIM, tq), F32),
            ],
            name="diff_attn_safe",
            **common,
        )(scal, q, k, vt, subg)

    return lax.cond(fast_ok, fast, safe, q, k, vt, scal, subg)


N_LEVELS = 6


def _hgrn_constants(tt):
    c = HG_CHUNK
    t = np.arange(tt)
    same_chunk = (t[:, None] // c) == (t[None, :] // c)
    mats, masks = [], []
    for j in range(N_LEVELS):
        m = 1 << j
        blk = t // m
        same_blk = blk[:, None] == blk[None, :]
        odd = (blk % 2) == 1
        incl = same_blk & (t[None, :] <= t[:, None])
        excl_rev = same_blk & (t[None, :] > t[:, None])
        mats.append(np.where(odd[:, None], incl, excl_rev))
        same_2m = (t[:, None] // (2 * m)) == (t[None, :] // (2 * m))
        masks.append((same_2m & odd[:, None] & (~odd)[None, :]).astype(np.float32))
    mats.append(same_chunk & (t[None, :] <= t[:, None]))
    mats.append(same_chunk & (t[None, :] > t[:, None]))
    mats = np.stack(mats).astype(np.float32)
    masks = np.stack(masks)
    mats_b = mats[:, ::-1, ::-1]
    masks_b = masks[:, ::-1, ::-1]
    mst = np.concatenate([mats.reshape(-1, tt), mats_b.reshape(-1, tt)], axis=0)
    amask = np.concatenate([masks, masks_b], axis=0)
    return jnp.asarray(mst, dtype=BF16), jnp.asarray(amask, dtype=F32)


def _hgrn_kernel(hpf_ref, hpb_ref, lgf_ref, lgb_ref, mst_ref, amask_ref, g256_ref, of_ref, ob_ref, s_ref, *, tt):
    i = pl.program_id(1)
    nlev = N_LEVELS
    nmat = N_LEVELS + 2
    nchunk = tt // HG_CHUNK

    @pl.when(i == 0)
    def _():
        s_ref[...] = jnp.zeros(s_ref.shape, F32)

    lane256 = lax.broadcasted_iota(jnp.int32, (1, 256), 1)
    lane128 = lax.broadcasted_iota(jnp.int32, (1, LANES), 1)
    r128 = lax.broadcasted_iota(jnp.int32, (LANES, LANES), 0)
    c128 = lax.broadcasted_iota(jnp.int32, (LANES, LANES), 1)
    bdmask = ((r128 // HG_HEAD_DIM) == (c128 // HG_HEAD_DIM)).astype(F32)

    def direction(d, hp_ref, lg_ref, o_ref):
        qs = hp_ref[:, 0:256].astype(F32)
        v = hp_ref[:, 256:512]
        kk = hp_ref[:, 512 + 256 * d:768 + 256 * d].astype(F32)
        eall = jnp.dot(mst_ref[d * nmat * tt:(d + 1) * nmat * tt, :], lg_ref[...].astype(BF16),
                       preferred_element_type=F32)

        def expo(j):
            return jnp.exp(eall[j * tt:(j + 1) * tt])

        a = [None] * 4
        for lev in range(nlev):
            xx = expo(lev)
            qt, kt = (qs * xx).astype(BF16), (kk * xx).astype(BF16)
            am = amask_ref[d * nlev + lev]
            for pr in range(2):
                qp = qt[:, LANES * pr:LANES * (pr + 1)]
                kp = kt[:, LANES * pr:LANES * (pr + 1)]
                for hh in range(2):
                    sel = (lane128 // HG_HEAD_DIM) == hh
                    qm = jnp.where(sel, qp, jnp.zeros_like(qp))
                    p = lax.dot_general(qm, kp, _NT, preferred_element_type=F32) * am
                    idx = 2 * pr + hh
                    a[idx] = p if a[idx] is None else a[idx] + p

        o = jnp.dot((qs * kk).astype(BF16), g256_ref[...], preferred_element_type=F32) * v.astype(F32)
        for idx in range(4):
            vm = jnp.where((lane256 // HG_HEAD_DIM) == idx, v, jnp.zeros_like(v))
            contrib = jnp.dot(a[idx].astype(BF16), vm, preferred_element_type=F32)
            o = o + contrib
        o_ref[...] = o

        xq = expo(N_LEVELS)
        xk = expo(N_LEVELS + 1)
        qc = (qs * xq).astype(BF16)
        kc = (kk * xk).astype(BF16)
        order = range(nchunk) if d == 0 else range(nchunk - 1, -1, -1)
        for c in order:
            r0 = c * HG_CHUNK
            rows = slice(r0, r0 + HG_CHUNK)
            drow = r0 + HG_CHUNK - 1 if d == 0 else r0
            for pr in range(2):
                cols = slice(LANES * pr, LANES * (pr + 1))
                st = s_ref[d, pr]
                inter = lax.dot_general(qc[rows, cols], st.astype(BF16), _NT, preferred_element_type=F32)
                o_ref[rows, cols] += inter
                ut = lax.dot_general(v[rows, cols], kc[rows, cols], _TN, preferred_element_type=F32)
                dec = xq[drow:drow + 1, cols]
                s_ref[d, pr] = st * dec + ut * bdmask

    direction(0, hpf_ref, lgf_ref, of_ref)
    direction(1, hpb_ref, lgb_ref, ob_ref)


def _hgrn(hp, lg, b, t, tt):
    nt = t // tt
    mst, amask = _hgrn_constants(tt)
    kern = functools.partial(_hgrn_kernel, tt=tt)
    fwd = lambda w, cb: pl.BlockSpec((None, tt, w), lambda bi, i: (bi, i, cb))
    bwd = lambda w, cb: pl.BlockSpec((None, tt, w), lambda bi, i: (bi, nt - 1 - i, cb))
    return pl.pallas_call(
        kern,
        grid=(b, nt),
        in_specs=[fwd(1024, 0), bwd(1024, 0), fwd(256, 0), bwd(256, 1),
                  _const_spec(mst.shape), _const_spec(amask.shape), _const_spec((256, 256))],
        out_specs=(fwd(256, 0), bwd(256, 0)),
        out_shape=(jax.ShapeDtypeStruct((b, t, HG_WIDTH), F32), jax.ShapeDtypeStruct((b, t, HG_WIDTH), F32)),
        scratch_shapes=[pltpu.VMEM((2, 2, LANES, LANES), F32)],
        compiler_params=_params(("parallel", "arbitrary")),
        name="hgrn2",
    )(hp, hp, lg, lg, mst, amask, _block_ones(256, HG_HEAD_DIM))


def _merge_kernel(*refs, tm, seq, with_router):
    (x_ref, oa_ref, cv_ref, cvp_ref, cvn_ref, of_ref, ob_ref, sg_ref, gate_ref, cw_ref, hgn_ref, g256_ref,
     wa_ref, wb_ref, wc_ref, wo_ref, g2_ref) = refs[:17]
    if with_router:
        wr_ref, xo_ref, h2_ref, lo_ref = refs[17:]
    else:
        xo_ref, h2_ref = refs[17:]
    i = pl.program_id(0)
    tile_start = (i * tm) % seq

    u = cv_ref[:, 256:512].astype(F32)
    row = lax.broadcasted_iota(jnp.int32, (tm, 1), 0)
    prev_row = jnp.where(tile_start == 0, 0.0, cvp_ref[7:8, 256:512].astype(F32))
    next_row = jnp.where(tile_start + tm == seq, 0.0, cvn_ref[0:1, 256:512].astype(F32))
    u_m1 = jnp.where(row == 0, prev_row, pltpu.roll(u, 1, axis=0))
    u_p1 = jnp.where(row == tm - 1, next_row, pltpu.roll(u, tm - 1, axis=0))
    conv = cw_ref[0:1, :] * u_m1 + cw_ref[1:2, :] * u + cw_ref[2:3, :] * u_p1
    yb_in = (cv_ref[:, 0:256].astype(F32) * conv).astype(BF16)

    o = of_ref[...] + ob_ref[...]
    ss = _group_sum(o * o, g256_ref[...])
    yc_in = (o * lax.rsqrt(ss * (1.0 / HG_HEAD_DIM) + EPS) * hgn_ref[...] * sg_ref[...].astype(F32)).astype(BF16)

    ya = jnp.dot(oa_ref[...], wa_ref[...], preferred_element_type=F32)
    yb = jnp.dot(yb_in, wb_ref[...], preferred_element_type=F32)
    yc = jnp.dot(yc_in, wc_ref[...], preferred_element_type=F32)
    merged = (gate_ref[:, 0:1024].astype(F32) * ya + gate_ref[:, 1024:2048].astype(F32) * yb
              + gate_ref[:, 2048:3072].astype(F32) * yc)
    xn = x_ref[...] + jnp.dot(merged.astype(BF16), wo_ref[...], preferred_element_type=F32)
    xo_ref[...] = xn

    ms = jnp.mean(xn * xn, axis=-1, keepdims=True)
    h2 = xn * lax.rsqrt(ms + EPS) * g2_ref[...]
    h2_ref[...] = h2.astype(BF16)
    if with_router:
        lo_ref[...] = lax.dot_general(wr_ref[...], h2, _NT, preferred_element_type=F32,
                                      precision=lax.Precision.HIGHEST)


def _merge(x, oa, cv, of, ob, sg, gates, conv_w, hg_norm, wa, wb, wc, wo, g2, w_router, seq, tm):
    n = x.shape[0]
    nb8 = n // 8
    step8 = tm // 8
    row = lambda w: pl.BlockSpec((tm, w), lambda i: (i, 0))
    in_specs = [
        row(D_MODEL), row(512), row(512),
        pl.BlockSpec((8, 512), lambda i: (jnp.maximum(i * step8 - 1, 0), 0)),
        pl.BlockSpec((8, 512), lambda i: (jnp.minimum((i + 1) * step8, nb8 - 1), 0)),
        row(256), row(256), row(256), row(3072),
        _const_spec((8, 256)), _const_spec((1, 256)), _const_spec((256, 256)),
        _const_spec(wa.shape), _const_spec(wb.shape), _const_spec(wc.shape), _const_spec(wo.shape),
        _const_spec((1, D_MODEL)),
    ]
    args = [x, oa, cv, cv, cv, of, ob, sg, gates, conv_w, hg_norm, _block_ones(256, HG_HEAD_DIM),
            wa, wb, wc, wo, g2]
    out_specs = [row(D_MODEL), row(D_MODEL)]
    out_shape = [jax.ShapeDtypeStruct((n, D_MODEL), F32), jax.ShapeDtypeStruct((n, D_MODEL), BF16)]
    with_router = w_router is not None
    if with_router:
        in_specs.append(_const_spec(w_router.shape))
        args.append(w_router)
        out_specs.append(pl.BlockSpec((N_EXPERTS, tm), lambda i: (0, i)))
        out_shape.append(jax.ShapeDtypeStruct((N_EXPERTS, n), F32))
    kern = functools.partial(_merge_kernel, tm=tm, seq=seq, with_router=with_router)
    return pl.pallas_call(
        kern,
        grid=(n // tm,),
        in_specs=in_specs,
        out_specs=tuple(out_specs),
        out_shape=tuple(out_shape),
        compiler_params=_params(("parallel",)),
        name="merge_router" if with_router else "merge",
    )(*args)


def _ffn_kernel(x_ref, h_ref, wg_ref, wu_ref, wd_ref, o_ref):
    h = h_ref[...]
    g = jnp.dot(h, wg_ref[...], preferred_element_type=F32)
    u = jnp.dot(h, wu_ref[...], preferred_element_type=F32)
    act = (jax.nn.silu(g) * u).astype(BF16)
    o_ref[...] = x_ref[...] + jnp.dot(act, wd_ref[...], preferred_element_type=F32)


def _ffn(x, h2, wg, wu, wd, tm):
    n = x.shape[0]
    row = lambda: pl.BlockSpec((tm, D_MODEL), lambda i: (i, 0))
    return pl.pallas_call(
        _ffn_kernel,
        grid=(n // tm,),
        in_specs=[row(), row(), _const_spec(wg.shape), _const_spec(wu.shape), _const_spec(wd.shape)],
        out_specs=row(),
        out_shape=jax.ShapeDtypeStruct((n, D_MODEL), F32),
        compiler_params=_params(("parallel",)),
        name="ffn_dense",
    )(x, h2, wg, wu, wd)


MOE_ROWS = 144


def _moe_expert_kernel(lo_ref, h_ref, y_ref, tri_ref, wg_ref, wu_ref, wd_ref, o_ref, *, expert, tm):
    lg = lo_ref[...]
    sub = lax.broadcasted_iota(jnp.int32, lg.shape, 0)
    m1 = jnp.max(lg, axis=0, keepdims=True)
    i1 = jnp.min(jnp.where(lg == m1, sub, N_EXPERTS), axis=0, keepdims=True)
    lg2 = jnp.where(sub == i1, NEG_BIG, lg)
    m2 = jnp.max(lg2, axis=0, keepdims=True)
    i2 = jnp.min(jnp.where(lg2 == m2, sub, N_EXPERTS), axis=0, keepdims=True)
    e2 = jnp.exp(m2 - m1)
    w_row = jnp.where(i1 == expert, 1.0 / (1.0 + e2), 0.0) + jnp.where(i2 == expert, e2 / (1.0 + e2), 0.0)
    routed = (i1 == expert) | (i2 == expert)
    routed_f = routed.astype(F32)
    csum = jnp.dot(jnp.broadcast_to(routed_f, (8, tm)).astype(BF16), tri_ref[...], preferred_element_type=F32)
    rank = csum[0:1, :].astype(jnp.int32) - 1
    count = jnp.sum(routed_f).astype(jnp.int32)

    o_ref[...] = y_ref[...]
    h = h_ref[...]

    def body(blk, carry):
        rows = blk * MOE_ROWS + lax.broadcasted_iota(jnp.int32, (MOE_ROWS, tm), 0)
        sel = (rank == rows) & routed
        sel_b = sel.astype(F32).astype(BF16)
        xe = jnp.dot(sel_b, h, preferred_element_type=F32).astype(BF16)
        g = jnp.dot(xe, wg_ref[...], preferred_element_type=F32)
        u = jnp.dot(xe, wu_ref[...], preferred_element_type=F32)
        act = (jax.nn.silu(g) * u).astype(BF16)
        ye = jnp.dot(act, wd_ref[...], preferred_element_type=F32)
        w_sel = jnp.sum(jnp.where(sel, w_row, 0.0), axis=1, keepdims=True)
        ye = (ye * w_sel).astype(BF16)
        o_ref[...] += lax.dot_general(sel_b, ye, _TN, preferred_element_type=F32)
        return carry

    lax.fori_loop(0, (count + MOE_ROWS - 1) // MOE_ROWS, body, 0)


def _moe(x, h2, logits_t, wg, wu, wd, tm):
    n = x.shape[0]
    row = lambda: pl.BlockSpec((tm, D_MODEL), lambda i: (i, 0))
    t = np.arange(tm)
    tri = jnp.asarray((t[:, None] <= t[None, :]).astype(np.float32), dtype=BF16)
    expert_w = lambda w, e: pl.BlockSpec((None,) + w.shape[1:], lambda i: (e, 0, 0), pipeline_mode=pl.Buffered(1))
    y = x
    for e in range(N_EXPERTS):
        y = pl.pallas_call(
            functools.partial(_moe_expert_kernel, expert=e, tm=tm),
            grid=(n // tm,),
            in_specs=[pl.BlockSpec((N_EXPERTS, tm), lambda i: (0, i)), row(), row(), _const_spec((tm, tm)),
                      expert_w(wg, e), expert_w(wu, e), expert_w(wd, e)],
            out_specs=row(),
            out_shape=jax.ShapeDtypeStruct((n, D_MODEL), F32),
            input_output_aliases={2: 0},
            compiler_params=_params(("parallel",)),
            name="moe_expert",
        )(logits_t, h2, y, tri, wg, wu, wd)
    return y


def _pick(limit, total):
    return min(limit, total)


def kernel(x_prompt, x_sample, norm_mix, w_in, q_norm, k_norm, lambda_q1, lambda_k1, lambda_q2, lambda_k2, sub_norm, conv_w, hg_lower, hg_norm, w_up_a, w_up_b, w_up_c, w_out, norm_ffn, w_gate_dense, w_up_dense, w_down_dense, w_router, w_gate_moe, w_up_moe, w_down_moe):
    depth = norm_mix.shape[0]
    lb_all = jnp.cumsum(jax.nn.softmax(hg_lower.astype(F32), axis=0), axis=0)
    lb_all = lb_all - lb_all[0]
    slopes = jnp.exp2(-8.0 * (jnp.arange(ATT_HEADS, dtype=F32) + 1.0) / ATT_HEADS)

    layers = []
    for l in range(depth):
        lam_init = 0.8 - 0.6 * math.exp(-0.3 * l)
        lam = (jnp.exp(jnp.sum(lambda_q1[l].astype(F32) * lambda_k1[l].astype(F32)))
               - jnp.exp(jnp.sum(lambda_q2[l].astype(F32) * lambda_k2[l].astype(F32))) + lam_init)
        wl = w_in[l]
        qtab, ktab, fast_ok = _alibi_tables(q_norm[l].astype(F32), k_norm[l].astype(F32))
        p = dict(
            gain=norm_mix[l].reshape(1, D_MODEL),
            w_main=wl[:, 1536:].astype(BF16),
            w_vt=wl[:, 1024:1536].T.astype(BF16),
            w_qx=_expand_heads(wl[:, 0:512]).astype(BF16),
            w_kx=_expand_heads(wl[:, 512:1024]).astype(BF16),
            qtab=qtab, ktab=ktab, fast_ok=fast_ok,
            lb=lb_all[l],
            scal=jnp.concatenate([slopes * LOG2E, lam.reshape(1), jnp.zeros((3,), F32)]),
            subg=(sub_norm[l] * (1.0 - lam_init)).reshape(ATT_V_DIM, 1),
            conv_w=jnp.concatenate([conv_w[l], jnp.zeros((5, SC_WIDTH), F32)], axis=0),
            hg_norm=hg_norm[l].reshape(1, HG_WIDTH),
            wa=w_up_a[l].astype(BF16), wb=w_up_b[l].astype(BF16), wc=w_up_c[l].astype(BF16),
            wo=w_out[l].astype(BF16),
            g2=norm_ffn[l].reshape(1, D_MODEL),
        )
        j = l // 2
        if l % 2 == 0:
            p.update(wg=w_gate_dense[j].astype(BF16), wu=w_up_dense[j].astype(BF16),
                     wd=w_down_dense[j].astype(BF16))
        else:
            p.update(w_router=w_router[j].T.astype(F32),
                     wg=w_gate_moe[j].astype(BF16), wu=w_up_moe[j].astype(BF16), wd=w_down_moe[j].astype(BF16))
        layers.append(p)

    def trunk(x3):
        b, t, _ = x3.shape
        n = b * t
        tm = _pick(512, t)
        tq = _pick(1024, t)
        tt = _pick(256, t)
        x = x3.reshape(n, D_MODEL)
        for l, p in enumerate(layers):
            q, k, vt, cv, hp, lg, sg, gates = _inproj(x, p["gain"], p["w_main"], p["w_vt"], p["w_qx"], p["w_kx"],
                                                      p["qtab"], p["ktab"], p["lb"], t, tm)
            oa = _attention(q.reshape(b, t, 1024), k.reshape(b, t, 1024), vt, p["scal"], p["subg"], p["fast_ok"],
                            b, t, tq, tq)
            of, ob = _hgrn(hp.reshape(b, t, 1024), lg.reshape(b, t, 512), b, t, tt)
            res = _merge(x, oa.reshape(n, 512), cv, of.reshape(n, 256), ob.reshape(n, 256), sg, gates,
                         p["conv_w"], p["hg_norm"], p["wa"], p["wb"], p["wc"], p["wo"], p["g2"],
                         p.get("w_router"), t, tm)
            if l % 2 == 0:
                xn, h2 = res
                x = _ffn(xn, h2, p["wg"], p["wu"], p["wd"], tm)
            else:
                xn, h2, logits = res
                x = _moe(xn, h2, logits, p["wg"], p["wu"], p["wd"], _pick(1024, t))
        return x.reshape(b, t, D_MODEL)

    return (trunk(x_prompt), trunk(x_sample))
```

```python
import functools
import math

import numpy as np
import jax
import jax.numpy as jnp
from jax import lax
from jax.experimental import pallas as pl
from jax.experimental.pallas import tpu as pltpu

F32 = jnp.float32
BF16 = jnp.bfloat16

D_MODEL = 1024
ATT_HEADS = 4
ATT_HEAD_DIM = 64
ATT_V_DIM = 128
ATT_QK_WIDTH = 512
ATT_WIDTH = 512
SC_WIDTH = 256
HG_WIDTH = 256
HG_HEAD_DIM = 64
HG_CHUNK = 64
N_BRANCH = 3
N_EXPERTS = 8
EPS = 1e-6

LOG2E = math.log2(math.e)
N_SLOPE_PIECES = 3
SCORE_BOUND_LIMIT = 48.0
EXP2_ZERO_ARG = 151.0

LANES = 128
POS_SHIFT = 7
VMEM_LIMIT = 56 * 1024 * 1024
NEG_BIG = -1e30

_NT = (((1,), (1,)), ((), ()))
_TN = (((0,), (0,)), ((), ()))


def _const_spec(shape):
    nd = len(shape)
    return pl.BlockSpec(shape, lambda *_: (0,) * nd, pipeline_mode=pl.Buffered(1))


def _params(sem):
    return pltpu.CompilerParams(dimension_semantics=sem, vmem_limit_bytes=VMEM_LIMIT)


def _split_bf16(x):
    hi = x.astype(BF16)
    lo = (x - hi.astype(F32)).astype(BF16)
    return hi, lo


def _group_sum(x, gmat):
    hi, lo = _split_bf16(x)
    return (jnp.dot(hi, gmat, preferred_element_type=F32) + jnp.dot(lo, gmat, preferred_element_type=F32))


def _block_ones(width, group):
    idx = np.arange(width) // group
    return jnp.asarray((idx[:, None] == idx[None, :]).astype(np.float32), dtype=BF16)


def _inproj_kernel(x_ref, g_ref, wm_ref, wvt_ref, wq_ref, wk_ref, qtab_ref, ktab_ref, lb_ref,
                   q_ref, k_ref, vt_ref, cv_ref, hp_ref, lg_ref, sg_ref, gate_ref, *, tm, seq):
    x = x_ref[...]
    ms = jnp.mean(x * x, axis=-1, keepdims=True)
    h = (x * lax.rsqrt(ms + EPS) * g_ref[...]).astype(BF16)

    vt_ref[...] = lax.dot_general(wvt_ref[...], h, _NT, preferred_element_type=F32).astype(BF16)

    pos = (pl.program_id(0) * tm) % seq + lax.broadcasted_iota(jnp.int32, (tm, 1), 0)
    pos_lo = (pos & (LANES - 1)).astype(F32)
    pos_hi = (pos >> POS_SHIFT).astype(F32)

    def head_norm(raw, tab_ref, out_ref):
        for j in range(2 * ATT_HEADS):
            cols = slice(LANES * j, LANES * (j + 1))
            blk = raw[:, cols]
            ss = jnp.sum(blk * blk, axis=-1, keepdims=True)
            aux = tab_ref[1:2, cols] + tab_ref[2:3, cols] * pos_lo + tab_ref[3:4, cols] * pos_hi
            out_ref[:, cols] = (blk * lax.rsqrt(ss * (1.0 / ATT_HEAD_DIM) + EPS) * tab_ref[0:1, cols]
                                + aux).astype(BF16)

    head_norm(jnp.dot(h, wq_ref[...], preferred_element_type=F32), qtab_ref, q_ref)
    head_norm(jnp.dot(h, wk_ref[...], preferred_element_type=F32), ktab_ref, k_ref)

    sc = jnp.dot(h, wm_ref[:, 0:768], preferred_element_type=F32)
    cv_ref[:, 0:256] = sc[:, 0:256].astype(BF16)
    cv_ref[:, 256:512] = (sc[:, 256:512] * sc[:, 512:768]).astype(BF16)

    hg = jnp.dot(h, wm_ref[:, 768:2048], preferred_element_type=F32)
    hp_ref[:, 0:256] = jax.nn.silu(hg[:, 0:256]).astype(BF16)
    hp_ref[:, 256:512] = hg[:, 256:512].astype(BF16)
    for d in range(2):
        z = hg[:, 512 + 256 * d:768 + 256 * d]
        lb = lb_ref[d:d + 1, :]
        f = lb + (1.0 - lb) * jax.nn.sigmoid(z)
        hp_ref[:, 512 + 256 * d:768 + 256 * d] = ((1.0 - lb) * jax.nn.sigmoid(-z)).astype(BF16)
        lg_ref[:, 256 * d:256 * d + 256] = jnp.log(f)
    sg_ref[...] = jax.nn.silu(hg[:, 1024:1280]).astype(BF16)

    gate_ref[...] = jax.nn.sigmoid(
        jnp.dot(h, wm_ref[:, 2048:5120], preferred_element_type=F32)).astype(BF16)


def _inproj(x, gain, w_main, w_vt, w_qx, w_kx, qtab, ktab, lb, seq, tm):
    n = x.shape[0]
    row = lambda w: pl.BlockSpec((tm, w), lambda i: (i, 0))
    out_shape = (
        jax.ShapeDtypeStruct((n, 1024), BF16),
        jax.ShapeDtypeStruct((n, 1024), BF16),
        jax.ShapeDtypeStruct((512, n), BF16),
        jax.ShapeDtypeStruct((n, 512), BF16),
        jax.ShapeDtypeStruct((n, 1024), BF16),
        jax.ShapeDtypeStruct((n, 512), F32),
        jax.ShapeDtypeStruct((n, 256), BF16),
        jax.ShapeDtypeStruct((n, 3072), BF16),
    )
    kern = functools.partial(_inproj_kernel, tm=tm, seq=seq)
    return pl.pallas_call(
        kern,
        grid=(n // tm,),
        in_specs=[row(D_MODEL), _const_spec((1, D_MODEL)), _const_spec(w_main.shape), _const_spec(w_vt.shape),
                  _const_spec(w_qx.shape), _const_spec(w_kx.shape), _const_spec((8, 1024)), _const_spec((8, 1024)),
                  _const_spec((2, 256))],
        out_specs=(row(1024), row(1024), pl.BlockSpec((512, tm), lambda i: (0, i)), row(512), row(1024),
                   row(512), row(256), row(3072)),
        out_shape=out_shape,
        compiler_params=_params(("parallel",)),
        name="inproj",
    )(x, gain, w_main, w_vt, w_qx, w_kx, qtab, ktab, lb)


def _alibi_tables(q_gain, k_gain):
    slopes = np.exp2(-8.0 * (np.arange(ATT_HEADS, dtype=np.float64) + 1.0) / ATT_HEADS) * LOG2E
    qc = np.zeros((8, 1024), np.float32)
    kc = np.zeros((8, 1024), np.float32)
    for h in range(ATT_HEADS):
        rest = np.float32(slopes[h])
        pieces = []
        for _ in range(N_SLOPE_PIECES):
            piece = np.asarray(rest, dtype=BF16).astype(np.float32)
            pieces.append(float(piece))
            rest = np.float32(rest - piece)
        for c in range(2):
            base = LANES * (2 * h + c) + ATT_HEAD_DIM
            for i, a in enumerate(pieces):
                qc[1, base + 2 * i] = -a
                qc[1, base + 2 * i + 1] = -a * LANES
                kc[2, base + 2 * i] = 1.0
                kc[3, base + 2 * i + 1] = 1.0
                off = base + 2 * N_SLOPE_PIECES
                qc[2, off + 2 * i] = 1.0
                qc[3, off + 2 * i + 1] = 1.0
                kc[1, off + 2 * i] = a
                kc[1, off + 2 * i + 1] = a * LANES
    lane = np.arange(1024) % LANES
    lane_is_data = lane < ATT_HEAD_DIM
    qg = jnp.where(lane_is_data, jnp.tile(q_gain, 16) * (LOG2E * ATT_HEAD_DIM ** -0.5), 0.0)
    kg = jnp.where(lane_is_data, jnp.tile(k_gain, 16), 0.0)
    bound = 1.02 * LOG2E * ATT_HEAD_DIM ** 0.5 * jnp.max(jnp.abs(q_gain)) * jnp.max(jnp.abs(k_gain))
    qconst = jnp.asarray(qc[1])
    rest = bound.astype(F32)
    bound_lane0 = ATT_HEAD_DIM + 4 * N_SLOPE_PIECES
    for i in range(N_SLOPE_PIECES):
        piece = rest.astype(BF16).astype(F32)
        rest = rest - piece
        qconst = jnp.where(lane == bound_lane0 + i, -piece, qconst)
        kc[1, lane == bound_lane0 + i] = 1.0
    qtab = jnp.asarray(qc).at[0].set(qg).at[1].set(qconst)
    ktab = jnp.asarray(kc).at[0].set(kg)
    return qtab, ktab, bound <= SCORE_BOUND_LIMIT


def _expand_heads(w):
    d = w.shape[0]
    w = w.reshape(d, 8, ATT_HEAD_DIM)
    return jnp.concatenate([w, jnp.zeros_like(w)], axis=-1).reshape(d, 1024)


def _attn_finish(sc_ref, subg_ref, o_ref, l0, l1, acc_ref):
    lam = sc_ref[ATT_HEADS]
    o = acc_ref[0] / l0 - lam * (acc_ref[1] / l1)
    ms = jnp.mean(o * o, axis=0, keepdims=True)
    o = o * lax.rsqrt(ms + EPS) * subg_ref[...]
    o_ref[...] = o.T.astype(BF16)


def _attn_key_tile(qi, st, nk, reach, band):
    if band:
        off = jnp.where(st <= reach, st, reach - st)
        kb = qi + off
        return jnp.clip(kb, 0, nk - 1), off > 0, (kb >= 0) & (kb < nk)
    return (qi + st) % nk, qi + st < nk, st >= 0


def _attn_fast_kernel(sc_ref, q_ref, k_ref, vt_ref, absd_ref, subg_ref, o_ref, qv_ref, l_ref, acc_ref,
                      *, tq, tk, nk, head, reach, band, nsteps):
    h = head
    qi = pl.program_id(1)
    st = pl.program_id(2)
    _, after, exists = _attn_key_tile(qi, st, nk, reach, band)

    @pl.when(st == 0)
    def _():
        lane = lax.broadcasted_iota(jnp.int32, (tq, LANES), 1)
        alibi = (lane >= ATT_HEAD_DIM) & (lane < ATT_HEAD_DIM + 4 * N_SLOPE_PIECES)
        for c in range(2):
            qc = q_ref[:, LANES * c:LANES * (c + 1)]
            qv_ref[c] = qc
            qv_ref[2 + c] = jnp.where(alibi, -qc, qc)
            qv_ref[4 + c] = jnp.where(alibi, jnp.zeros_like(qc), qc)
        l_ref[...] = jnp.zeros(l_ref.shape, F32)
        acc_ref[...] = jnp.zeros(acc_ref.shape, F32)

    def step(diag):
        vt = vt_ref[...]
        for c in range(2):
            kc = k_ref[:, LANES * c:LANES * (c + 1)]
            if diag:
                qsel = qv_ref[4 + c]
            else:
                qsel = qv_ref[jnp.where(after, 0, 2) + c]
            s = lax.dot_general(kc, qsel, _NT, preferred_element_type=F32)
            if diag:
                s = s - sc_ref[h] * absd_ref[...]
            p = jnp.exp2(s)
            l_ref[c] += jnp.sum(p.reshape(tk // 8, 8, tq), axis=0)
            acc_ref[c] += jnp.dot(vt, p.astype(BF16), preferred_element_type=F32)

    @pl.when(st == 0)
    def _():
        step(True)

    @pl.when((st != 0) & exists)
    def _():
        step(False)

    @pl.when(st == nsteps - 1)
    def _():
        _attn_finish(sc_ref, subg_ref, o_ref, jnp.sum(l_ref[0], axis=0, keepdims=True),
                     jnp.sum(l_ref[1], axis=0, keepdims=True), acc_ref)


def _attn_safe_kernel(sc_ref, q_ref, k_ref, vt_ref, subg_ref, o_ref, qm_ref, m_ref, l_ref, acc_ref, *, tq, tk, nk):
    h = pl.program_id(1)
    qi = pl.program_id(2)
    ki = pl.program_id(3)

    @pl.when(ki == 0)
    def _():
        lane = lax.broadcasted_iota(jnp.int32, (tq, LANES), 1)
        for c in range(2):
            qc = q_ref[:, LANES * c:LANES * (c + 1)]
            qm_ref[c] = jnp.where(lane < ATT_HEAD_DIM, qc, jnp.zeros_like(qc))
        m_ref[...] = jnp.full(m_ref.shape, NEG_BIG, F32)
        l_ref[...] = jnp.zeros(l_ref.shape, F32)
        acc_ref[...] = jnp.zeros(acc_ref.shape, F32)

    kpos = ki * tk + lax.broadcasted_iota(jnp.int32, (tk, tq), 0)
    qpos = qi * tq + lax.broadcasted_iota(jnp.int32, (tk, tq), 1)
    bias = sc_ref[h] * jnp.abs(kpos - qpos).astype(F32)
    vt = vt_ref[...]
    for c in range(2):
        kc = k_ref[:, LANES * c:LANES * (c + 1)]
        s = lax.dot_general(kc, qm_ref[c], _NT, preferred_element_type=F32) - bias
        m_prev = m_ref[c]
        m_new = jnp.maximum(m_prev, jnp.max(s, axis=0, keepdims=True))
        alpha = jnp.exp2(m_prev - m_new)
        p = jnp.exp2(s - m_new)
        l_ref[c] = alpha * l_ref[c] + jnp.sum(p, axis=0, keepdims=True)
        acc_ref[c] = alpha * acc_ref[c] + jnp.dot(vt, p.astype(BF16), preferred_element_type=F32)
        m_ref[c] = m_new

    @pl.when(ki == nk - 1)
    def _():
        _attn_finish(sc_ref, subg_ref, o_ref, l_ref[0], l_ref[1], acc_ref)


def _attention(q, k, vt, scal, subg, fast_ok, b, t, tq, tk):
    nq, nk = t // tq, t // tk
    blk = 2 * LANES
    common = dict(
        grid=(b, ATT_HEADS, nq, nk),
        out_specs=pl.BlockSpec((None, tq, LANES), lambda bi, h, qi, ki: (bi, qi, h)),
        out_shape=jax.ShapeDtypeStruct((b, t, ATT_WIDTH), BF16),
        compiler_params=_params(("parallel", "parallel", "parallel", "arbitrary")),
    )
    qkv_specs = [
        pl.BlockSpec((None, tq, blk), lambda bi, h, qi, ki: (bi, qi, h)),
        pl.BlockSpec((None, tk, blk), lambda bi, h, qi, ki: (bi, ki, h)),
        pl.BlockSpec((LANES, tk), lambda bi, h, qi, ki: (h, bi * nk + ki)),
    ]
    smem = pl.BlockSpec(memory_space=pltpu.SMEM)
    idx = np.arange(tk)[:, None] - np.arange(tq)[None, :]
    absd = jnp.asarray(np.abs(idx), dtype=F32)
    slopes2 = np.exp2(-8.0 * (np.arange(ATT_HEADS) + 1.0) / ATT_HEADS) * LOG2E
    window = tuple(int((EXP2_ZERO_ARG / s - 1.0) // tk) + 1 for s in slopes2)
    def fast_head(h, q, k, vt, scal, subg):
        reach = window[h]
        band = 2 * reach + 1 < nk
        nsteps = 2 * reach + 1 if band else nk
        tile = lambda qi, st: _attn_key_tile(qi, st, nk, reach, band)[0]
        return pl.pallas_call(
            functools.partial(_attn_fast_kernel, tq=tq, tk=tk, nk=nk, head=h, reach=reach, band=band,
                              nsteps=nsteps),
            grid=(b, nq, nsteps),
            in_specs=[
                smem,
                pl.BlockSpec((None, tq, blk), lambda bi, qi, st: (bi, qi, h)),
                pl.BlockSpec((None, tk, blk), lambda bi, qi, st: (bi, tile(qi, st), h)),
                pl.BlockSpec((LANES, tk), lambda bi, qi, st: (h, bi * nk + tile(qi, st))),
                _const_spec((tk, tq)), _const_spec((ATT_V_DIM, 1)),
            ],
            out_specs=pl.BlockSpec((None, tq, LANES), lambda bi, qi, st: (bi, qi, 0)),
            out_shape=jax.ShapeDtypeStruct((b, t, LANES), BF16),
            scratch_shapes=[
                pltpu.VMEM((6, tq, LANES), BF16),
                pltpu.VMEM((2, 8, tq), F32),
                pltpu.VMEM((2, ATT_V_DIM, tq), F32),
            ],
            compiler_params=_params(("parallel", "parallel", "arbitrary")),
            name="diff_attn_h%d" % h,
        )(scal, q, k, vt, absd, subg)

    def fast(q, k, vt, scal, subg):
        return tuple(fast_head(h, q, k, vt, scal, subg) for h in range(ATT_HEADS))

    def safe(q, k, vt, scal, subg):
        o = safe_call(q, k, vt, scal, subg)
        return tuple(o[:, :, LANES * h:LANES * (h + 1)] for h in range(ATT_HEADS))

    def safe_call(q, k, vt, scal, subg):
        return pl.pallas_call(
            functools.partial(_attn_safe_kernel, tq=tq, tk=tk, nk=nk),
            in_specs=[smem] + qkv_specs + [_const_spec((ATT_V_DIM, 1))],
            scratch_shapes=[
                pltpu.VMEM((2, tq, LANES), BF16),
                pltpu.VMEM((2, 1, tq), F32),
                pltpu.VMEM((2, 1, tq), F32),
                pltpu.VMEM((2, ATT_V_DIM, tq), F32),
            ],
            name="diff_attn_safe",
            **common,
        )(scal, q, k, vt, subg)

    return lax.cond(fast_ok, fast, safe, q, k, vt, scal, subg)


N_LEVELS = 6


def _hgrn_constants(tt):
    c = HG_CHUNK
    t = np.arange(tt)
    same_chunk = (t[:, None] // c) == (t[None, :] // c)
    mats, masks = [], []
    for j in range(N_LEVELS):
        m = 1 << j
        blk = t // m
        same_blk = blk[:, None] == blk[None, :]
        odd = (blk % 2) == 1
        incl = same_blk & (t[None, :] <= t[:, None])
        excl_rev = same_blk & (t[None, :] > t[:, None])
        mats.append(np.where(odd[:, None], incl, excl_rev))
        same_2m = (t[:, None] // (2 * m)) == (t[None, :] // (2 * m))
        masks.append((same_2m & odd[:, None] & (~odd)[None, :]).astype(np.float32))
    mats.append(same_chunk & (t[None, :] <= t[:, None]))
    mats.append(same_chunk & (t[None, :] > t[:, None]))
    mats = np.stack(mats).astype(np.float32)
    masks = np.stack(masks)
    mats_b = mats[:, ::-1, ::-1]
    masks_b = masks[:, ::-1, ::-1]
    mst = np.concatenate([mats.reshape(-1, tt), mats_b.reshape(-1, tt)], axis=0)
    amask = np.concatenate([masks, masks_b], axis=0)
    return jnp.asarray(mst, dtype=BF16), jnp.asarray(amask, dtype=F32)


def _hgrn_kernel(hpf_ref, hpb_ref, lgf_ref, lgb_ref, mst_ref, amask_ref, g256_ref, of_ref, ob_ref, s_ref, *, tt):
    i = pl.program_id(1)
    nlev = N_LEVELS
    nmat = N_LEVELS + 2
    nchunk = tt // HG_CHUNK

    @pl.when(i == 0)
    def _():
        s_ref[...] = jnp.zeros(s_ref.shape, F32)

    lane256 = lax.broadcasted_iota(jnp.int32, (1, 256), 1)
    lane128 = lax.broadcasted_iota(jnp.int32, (1, LANES), 1)
    r128 = lax.broadcasted_iota(jnp.int32, (LANES, LANES), 0)
    c128 = lax.broadcasted_iota(jnp.int32, (LANES, LANES), 1)
    bdmask = ((r128 // HG_HEAD_DIM) == (c128 // HG_HEAD_DIM)).astype(F32)

    def direction(d, hp_ref, lg_ref, o_ref):
        qs = hp_ref[:, 0:256].astype(F32)
        v = hp_ref[:, 256:512]
        kk = hp_ref[:, 512 + 256 * d:768 + 256 * d].astype(F32)
        eall = jnp.dot(mst_ref[d * nmat * tt:(d + 1) * nmat * tt, :], lg_ref[...].astype(BF16),
                       preferred_element_type=F32)

        def expo(j):
            return jnp.exp(eall[j * tt:(j + 1) * tt])

        a = [None] * 4
        for lev in range(nlev):
            xx = expo(lev)
            qt, kt = (qs * xx).astype(BF16), (kk * xx).astype(BF16)
            am = amask_ref[d * nlev + lev]
            for pr in range(2):
                qp = qt[:, LANES * pr:LANES * (pr + 1)]
                kp = kt[:, LANES * pr:LANES * (pr + 1)]
                for hh in range(2):
                    sel = (lane128 // HG_HEAD_DIM) == hh
                    qm = jnp.where(sel, qp, jnp.zeros_like(qp))
                    p = lax.dot_general(qm, kp, _NT, preferred_element_type=F32) * am
                    idx = 2 * pr + hh
                    a[idx] = p if a[idx] is None else a[idx] + p

        o = jnp.dot((qs * kk).astype(BF16), g256_ref[...], preferred_element_type=F32) * v.astype(F32)
        for idx in range(4):
            vm = jnp.where((lane256 // HG_HEAD_DIM) == idx, v, jnp.zeros_like(v))
            contrib = jnp.dot(a[idx].astype(BF16), vm, preferred_element_type=F32)
            o = o + contrib
        o_ref[...] = o

        xq = expo(N_LEVELS)
        xk = expo(N_LEVELS + 1)
        qc = (qs * xq).astype(BF16)
        kc = (kk * xk).astype(BF16)
        order = range(nchunk) if d == 0 else range(nchunk - 1, -1, -1)
        for c in order:
            r0 = c * HG_CHUNK
            rows = slice(r0, r0 + HG_CHUNK)
            drow = r0 + HG_CHUNK - 1 if d == 0 else r0
            for pr in range(2):
                cols = slice(LANES * pr, LANES * (pr + 1))
                st = s_ref[d, pr]
                inter = lax.dot_general(qc[rows, cols], st.astype(BF16), _NT, preferred_element_type=F32)
                o_ref[rows, cols] += inter
                ut = lax.dot_general(v[rows, cols], kc[rows, cols], _TN, preferred_element_type=F32)
                dec = xq[drow:drow + 1, cols]
                s_ref[d, pr] = st * dec + ut * bdmask

    direction(0, hpf_ref, lgf_ref, of_ref)
    direction(1, hpb_ref, lgb_ref, ob_ref)


def _hgrn(hp, lg, b, t, tt):
    nt = t // tt
    mst, amask = _hgrn_constants(tt)
    kern = functools.partial(_hgrn_kernel, tt=tt)
    fwd = lambda w, cb: pl.BlockSpec((None, tt, w), lambda bi, i: (bi, i, cb))
    bwd = lambda w, cb: pl.BlockSpec((None, tt, w), lambda bi, i: (bi, nt - 1 - i, cb))
    return pl.pallas_call(
        kern,
        grid=(b, nt),
        in_specs=[fwd(1024, 0), bwd(1024, 0), fwd(256, 0), bwd(256, 1),
                  _const_spec(mst.shape), _const_spec(amask.shape), _const_spec((256, 256))],
        out_specs=(fwd(256, 0), bwd(256, 0)),
        out_shape=(jax.ShapeDtypeStruct((b, t, HG_WIDTH), F32), jax.ShapeDtypeStruct((b, t, HG_WIDTH), F32)),
        scratch_shapes=[pltpu.VMEM((2, 2, LANES, LANES), F32)],
        compiler_params=_params(("parallel", "arbitrary")),
        name="hgrn2",
    )(hp, hp, lg, lg, mst, amask, _block_ones(256, HG_HEAD_DIM))


def _merge_kernel(*refs, tm, seq, with_router):
    oa_refs = refs[:ATT_HEADS]
    refs = refs[ATT_HEADS:]
    (x_ref, cv_ref, cvp_ref, cvn_ref, of_ref, ob_ref, sg_ref, gate_ref, cw_ref, hgn_ref, g256_ref,
     wa_ref, wb_ref, wc_ref, wo_ref, g2_ref) = refs[:16]
    if with_router:
        wr_ref, xo_ref, h2_ref, lo_ref = refs[16:]
    else:
        xo_ref, h2_ref = refs[16:]
    i = pl.program_id(0)
    tile_start = (i * tm) % seq

    u = cv_ref[:, 256:512].astype(F32)
    row = lax.broadcasted_iota(jnp.int32, (tm, 1), 0)
    prev_row = jnp.where(tile_start == 0, 0.0, cvp_ref[7:8, 256:512].astype(F32))
    next_row = jnp.where(tile_start + tm == seq, 0.0, cvn_ref[0:1, 256:512].astype(F32))
    u_m1 = jnp.where(row == 0, prev_row, pltpu.roll(u, 1, axis=0))
    u_p1 = jnp.where(row == tm - 1, next_row, pltpu.roll(u, tm - 1, axis=0))
    conv = cw_ref[0:1, :] * u_m1 + cw_ref[1:2, :] * u + cw_ref[2:3, :] * u_p1
    yb_in = (cv_ref[:, 0:256].astype(F32) * conv).astype(BF16)

    o = of_ref[...] + ob_ref[...]
    ss = _group_sum(o * o, g256_ref[...])
    yc_in = (o * lax.rsqrt(ss * (1.0 / HG_HEAD_DIM) + EPS) * hgn_ref[...] * sg_ref[...].astype(F32)).astype(BF16)

    oa = jnp.concatenate([r[...] for r in oa_refs], axis=1)
    ya = jnp.dot(oa, wa_ref[...], preferred_element_type=F32)
    yb = jnp.dot(yb_in, wb_ref[...], preferred_element_type=F32)
    yc = jnp.dot(yc_in, wc_ref[...], preferred_element_type=F32)
    merged = (gate_ref[:, 0:1024].astype(F32) * ya + gate_ref[:, 1024:2048].astype(F32) * yb
              + gate_ref[:, 2048:3072].astype(F32) * yc)
    xn = x_ref[...] + jnp.dot(merged.astype(BF16), wo_ref[...], preferred_element_type=F32)
    xo_ref[...] = xn

    ms = jnp.mean(xn * xn, axis=-1, keepdims=True)
    h2 = xn * lax.rsqrt(ms + EPS) * g2_ref[...]
    h2_ref[...] = h2.astype(BF16)
    if with_router:
        lo_ref[...] = lax.dot_general(wr_ref[...], h2, _NT, preferred_element_type=F32,
                                      precision=lax.Precision.HIGHEST)


def _merge(x, oa, cv, of, ob, sg, gates, conv_w, hg_norm, wa, wb, wc, wo, g2, w_router, seq, tm):
    n = x.shape[0]
    nb8 = n // 8
    step8 = tm // 8
    row = lambda w: pl.BlockSpec((tm, w), lambda i: (i, 0))
    in_specs = [
        row(LANES), row(LANES), row(LANES), row(LANES), row(D_MODEL), row(512),
        pl.BlockSpec((8, 512), lambda i: (jnp.maximum(i * step8 - 1, 0), 0)),
        pl.BlockSpec((8, 512), lambda i: (jnp.minimum((i + 1) * step8, nb8 - 1), 0)),
        row(256), row(256), row(256), row(3072),
        _const_spec((8, 256)), _const_spec((1, 256)), _const_spec((256, 256)),
        _const_spec(wa.shape), _const_spec(wb.shape), _const_spec(wc.shape), _const_spec(wo.shape),
        _const_spec((1, D_MODEL)),
    ]
    args = list(oa) + [x, cv, cv, cv, of, ob, sg, gates, conv_w, hg_norm, _block_ones(256, HG_HEAD_DIM),
            wa, wb, wc, wo, g2]
    out_specs = [row(D_MODEL), row(D_MODEL)]
    out_shape = [jax.ShapeDtypeStruct((n, D_MODEL), F32), jax.ShapeDtypeStruct((n, D_MODEL), BF16)]
    with_router = w_router is not None
    if with_router:
        in_specs.append(_const_spec(w_router.shape))
        args.append(w_router)
        out_specs.append(pl.BlockSpec((N_EXPERTS, tm), lambda i: (0, i)))
        out_shape.append(jax.ShapeDtypeStruct((N_EXPERTS, n), F32))
    kern = functools.partial(_merge_kernel, tm=tm, seq=seq, with_router=with_router)
    return pl.pallas_call(
        kern,
        grid=(n // tm,),
        in_specs=in_specs,
        out_specs=tuple(out_specs),
        out_shape=tuple(out_shape),
        compiler_params=_params(("parallel",)),
        name="merge_router" if with_router else "merge",
    )(*args)


def _ffn_kernel(x_ref, h_ref, wg_ref, wu_ref, wd_ref, o_ref):
    h = h_ref[...]
    g = jnp.dot(h, wg_ref[...], preferred_element_type=F32)
    u = jnp.dot(h, wu_ref[...], preferred_element_type=F32)
    act = (jax.nn.silu(g) * u).astype(BF16)
    o_ref[...] = x_ref[...] + jnp.dot(act, wd_ref[...], preferred_element_type=F32)


def _ffn(x, h2, wg, wu, wd, tm):
    n = x.shape[0]
    row = lambda: pl.BlockSpec((tm, D_MODEL), lambda i: (i, 0))
    return pl.pallas_call(
        _ffn_kernel,
        grid=(n // tm,),
        in_specs=[row(), row(), _const_spec(wg.shape), _const_spec(wu.shape), _const_spec(wd.shape)],
        out_specs=row(),
        out_shape=jax.ShapeDtypeStruct((n, D_MODEL), F32),
        compiler_params=_params(("parallel",)),
        name="ffn_dense",
    )(x, h2, wg, wu, wd)


MOE_ROWS = 144


def _moe_expert_kernel(lo_ref, h_ref, y_ref, tri_ref, wg_ref, wu_ref, wd_ref, o_ref, *, expert, tm):
    lg = lo_ref[...]
    sub = lax.broadcasted_iota(jnp.int32, lg.shape, 0)
    m1 = jnp.max(lg, axis=0, keepdims=True)
    i1 = jnp.min(jnp.where(lg == m1, sub, N_EXPERTS), axis=0, keepdims=True)
    lg2 = jnp.where(sub == i1, NEG_BIG, lg)
    m2 = jnp.max(lg2, axis=0, keepdims=True)
    i2 = jnp.min(jnp.where(lg2 == m2, sub, N_EXPERTS), axis=0, keepdims=True)
    e2 = jnp.exp(m2 - m1)
    w_row = jnp.where(i1 == expert, 1.0 / (1.0 + e2), 0.0) + jnp.where(i2 == expert, e2 / (1.0 + e2), 0.0)
    routed = (i1 == expert) | (i2 == expert)
    routed_f = routed.astype(F32)
    nblk = tm // LANES
    by_block = jnp.concatenate([routed_f[:, LANES * j:LANES * (j + 1)] for j in range(nblk)], axis=0)
    within = jnp.dot(by_block.astype(BF16), tri_ref[...], preferred_element_type=F32)
    totals = jnp.broadcast_to(within[:, LANES - 1:LANES], (nblk, LANES))
    blk_id = lax.broadcasted_iota(jnp.int32, (nblk, LANES), 0)
    before = jnp.zeros((nblk, LANES), F32)
    for j in range(nblk - 1):
        before = before + jnp.where(blk_id > j, totals[j:j + 1, :], 0.0)
    rank2d = (within + before).astype(jnp.int32) - 1
    rank = jnp.concatenate([rank2d[j:j + 1, :] for j in range(nblk)], axis=1)
    count = jnp.sum(routed_f).astype(jnp.int32)

    h = h_ref[...]

    def expert_rows(blk):
        rows = blk * MOE_ROWS + lax.broadcasted_iota(jnp.int32, (MOE_ROWS, tm), 0)
        sel = (rank == rows) & routed
        sel_b = sel.astype(F32).astype(BF16)
        xe = jnp.dot(sel_b, h, preferred_element_type=F32).astype(BF16)
        g = jnp.dot(xe, wg_ref[...], preferred_element_type=F32)
        u = jnp.dot(xe, wu_ref[...], preferred_element_type=F32)
        act = (jax.nn.silu(g) * u).astype(BF16)
        ye = jnp.dot(act, wd_ref[...], preferred_element_type=F32)
        w_sel = jnp.sum(jnp.where(sel, w_row, 0.0), axis=1, keepdims=True)
        ye = (ye * w_sel).astype(BF16)
        return lax.dot_general(sel_b, ye, _TN, preferred_element_type=F32)

    o_ref[...] = y_ref[...] + expert_rows(0)

    def body(blk, carry):
        o_ref[...] += expert_rows(blk)
        return carry

    lax.fori_loop(1, (count + MOE_ROWS - 1) // MOE_ROWS, body, 0)


def _moe(x, h2, logits_t, wg, wu, wd, tm):
    n = x.shape[0]
    row = lambda: pl.BlockSpec((tm, D_MODEL), lambda i: (i, 0))
    t = np.arange(LANES)
    tri = jnp.asarray((t[:, None] <= t[None, :]).astype(np.float32), dtype=BF16)
    expert_w = lambda w, e: pl.BlockSpec((None,) + w.shape[1:], lambda i: (e, 0, 0), pipeline_mode=pl.Buffered(1))
    y = x
    for e in range(N_EXPERTS):
        y = pl.pallas_call(
            functools.partial(_moe_expert_kernel, expert=e, tm=tm),
            grid=(n // tm,),
            in_specs=[pl.BlockSpec((N_EXPERTS, tm), lambda i: (0, i)), row(), row(),
                      _const_spec((LANES, LANES)),
                      expert_w(wg, e), expert_w(wu, e), expert_w(wd, e)],
            out_specs=row(),
            out_shape=jax.ShapeDtypeStruct((n, D_MODEL), F32),
            input_output_aliases={2: 0},
            compiler_params=_params(("parallel",)),
            name="moe_expert",
        )(logits_t, h2, y, tri, wg, wu, wd)
    return y


def _pick(limit, total):
    return min(limit, total)


def kernel(x_prompt, x_sample, norm_mix, w_in, q_norm, k_norm, lambda_q1, lambda_k1, lambda_q2, lambda_k2, sub_norm, conv_w, hg_lower, hg_norm, w_up_a, w_up_b, w_up_c, w_out, norm_ffn, w_gate_dense, w_up_dense, w_down_dense, w_router, w_gate_moe, w_up_moe, w_down_moe):
    depth = norm_mix.shape[0]
    lb_all = jnp.cumsum(jax.nn.softmax(hg_lower.astype(F32), axis=0), axis=0)
    lb_all = lb_all - lb_all[0]
    slopes = jnp.exp2(-8.0 * (jnp.arange(ATT_HEADS, dtype=F32) + 1.0) / ATT_HEADS)

    layers = []
    for l in range(depth):
        lam_init = 0.8 - 0.6 * math.exp(-0.3 * l)
        lam = (jnp.exp(jnp.sum(lambda_q1[l].astype(F32) * lambda_k1[l].astype(F32)))
               - jnp.exp(jnp.sum(lambda_q2[l].astype(F32) * lambda_k2[l].astype(F32))) + lam_init)
        wl = w_in[l]
        qtab, ktab, fast_ok = _alibi_tables(q_norm[l].astype(F32), k_norm[l].astype(F32))
        p = dict(
            gain=norm_mix[l].reshape(1, D_MODEL),
            w_main=wl[:, 1536:].astype(BF16),
            w_vt=wl[:, 1024:1536].T.astype(BF16),
            w_qx=_expand_heads(wl[:, 0:512]).astype(BF16),
            w_kx=_expand_heads(wl[:, 512:1024]).astype(BF16),
            qtab=qtab, ktab=ktab, fast_ok=fast_ok,
            lb=lb_all[l],
            scal=jnp.concatenate([slopes * LOG2E, lam.reshape(1), jnp.zeros((3,), F32)]),
            subg=(sub_norm[l] * (1.0 - lam_init)).reshape(ATT_V_DIM, 1),
            conv_w=jnp.concatenate([conv_w[l], jnp.zeros((5, SC_WIDTH), F32)], axis=0),
            hg_norm=hg_norm[l].reshape(1, HG_WIDTH),
            wa=w_up_a[l].astype(BF16), wb=w_up_b[l].astype(BF16), wc=w_up_c[l].astype(BF16),
            wo=w_out[l].astype(BF16),
            g2=norm_ffn[l].reshape(1, D_MODEL),
        )
        j = l // 2
        if l % 2 == 0:
            p.update(wg=w_gate_dense[j].astype(BF16), wu=w_up_dense[j].astype(BF16),
                     wd=w_down_dense[j].astype(BF16))
        else:
            p.update(w_router=w_router[j].T.astype(F32),
                     wg=w_gate_moe[j].astype(BF16), wu=w_up_moe[j].astype(BF16), wd=w_down_moe[j].astype(BF16))
        layers.append(p)

    def trunk(x3):
        b, t, _ = x3.shape
        n = b * t
        tm = _pick(512, t)
        tq = _pick(1024, t)
        tt = _pick(256, t)
        x = x3.reshape(n, D_MODEL)
        for l, p in enumerate(layers):
            q, k, vt, cv, hp, lg, sg, gates = _inproj(x, p["gain"], p["w_main"], p["w_vt"], p["w_qx"], p["w_kx"],
                                                      p["qtab"], p["ktab"], p["lb"], t, tm)
            oa = _attention(q.reshape(b, t, 1024), k.reshape(b, t, 1024), vt, p["scal"], p["subg"], p["fast_ok"],
                            b, t, tq, tq)
            of, ob = _hgrn(hp.reshape(b, t, 1024), lg.reshape(b, t, 512), b, t, tt)
            res = _merge(x, [o.reshape(n, LANES) for o in oa], cv, of.reshape(n, 256), ob.reshape(n, 256), sg, gates,
                         p["conv_w"], p["hg_norm"], p["wa"], p["wb"], p["wc"], p["wo"], p["g2"],
                         p.get("w_router"), t, tm)
            if l % 2 == 0:
                xn, h2 = res
                x = _ffn(xn, h2, p["wg"], p["wu"], p["wd"], tm)
            else:
                xn, h2, logits = res
                x = _moe(xn, h2, logits, p["wg"], p["wu"], p["wd"], _pick(1024, t))
        return x.reshape(b, t, D_MODEL)

    return (trunk(x_prompt), trunk(x_sample))
```

```python
import functools
import math

import numpy as np
import jax
import jax.numpy as jnp
from jax import lax
from jax.experimental import pallas as pl
from jax.experimental.pallas import tpu as pltpu

F32 = jnp.float32
BF16 = jnp.bfloat16

D_MODEL = 1024
ATT_HEADS = 4
ATT_HEAD_DIM = 64
ATT_V_DIM = 128
ATT_QK_WIDTH = 512
ATT_WIDTH = 512
SC_WIDTH = 256
HG_WIDTH = 256
HG_HEAD_DIM = 64
HG_CHUNK = 64
N_BRANCH = 3
N_EXPERTS = 8
EPS = 1e-6

LOG2E = math.log2(math.e)
N_SLOPE_PIECES = 3
SCORE_BOUND_LIMIT = 48.0
EXP2_ZERO_ARG = 151.0

LANES = 128
POS_SHIFT = 7
VMEM_LIMIT = 56 * 1024 * 1024
NEG_BIG = -1e30

_NT = (((1,), (1,)), ((), ()))
_TN = (((0,), (0,)), ((), ()))


def _const_spec(shape):
    nd = len(shape)
    return pl.BlockSpec(shape, lambda *_: (0,) * nd, pipeline_mode=pl.Buffered(1))


def _params(sem):
    return pltpu.CompilerParams(dimension_semantics=sem, vmem_limit_bytes=VMEM_LIMIT)


def _split_bf16(x):
    hi = x.astype(BF16)
    lo = (x - hi.astype(F32)).astype(BF16)
    return hi, lo


def _group_sum(x, gmat):
    hi, lo = _split_bf16(x)
    return (jnp.dot(hi, gmat, preferred_element_type=F32) + jnp.dot(lo, gmat, preferred_element_type=F32))


def _block_ones(width, group):
    idx = np.arange(width) // group
    return jnp.asarray((idx[:, None] == idx[None, :]).astype(np.float32), dtype=BF16)


def _inproj_kernel(x_ref, g_ref, wm_ref, wvt_ref, wqk_ref, qtab_ref, ktab_ref, lb_ref,
                   q_ref, k_ref, vt_ref, cv_ref, hp_ref, lg_ref, sg_ref, gate_ref, *, tm, seq):
    x = x_ref[...]
    ms = jnp.mean(x * x, axis=-1, keepdims=True)
    h = (x * lax.rsqrt(ms + EPS) * g_ref[...]).astype(BF16)

    vt_ref[...] = lax.dot_general(wvt_ref[...], h, _NT, preferred_element_type=F32).astype(BF16)

    pos = (pl.program_id(0) * tm) % seq + lax.broadcasted_iota(jnp.int32, (tm, 1), 0)
    pos_lo = (pos & (LANES - 1)).astype(F32)
    pos_hi = (pos >> POS_SHIFT).astype(F32)

    first_half = lax.broadcasted_iota(jnp.int32, (1, LANES), 1) < ATT_HEAD_DIM

    def head_norm(raw, tab_ref, out_ref):
        for hd in range(ATT_HEADS):
            slab = raw[:, LANES * hd:LANES * (hd + 1)]
            sq = slab * slab
            ss_all = jnp.sum(sq, axis=-1, keepdims=True)
            ss_0 = jnp.sum(jnp.where(first_half, sq, 0.0), axis=-1, keepdims=True)
            ss = jnp.where(first_half, ss_0, ss_all - ss_0)
            normed = slab * lax.rsqrt(ss * (1.0 / ATT_HEAD_DIM) + EPS)
            for c in range(2):
                cols = slice(LANES * (2 * hd + c), LANES * (2 * hd + c + 1))
                data = normed if c == 0 else pltpu.roll(normed, ATT_HEAD_DIM, axis=1)
                aux = tab_ref[1:2, cols] + tab_ref[2:3, cols] * pos_lo + tab_ref[3:4, cols] * pos_hi
                out_ref[:, cols] = jnp.where(first_half, data * tab_ref[0:1, cols], aux).astype(BF16)

    qk = jnp.dot(h, wqk_ref[...], preferred_element_type=F32)
    head_norm(qk[:, 0:ATT_QK_WIDTH], qtab_ref, q_ref)
    head_norm(qk[:, ATT_QK_WIDTH:2 * ATT_QK_WIDTH], ktab_ref, k_ref)

    sc = jnp.dot(h, wm_ref[:, 0:768], preferred_element_type=F32)
    cv_ref[:, 0:256] = sc[:, 0:256].astype(BF16)
    cv_ref[:, 256:512] = (sc[:, 256:512] * sc[:, 512:768]).astype(BF16)

    hg = jnp.dot(h, wm_ref[:, 768:2048], preferred_element_type=F32)
    hp_ref[:, 0:256] = jax.nn.silu(hg[:, 0:256]).astype(BF16)
    hp_ref[:, 256:512] = hg[:, 256:512].astype(BF16)
    for d in range(2):
        z = hg[:, 512 + 256 * d:768 + 256 * d]
        lb = lb_ref[d:d + 1, :]
        f = lb + (1.0 - lb) * jax.nn.sigmoid(z)
        hp_ref[:, 512 + 256 * d:768 + 256 * d] = ((1.0 - lb) * jax.nn.sigmoid(-z)).astype(BF16)
        lg_ref[:, 256 * d:256 * d + 256] = jnp.log(f)
    sg_ref[...] = jax.nn.silu(hg[:, 1024:1280]).astype(BF16)

    gate_ref[...] = jax.nn.sigmoid(
        jnp.dot(h, wm_ref[:, 2048:5120], preferred_element_type=F32)).astype(BF16)


def _inproj(x, gain, w_main, w_vt, w_qk, qtab, ktab, lb, seq, tm):
    n = x.shape[0]
    row = lambda w: pl.BlockSpec((tm, w), lambda i: (i, 0))
    out_shape = (
        jax.ShapeDtypeStruct((n, 1024), BF16),
        jax.ShapeDtypeStruct((n, 1024), BF16),
        jax.ShapeDtypeStruct((512, n), BF16),
        jax.ShapeDtypeStruct((n, 512), BF16),
        jax.ShapeDtypeStruct((n, 1024), BF16),
        jax.ShapeDtypeStruct((n, 512), F32),
        jax.ShapeDtypeStruct((n, 256), BF16),
        jax.ShapeDtypeStruct((n, 3072), BF16),
    )
    kern = functools.partial(_inproj_kernel, tm=tm, seq=seq)
    return pl.pallas_call(
        kern,
        grid=(n // tm,),
        in_specs=[row(D_MODEL), _const_spec((1, D_MODEL)), _const_spec(w_main.shape), _const_spec(w_vt.shape),
                  _const_spec(w_qk.shape), _const_spec((8, 1024)), _const_spec((8, 1024)),
                  _const_spec((2, 256))],
        out_specs=(row(1024), row(1024), pl.BlockSpec((512, tm), lambda i: (0, i)), row(512), row(1024),
                   row(512), row(256), row(3072)),
        out_shape=out_shape,
        compiler_params=_params(("parallel",)),
        name="inproj",
    )(x, gain, w_main, w_vt, w_qk, qtab, ktab, lb)


def _alibi_tables(q_gain, k_gain):
    slopes = np.exp2(-8.0 * (np.arange(ATT_HEADS, dtype=np.float64) + 1.0) / ATT_HEADS) * LOG2E
    qc = np.zeros((8, 1024), np.float32)
    kc = np.zeros((8, 1024), np.float32)
    for h in range(ATT_HEADS):
        rest = np.float32(slopes[h])
        pieces = []
        for _ in range(N_SLOPE_PIECES):
            piece = np.asarray(rest, dtype=BF16).astype(np.float32)
            pieces.append(float(piece))
            rest = np.float32(rest - piece)
        for c in range(2):
            base = LANES * (2 * h + c) + ATT_HEAD_DIM
            for i, a in enumerate(pieces):
                qc[1, base + 2 * i] = -a
                qc[1, base + 2 * i + 1] = -a * LANES
                kc[2, base + 2 * i] = 1.0
                kc[3, base + 2 * i + 1] = 1.0
                off = base + 2 * N_SLOPE_PIECES
                qc[2, off + 2 * i] = 1.0
                qc[3, off + 2 * i + 1] = 1.0
                kc[1, off + 2 * i] = a
                kc[1, off + 2 * i + 1] = a * LANES
    lane = np.arange(1024) % LANES
    lane_is_data = lane < ATT_HEAD_DIM
    qg = jnp.where(lane_is_data, jnp.tile(q_gain, 16) * (LOG2E * ATT_HEAD_DIM ** -0.5), 0.0)
    kg = jnp.where(lane_is_data, jnp.tile(k_gain, 16), 0.0)
    bound = 1.02 * LOG2E * ATT_HEAD_DIM ** 0.5 * jnp.max(jnp.abs(q_gain)) * jnp.max(jnp.abs(k_gain))
    qconst = jnp.asarray(qc[1])
    rest = bound.astype(F32)
    bound_lane0 = ATT_HEAD_DIM + 4 * N_SLOPE_PIECES
    for i in range(N_SLOPE_PIECES):
        piece = rest.astype(BF16).astype(F32)
        rest = rest - piece
        qconst = jnp.where(lane == bound_lane0 + i, -piece, qconst)
        kc[1, lane == bound_lane0 + i] = 1.0
    qtab = jnp.asarray(qc).at[0].set(qg).at[1].set(qconst)
    ktab = jnp.asarray(kc).at[0].set(kg)
    return qtab, ktab, bound <= SCORE_BOUND_LIMIT


def _attn_finish(sc_ref, subg_ref, o_ref, l0, l1, acc_ref):
    lam = sc_ref[ATT_HEADS]
    o = acc_ref[0] / l0 - lam * (acc_ref[1] / l1)
    ms = jnp.mean(o * o, axis=0, keepdims=True)
    o = o * lax.rsqrt(ms + EPS) * subg_ref[...]
    o_ref[...] = o.T.astype(BF16)


def _attn_key_tile(qi, st, nk, reach, band):
    if band:
        off = jnp.where(st <= reach, st, reach - st)
        kb = qi + off
        return jnp.clip(kb, 0, nk - 1), off > 0, (kb >= 0) & (kb < nk)
    return (qi + st) % nk, qi + st < nk, st >= 0


def _attn_fast_kernel(sc_ref, q_ref, k_ref, vt_ref, absd_ref, subg_ref, o_ref, qv_ref, l_ref, acc_ref,
                      *, tq, tk, nk, head, reach, band, nsteps):
    h = head
    qi = pl.program_id(1)
    st = pl.program_id(2)
    _, after, exists = _attn_key_tile(qi, st, nk, reach, band)

    @pl.when(st == 0)
    def _():
        lane = lax.broadcasted_iota(jnp.int32, (tq, LANES), 1)
        alibi = (lane >= ATT_HEAD_DIM) & (lane < ATT_HEAD_DIM + 4 * N_SLOPE_PIECES)
        for c in range(2):
            qc = q_ref[:, LANES * c:LANES * (c + 1)]
            qv_ref[c] = qc
            qv_ref[2 + c] = jnp.where(alibi, -qc, qc)
            qv_ref[4 + c] = jnp.where(alibi, jnp.zeros_like(qc), qc)
        l_ref[...] = jnp.zeros(l_ref.shape, F32)
        acc_ref[...] = jnp.zeros(acc_ref.shape, F32)

    def step(diag):
        vt = vt_ref[...]
        for c in range(2):
            kc = k_ref[:, LANES * c:LANES * (c + 1)]
            if diag:
                qsel = qv_ref[4 + c]
            else:
                qsel = qv_ref[jnp.where(after, 0, 2) + c]
            s = lax.dot_general(kc, qsel, _NT, preferred_element_type=F32)
            if diag:
                s = s - sc_ref[h] * absd_ref[...]
            p = jnp.exp2(s)
            l_ref[c] += jnp.sum(p.reshape(tk // 8, 8, tq), axis=0)
            acc_ref[c] += jnp.dot(vt, p.astype(BF16), preferred_element_type=F32)

    @pl.when(st == 0)
    def _():
        step(True)

    @pl.when((st != 0) & exists)
    def _():
        step(False)

    @pl.when(st == nsteps - 1)
    def _():
        _attn_finish(sc_ref, subg_ref, o_ref, jnp.sum(l_ref[0], axis=0, keepdims=True),
                     jnp.sum(l_ref[1], axis=0, keepdims=True), acc_ref)


def _attn_safe_kernel(sc_ref, q_ref, k_ref, vt_ref, subg_ref, o_ref, qm_ref, m_ref, l_ref, acc_ref, *, tq, tk, nk):
    h = pl.program_id(1)
    qi = pl.program_id(2)
    ki = pl.program_id(3)

    @pl.when(ki == 0)
    def _():
        lane = lax.broadcasted_iota(jnp.int32, (tq, LANES), 1)
        for c in range(2):
            qc = q_ref[:, LANES * c:LANES * (c + 1)]
            qm_ref[c] = jnp.where(lane < ATT_HEAD_DIM, qc, jnp.zeros_like(qc))
        m_ref[...] = jnp.full(m_ref.shape, NEG_BIG, F32)
        l_ref[...] = jnp.zeros(l_ref.shape, F32)
        acc_ref[...] = jnp.zeros(acc_ref.shape, F32)

    kpos = ki * tk + lax.broadcasted_iota(jnp.int32, (tk, tq), 0)
    qpos = qi * tq + lax.broadcasted_iota(jnp.int32, (tk, tq), 1)
    bias = sc_ref[h] * jnp.abs(kpos - qpos).astype(F32)
    vt = vt_ref[...]
    for c in range(2):
        kc = k_ref[:, LANES * c:LANES * (c + 1)]
        s = lax.dot_general(kc, qm_ref[c], _NT, preferred_element_type=F32) - bias
        m_prev = m_ref[c]
        m_new = jnp.maximum(m_prev, jnp.max(s, axis=0, keepdims=True))
        alpha = jnp.exp2(m_prev - m_new)
        p = jnp.exp2(s - m_new)
        l_ref[c] = alpha * l_ref[c] + jnp.sum(p, axis=0, keepdims=True)
        acc_ref[c] = alpha * acc_ref[c] + jnp.dot(vt, p.astype(BF16), preferred_element_type=F32)
        m_ref[c] = m_new

    @pl.when(ki == nk - 1)
    def _():
        _attn_finish(sc_ref, subg_ref, o_ref, l_ref[0], l_ref[1], acc_ref)


def _attention(q, k, vt, scal, subg, fast_ok, b, t, tq, tk):
    nq, nk = t // tq, t // tk
    blk = 2 * LANES
    common = dict(
        grid=(b, ATT_HEADS, nq, nk),
        out_specs=pl.BlockSpec((None, tq, LANES), lambda bi, h, qi, ki: (bi, qi, h)),
        out_shape=jax.ShapeDtypeStruct((b, t, ATT_WIDTH), BF16),
        compiler_params=_params(("parallel", "parallel", "parallel", "arbitrary")),
    )
    qkv_specs = [
        pl.BlockSpec((None, tq, blk), lambda bi, h, qi, ki: (bi, qi, h)),
        pl.BlockSpec((None, tk, blk), lambda bi, h, qi, ki: (bi, ki, h)),
        pl.BlockSpec((LANES, tk), lambda bi, h, qi, ki: (h, bi * nk + ki)),
    ]
    smem = pl.BlockSpec(memory_space=pltpu.SMEM)
    idx = np.arange(tk)[:, None] - np.arange(tq)[None, :]
    absd = jnp.asarray(np.abs(idx), dtype=F32)
    slopes2 = np.exp2(-8.0 * (np.arange(ATT_HEADS) + 1.0) / ATT_HEADS) * LOG2E
    window = tuple(int((EXP2_ZERO_ARG / s - 1.0) // tk) + 1 for s in slopes2)
    def fast_head(h, q, k, vt, scal, subg):
        reach = window[h]
        band = 2 * reach + 1 < nk
        nsteps = 2 * reach + 1 if band else nk
        tile = lambda qi, st: _attn_key_tile(qi, st, nk, reach, band)[0]
        return pl.pallas_call(
            functools.partial(_attn_fast_kernel, tq=tq, tk=tk, nk=nk, head=h, reach=reach, band=band,
                              nsteps=nsteps),
            grid=(b, nq, nsteps),
            in_specs=[
                smem,
                pl.BlockSpec((None, tq, blk), lambda bi, qi, st: (bi, qi, h)),
                pl.BlockSpec((None, tk, blk), lambda bi, qi, st: (bi, tile(qi, st), h)),
                pl.BlockSpec((LANES, tk), lambda bi, qi, st: (h, bi * nk + tile(qi, st))),
                _const_spec((tk, tq)), _const_spec((ATT_V_DIM, 1)),
            ],
            out_specs=pl.BlockSpec((None, tq, LANES), lambda bi, qi, st: (bi, qi, 0)),
            out_shape=jax.ShapeDtypeStruct((b, t, LANES), BF16),
            scratch_shapes=[
                pltpu.VMEM((6, tq, LANES), BF16),
                pltpu.VMEM((2, 8, tq), F32),
                pltpu.VMEM((2, ATT_V_DIM, tq), F32),
            ],
            compiler_params=_params(("parallel", "parallel", "arbitrary")),
            name="diff_attn_h%d" % h,
        )(scal, q, k, vt, absd, subg)

    def fast(q, k, vt, scal, subg):
        return tuple(fast_head(h, q, k, vt, scal, subg) for h in range(ATT_HEADS))

    def safe(q, k, vt, scal, subg):
        o = safe_call(q, k, vt, scal, subg)
        return tuple(o[:, :, LANES * h:LANES * (h + 1)] for h in range(ATT_HEADS))

    def safe_call(q, k, vt, scal, subg):
        return pl.pallas_call(
            functools.partial(_attn_safe_kernel, tq=tq, tk=tk, nk=nk),
            in_specs=[smem] + qkv_specs + [_const_spec((ATT_V_DIM, 1))],
            scratch_shapes=[
                pltpu.VMEM((2, tq, LANES), BF16),
                pltpu.VMEM((2, 1, tq), F32),
                pltpu.VMEM((2, 1, tq), F32),
                pltpu.VMEM((2, ATT_V_DIM, tq), F32),
            ],
            name="diff_attn_safe",
            **common,
        )(scal, q, k, vt, subg)

    return lax.cond(fast_ok, fast, safe, q, k, vt, scal, subg)


N_LEVELS = 6


def _hgrn_constants(tt):
    c = HG_CHUNK
    t = np.arange(tt)
    same_chunk = (t[:, None] // c) == (t[None, :] // c)
    mats, masks = [], []
    for j in range(N_LEVELS):
        m = 1 << j
        blk = t // m
        same_blk = blk[:, None] == blk[None, :]
        odd = (blk % 2) == 1
        incl = same_blk & (t[None, :] <= t[:, None])
        excl_rev = same_blk & (t[None, :] > t[:, None])
        mats.append(np.where(odd[:, None], incl, excl_rev))
        same_2m = (t[:, None] // (2 * m)) == (t[None, :] // (2 * m))
        masks.append((same_2m & odd[:, None] & (~odd)[None, :]).astype(np.float32))
    mats.append(same_chunk & (t[None, :] <= t[:, None]))
    mats.append(same_chunk & (t[None, :] > t[:, None]))
    mats = np.stack(mats).astype(np.float32)
    masks = np.stack(masks)
    mats_b = mats[:, ::-1, ::-1]
    masks_b = masks[:, ::-1, ::-1]
    mst = np.concatenate([mats.reshape(-1, tt), mats_b.reshape(-1, tt)], axis=0)
    amask = np.concatenate([masks, masks_b], axis=0)
    return jnp.asarray(mst, dtype=BF16), jnp.asarray(amask, dtype=F32)


def _hgrn_kernel(hpf_ref, hpb_ref, lgf_ref, lgb_ref, mst_ref, amask_ref, g256_ref, of_ref, ob_ref, s_ref, *, tt):
    i = pl.program_id(1)
    nlev = N_LEVELS
    nmat = N_LEVELS + 2
    nchunk = tt // HG_CHUNK

    @pl.when(i == 0)
    def _():
        s_ref[...] = jnp.zeros(s_ref.shape, F32)

    lane256 = lax.broadcasted_iota(jnp.int32, (1, 256), 1)
    lane128 = lax.broadcasted_iota(jnp.int32, (1, LANES), 1)
    r128 = lax.broadcasted_iota(jnp.int32, (LANES, LANES), 0)
    c128 = lax.broadcasted_iota(jnp.int32, (LANES, LANES), 1)
    bdmask = ((r128 // HG_HEAD_DIM) == (c128 // HG_HEAD_DIM)).astype(F32)

    def direction(d, hp_ref, lg_ref, o_ref):
        qs = hp_ref[:, 0:256].astype(F32)
        v = hp_ref[:, 256:512]
        kk = hp_ref[:, 512 + 256 * d:768 + 256 * d].astype(F32)
        eall = jnp.dot(mst_ref[d * nmat * tt:(d + 1) * nmat * tt, :], lg_ref[...].astype(BF16),
                       preferred_element_type=F32)

        def expo(j):
            return jnp.exp(eall[j * tt:(j + 1) * tt])

        a = [None] * 4
        for lev in range(nlev):
            xx = expo(lev)
            qt, kt = (qs * xx).astype(BF16), (kk * xx).astype(BF16)
            am = amask_ref[d * nlev + lev]
            for pr in range(2):
                qp = qt[:, LANES * pr:LANES * (pr + 1)]
                kp = kt[:, LANES * pr:LANES * (pr + 1)]
                for hh in range(2):
                    sel = (lane128 // HG_HEAD_DIM) == hh
                    qm = jnp.where(sel, qp, jnp.zeros_like(qp))
                    p = lax.dot_general(qm, kp, _NT, preferred_element_type=F32) * am
                    idx = 2 * pr + hh
                    a[idx] = p if a[idx] is None else a[idx] + p

        o = jnp.dot((qs * kk).astype(BF16), g256_ref[...], preferred_element_type=F32) * v.astype(F32)
        for idx in range(4):
            vm = jnp.where((lane256 // HG_HEAD_DIM) == idx, v, jnp.zeros_like(v))
            contrib = jnp.dot(a[idx].astype(BF16), vm, preferred_element_type=F32)
            o = o + contrib
        o_ref[...] = o

        xq = expo(N_LEVELS)
        xk = expo(N_LEVELS + 1)
        qc = (qs * xq).astype(BF16)
        kc = (kk * xk).astype(BF16)
        order = range(nchunk) if d == 0 else range(nchunk - 1, -1, -1)
        for c in order:
            r0 = c * HG_CHUNK
            rows = slice(r0, r0 + HG_CHUNK)
            drow = r0 + HG_CHUNK - 1 if d == 0 else r0
            for pr in range(2):
                cols = slice(LANES * pr, LANES * (pr + 1))
                st = s_ref[d, pr]
                inter = lax.dot_general(qc[rows, cols], st.astype(BF16), _NT, preferred_element_type=F32)
                o_ref[rows, cols] += inter
                ut = lax.dot_general(v[rows, cols], kc[rows, cols], _TN, preferred_element_type=F32)
                dec = xq[drow:drow + 1, cols]
                s_ref[d, pr] = st * dec + ut * bdmask

    direction(0, hpf_ref, lgf_ref, of_ref)
    direction(1, hpb_ref, lgb_ref, ob_ref)


def _hgrn(hp, lg, b, t, tt):
    nt = t // tt
    mst, amask = _hgrn_constants(tt)
    kern = functools.partial(_hgrn_kernel, tt=tt)
    fwd = lambda w, cb: pl.BlockSpec((None, tt, w), lambda bi, i: (bi, i, cb))
    bwd = lambda w, cb: pl.BlockSpec((None, tt, w), lambda bi, i: (bi, nt - 1 - i, cb))
    return pl.pallas_call(
        kern,
        grid=(b, nt),
        in_specs=[fwd(1024, 0), bwd(1024, 0), fwd(256, 0), bwd(256, 1),
                  _const_spec(mst.shape), _const_spec(amask.shape), _const_spec((256, 256))],
        out_specs=(fwd(256, 0), bwd(256, 0)),
        out_shape=(jax.ShapeDtypeStruct((b, t, HG_WIDTH), F32), jax.ShapeDtypeStruct((b, t, HG_WIDTH), F32)),
        scratch_shapes=[pltpu.VMEM((2, 2, LANES, LANES), F32)],
        compiler_params=_params(("parallel", "arbitrary")),
        name="hgrn2",
    )(hp, hp, lg, lg, mst, amask, _block_ones(256, HG_HEAD_DIM))


def _merge_kernel(*refs, tm, seq, with_router):
    oa_refs = refs[:ATT_HEADS]
    refs = refs[ATT_HEADS:]
    (x_ref, cv_ref, cvp_ref, cvn_ref, of_ref, ob_ref, sg_ref, gate_ref, cw_ref, hgn_ref, g256_ref,
     wa_ref, wb_ref, wc_ref, wo_ref, g2_ref) = refs[:16]
    if with_router:
        wr_ref, xo_ref, h2_ref, lo_ref = refs[16:]
    else:
        xo_ref, h2_ref = refs[16:]
    i = pl.program_id(0)
    tile_start = (i * tm) % seq

    u = cv_ref[:, 256:512].astype(F32)
    row = lax.broadcasted_iota(jnp.int32, (tm, 1), 0)
    prev_row = jnp.where(tile_start == 0, 0.0, cvp_ref[7:8, 256:512].astype(F32))
    next_row = jnp.where(tile_start + tm == seq, 0.0, cvn_ref[0:1, 256:512].astype(F32))
    u_m1 = jnp.where(row == 0, prev_row, pltpu.roll(u, 1, axis=0))
    u_p1 = jnp.where(row == tm - 1, next_row, pltpu.roll(u, tm - 1, axis=0))
    conv = cw_ref[0:1, :] * u_m1 + cw_ref[1:2, :] * u + cw_ref[2:3, :] * u_p1
    yb_in = (cv_ref[:, 0:256].astype(F32) * conv).astype(BF16)

    o = of_ref[...] + ob_ref[...]
    ss = _group_sum(o * o, g256_ref[...])
    yc_in = (o * lax.rsqrt(ss * (1.0 / HG_HEAD_DIM) + EPS) * hgn_ref[...] * sg_ref[...].astype(F32)).astype(BF16)

    oa = jnp.concatenate([r[...] for r in oa_refs], axis=1)
    ya = jnp.dot(oa, wa_ref[...], preferred_element_type=F32)
    yb = jnp.dot(yb_in, wb_ref[...], preferred_element_type=F32)
    yc = jnp.dot(yc_in, wc_ref[...], preferred_element_type=F32)
    merged = (gate_ref[:, 0:1024].astype(F32) * ya + gate_ref[:, 1024:2048].astype(F32) * yb
              + gate_ref[:, 2048:3072].astype(F32) * yc)
    xn = x_ref[...] + jnp.dot(merged.astype(BF16), wo_ref[...], preferred_element_type=F32)
    xo_ref[...] = xn

    ms = jnp.mean(xn * xn, axis=-1, keepdims=True)
    h2 = xn * lax.rsqrt(ms + EPS) * g2_ref[...]
    h_hi = h2.astype(BF16)
    h2_ref[...] = h_hi
    if with_router:
        h_lo = (h2 - h_hi.astype(F32)).astype(BF16)
        t1 = lax.dot_general(wr_ref[0:16, :], h_hi, _NT, preferred_element_type=F32)
        t2 = lax.dot_general(wr_ref[16:32, :], h_lo, _NT, preferred_element_type=F32)
        lo_ref[...] = t1[0:N_EXPERTS] + t1[N_EXPERTS:2 * N_EXPERTS] + t2[0:N_EXPERTS]


def _merge(x, oa, cv, of, ob, sg, gates, conv_w, hg_norm, wa, wb, wc, wo, g2, w_router, seq, tm):
    n = x.shape[0]
    nb8 = n // 8
    step8 = tm // 8
    row = lambda w: pl.BlockSpec((tm, w), lambda i: (i, 0))
    in_specs = [
        row(LANES), row(LANES), row(LANES), row(LANES), row(D_MODEL), row(512),
        pl.BlockSpec((8, 512), lambda i: (jnp.maximum(i * step8 - 1, 0), 0)),
        pl.BlockSpec((8, 512), lambda i: (jnp.minimum((i + 1) * step8, nb8 - 1), 0)),
        row(256), row(256), row(256), row(3072),
        _const_spec((8, 256)), _const_spec((1, 256)), _const_spec((256, 256)),
        _const_spec(wa.shape), _const_spec(wb.shape), _const_spec(wc.shape), _const_spec(wo.shape),
        _const_spec((1, D_MODEL)),
    ]
    args = list(oa) + [x, cv, cv, cv, of, ob, sg, gates, conv_w, hg_norm, _block_ones(256, HG_HEAD_DIM),
            wa, wb, wc, wo, g2]
    out_specs = [row(D_MODEL), row(D_MODEL)]
    out_shape = [jax.ShapeDtypeStruct((n, D_MODEL), F32), jax.ShapeDtypeStruct((n, D_MODEL), BF16)]
    with_router = w_router is not None
    if with_router:
        in_specs.append(_const_spec(w_router.shape))
        args.append(w_router)
        out_specs.append(pl.BlockSpec((N_EXPERTS, tm), lambda i: (0, i)))
        out_shape.append(jax.ShapeDtypeStruct((N_EXPERTS, n), F32))
    kern = functools.partial(_merge_kernel, tm=tm, seq=seq, with_router=with_router)
    return pl.pallas_call(
        kern,
        grid=(n // tm,),
        in_specs=in_specs,
        out_specs=tuple(out_specs),
        out_shape=tuple(out_shape),
        compiler_params=_params(("parallel",)),
        name="merge_router" if with_router else "merge",
    )(*args)


def _ffn_kernel(x_ref, h_ref, wg_ref, wu_ref, wd_ref, o_ref):
    h = h_ref[...]
    g = jnp.dot(h, wg_ref[...], preferred_element_type=F32)
    u = jnp.dot(h, wu_ref[...], preferred_element_type=F32)
    act = (jax.nn.silu(g) * u).astype(BF16)
    o_ref[...] = x_ref[...] + jnp.dot(act, wd_ref[...], preferred_element_type=F32)


def _ffn(x, h2, wg, wu, wd, tm):
    n = x.shape[0]
    row = lambda: pl.BlockSpec((tm, D_MODEL), lambda i: (i, 0))
    return pl.pallas_call(
        _ffn_kernel,
        grid=(n // tm,),
        in_specs=[row(), row(), _const_spec(wg.shape), _const_spec(wu.shape), _const_spec(wd.shape)],
        out_specs=row(),
        out_shape=jax.ShapeDtypeStruct((n, D_MODEL), F32),
        compiler_params=_params(("parallel",)),
        name="ffn_dense",
    )(x, h2, wg, wu, wd)


MOE_ROWS = 144


def _moe_expert_kernel(lo_ref, h_ref, y_ref, tri_ref, wg_ref, wu_ref, wd_ref, o_ref, *, expert, tm):
    lg = lo_ref[...]
    sub = lax.broadcasted_iota(jnp.int32, lg.shape, 0)
    m1 = jnp.max(lg, axis=0, keepdims=True)
    i1 = jnp.min(jnp.where(lg == m1, sub, N_EXPERTS), axis=0, keepdims=True)
    lg2 = jnp.where(sub == i1, NEG_BIG, lg)
    m2 = jnp.max(lg2, axis=0, keepdims=True)
    i2 = jnp.min(jnp.where(lg2 == m2, sub, N_EXPERTS), axis=0, keepdims=True)
    e2 = jnp.exp(m2 - m1)
    w_row = jnp.where(i1 == expert, 1.0 / (1.0 + e2), 0.0) + jnp.where(i2 == expert, e2 / (1.0 + e2), 0.0)
    routed = (i1 == expert) | (i2 == expert)
    routed_f = routed.astype(F32)
    nblk = tm // LANES
    by_block = jnp.concatenate([routed_f[:, LANES * j:LANES * (j + 1)] for j in range(nblk)], axis=0)
    within = jnp.dot(by_block.astype(BF16), tri_ref[...], preferred_element_type=F32)
    totals = jnp.broadcast_to(within[:, LANES - 1:LANES], (nblk, LANES))
    blk_id = lax.broadcasted_iota(jnp.int32, (nblk, LANES), 0)
    before = jnp.zeros((nblk, LANES), F32)
    for j in range(nblk - 1):
        before = before + jnp.where(blk_id > j, totals[j:j + 1, :], 0.0)
    rank2d = (within + before).astype(jnp.int32) - 1
    rank = jnp.concatenate([rank2d[j:j + 1, :] for j in range(nblk)], axis=1)
    count = jnp.sum(routed_f).astype(jnp.int32)

    h = h_ref[...]

    def expert_rows(blk):
        rows = blk * MOE_ROWS + lax.broadcasted_iota(jnp.int32, (MOE_ROWS, tm), 0)
        sel = (rank == rows) & routed
        sel_b = sel.astype(F32).astype(BF16)
        xe = jnp.dot(sel_b, h, preferred_element_type=F32).astype(BF16)
        g = jnp.dot(xe, wg_ref[...], preferred_element_type=F32)
        u = jnp.dot(xe, wu_ref[...], preferred_element_type=F32)
        act = (jax.nn.silu(g) * u).astype(BF16)
        ye = jnp.dot(act, wd_ref[...], preferred_element_type=F32)
        w_sel = jnp.sum(jnp.where(sel, w_row, 0.0), axis=1, keepdims=True)
        ye = (ye * w_sel).astype(BF16)
        return lax.dot_general(sel_b, ye, _TN, preferred_element_type=F32)

    o_ref[...] = y_ref[...] + expert_rows(0)

    def body(blk, carry):
        o_ref[...] += expert_rows(blk)
        return carry

    lax.fori_loop(1, (count + MOE_ROWS - 1) // MOE_ROWS, body, 0)


def _moe(x, h2, logits_t, wg, wu, wd, tm):
    n = x.shape[0]
    row = lambda: pl.BlockSpec((tm, D_MODEL), lambda i: (i, 0))
    t = np.arange(LANES)
    tri = jnp.asarray((t[:, None] <= t[None, :]).astype(np.float32), dtype=BF16)
    expert_w = lambda w, e: pl.BlockSpec((None,) + w.shape[1:], lambda i: (e, 0, 0), pipeline_mode=pl.Buffered(1))
    y = x
    for e in range(N_EXPERTS):
        y = pl.pallas_call(
            functools.partial(_moe_expert_kernel, expert=e, tm=tm),
            grid=(n // tm,),
            in_specs=[pl.BlockSpec((N_EXPERTS, tm), lambda i: (0, i)), row(), row(),
                      _const_spec((LANES, LANES)),
                      expert_w(wg, e), expert_w(wu, e), expert_w(wd, e)],
            out_specs=row(),
            out_shape=jax.ShapeDtypeStruct((n, D_MODEL), F32),
            input_output_aliases={2: 0},
            compiler_params=_params(("parallel",)),
            name="moe_expert",
        )(logits_t, h2, y, tri, wg, wu, wd)
    return y


def _pick(limit, total):
    return min(limit, total)


def kernel(x_prompt, x_sample, norm_mix, w_in, q_norm, k_norm, lambda_q1, lambda_k1, lambda_q2, lambda_k2, sub_norm, conv_w, hg_lower, hg_norm, w_up_a, w_up_b, w_up_c, w_out, norm_ffn, w_gate_dense, w_up_dense, w_down_dense, w_router, w_gate_moe, w_up_moe, w_down_moe):
    depth = norm_mix.shape[0]
    lb_all = jnp.cumsum(jax.nn.softmax(hg_lower.astype(F32), axis=0), axis=0)
    lb_all = lb_all - lb_all[0]
    slopes = jnp.exp2(-8.0 * (jnp.arange(ATT_HEADS, dtype=F32) + 1.0) / ATT_HEADS)

    layers = []
    for l in range(depth):
        lam_init = 0.8 - 0.6 * math.exp(-0.3 * l)
        lam = (jnp.exp(jnp.sum(lambda_q1[l].astype(F32) * lambda_k1[l].astype(F32)))
               - jnp.exp(jnp.sum(lambda_q2[l].astype(F32) * lambda_k2[l].astype(F32))) + lam_init)
        wl = w_in[l]
        qtab, ktab, fast_ok = _alibi_tables(q_norm[l].astype(F32), k_norm[l].astype(F32))
        p = dict(
            gain=norm_mix[l].reshape(1, D_MODEL),
            w_main=wl[:, 1536:].astype(BF16),
            w_vt=wl[:, 1024:1536].T.astype(BF16),
            w_qk=wl[:, 0:2 * ATT_QK_WIDTH].astype(BF16),
            qtab=qtab, ktab=ktab, fast_ok=fast_ok,
            lb=lb_all[l],
            scal=jnp.concatenate([slopes * LOG2E, lam.reshape(1), jnp.zeros((3,), F32)]),
            subg=(sub_norm[l] * (1.0 - lam_init)).reshape(ATT_V_DIM, 1),
            conv_w=jnp.concatenate([conv_w[l], jnp.zeros((5, SC_WIDTH), F32)], axis=0),
            hg_norm=hg_norm[l].reshape(1, HG_WIDTH),
            wa=w_up_a[l].astype(BF16), wb=w_up_b[l].astype(BF16), wc=w_up_c[l].astype(BF16),
            wo=w_out[l].astype(BF16),
            g2=norm_ffn[l].reshape(1, D_MODEL),
        )
        j = l // 2
        if l % 2 == 0:
            p.update(wg=w_gate_dense[j].astype(BF16), wu=w_up_dense[j].astype(BF16),
                     wd=w_down_dense[j].astype(BF16))
        else:
            wr_hi, wr_lo = _split_bf16(w_router[j].T.astype(F32))
            p.update(w_router=jnp.concatenate([wr_hi, wr_lo, wr_hi, jnp.zeros_like(wr_hi)], axis=0),
                     wg=w_gate_moe[j].astype(BF16), wu=w_up_moe[j].astype(BF16), wd=w_down_moe[j].astype(BF16))
        layers.append(p)

    def trunk(x3):
        b, t, _ = x3.shape
        n = b * t
        tm = _pick(512, t)
        tq = _pick(1024, t)
        tt = _pick(256, t)
        x = x3.reshape(n, D_MODEL)
        for l, p in enumerate(layers):
            q, k, vt, cv, hp, lg, sg, gates = _inproj(x, p["gain"], p["w_main"], p["w_vt"], p["w_qk"],
                                                      p["qtab"], p["ktab"], p["lb"], t, tm)
            oa = _attention(q.reshape(b, t, 1024), k.reshape(b, t, 1024), vt, p["scal"], p["subg"], p["fast_ok"],
                            b, t, tq, tq)
            of, ob = _hgrn(hp.reshape(b, t, 1024), lg.reshape(b, t, 512), b, t, tt)
            res = _merge(x, [o.reshape(n, LANES) for o in oa], cv, of.reshape(n, 256), ob.reshape(n, 256), sg, gates,
                         p["conv_w"], p["hg_norm"], p["wa"], p["wb"], p["wc"], p["wo"], p["g2"],
                         p.get("w_router"), t, _pick(1024, t))
            if l % 2 == 0:
                xn, h2 = res
                x = _ffn(xn, h2, p["wg"], p["wu"], p["wd"], tm)
            else:
                xn, h2, logits = res
                x = _moe(xn, h2, logits, p["wg"], p["wu"], p["wd"], _pick(1024, t))
        return x.reshape(b, t, D_MODEL)

    return (trunk(x_prompt), trunk(x_sample))
```

```python
import functools
import math

import numpy as np
import jax
import jax.numpy as jnp
from jax import lax
from jax.experimental import pallas as pl
from jax.experimental.pallas import tpu as pltpu

F32 = jnp.float32
BF16 = jnp.bfloat16

D_MODEL = 1024
ATT_HEADS = 4
ATT_HEAD_DIM = 64
ATT_V_DIM = 128
ATT_QK_WIDTH = 512
ATT_WIDTH = 512
SC_WIDTH = 256
HG_WIDTH = 256
HG_HEAD_DIM = 64
HG_CHUNK = 64
N_BRANCH = 3
N_EXPERTS = 8
EPS = 1e-6

LOG2E = math.log2(math.e)
N_SLOPE_PIECES = 3
SCORE_BOUND_LIMIT = 48.0
EXP2_ZERO_ARG = 151.0

LANES = 128
POS_SHIFT = 7
VMEM_LIMIT = 56 * 1024 * 1024
NEG_BIG = -1e30

_NT = (((1,), (1,)), ((), ()))
_TN = (((0,), (0,)), ((), ()))


def _const_spec(shape):
    nd = len(shape)
    return pl.BlockSpec(shape, lambda *_: (0,) * nd, pipeline_mode=pl.Buffered(1))


def _params(sem):
    return pltpu.CompilerParams(dimension_semantics=sem, vmem_limit_bytes=VMEM_LIMIT)


def _split_bf16(x):
    hi = x.astype(BF16)
    lo = (x - hi.astype(F32)).astype(BF16)
    return hi, lo


def _group_sum(x, gmat):
    hi, lo = _split_bf16(x)
    return (jnp.dot(hi, gmat, preferred_element_type=F32) + jnp.dot(lo, gmat, preferred_element_type=F32))


def _block_ones(width, group):
    idx = np.arange(width) // group
    return jnp.asarray((idx[:, None] == idx[None, :]).astype(np.float32), dtype=BF16)


def _inproj_kernel(x_ref, g_ref, wm_ref, wvt_ref, wqk_ref, qtab_ref, ktab_ref, lb_ref,
                   q_ref, k_ref, vt_ref, cv_ref, hp_ref, lg_ref, sg_ref, gate_ref, *, tm, seq):
    x = x_ref[...]
    ms = jnp.mean(x * x, axis=-1, keepdims=True)
    h = (x * lax.rsqrt(ms + EPS) * g_ref[...]).astype(BF16)

    vt_ref[...] = lax.dot_general(wvt_ref[...], h, _NT, preferred_element_type=F32).astype(BF16)

    pos = (pl.program_id(0) * tm) % seq + lax.broadcasted_iota(jnp.int32, (tm, 1), 0)
    pos_lo = (pos & (LANES - 1)).astype(F32)
    pos_hi = (pos >> POS_SHIFT).astype(F32)

    first_half = lax.broadcasted_iota(jnp.int32, (1, LANES), 1) < ATT_HEAD_DIM

    def head_norm(raw, tab_ref, out_ref):
        for hd in range(ATT_HEADS):
            slab = raw[:, LANES * hd:LANES * (hd + 1)]
            sq = slab * slab
            ss_all = jnp.sum(sq, axis=-1, keepdims=True)
            ss_0 = jnp.sum(jnp.where(first_half, sq, 0.0), axis=-1, keepdims=True)
            ss = jnp.where(first_half, ss_0, ss_all - ss_0)
            normed = slab * lax.rsqrt(ss * (1.0 / ATT_HEAD_DIM) + EPS)
            for c in range(2):
                cols = slice(LANES * (2 * hd + c), LANES * (2 * hd + c + 1))
                data = normed if c == 0 else pltpu.roll(normed, ATT_HEAD_DIM, axis=1)
                aux = tab_ref[1:2, cols] + tab_ref[2:3, cols] * pos_lo + tab_ref[3:4, cols] * pos_hi
                out_ref[:, cols] = jnp.where(first_half, data * tab_ref[0:1, cols], aux).astype(BF16)

    qk = jnp.dot(h, wqk_ref[...], preferred_element_type=F32)
    head_norm(qk[:, 0:ATT_QK_WIDTH], qtab_ref, q_ref)
    head_norm(qk[:, ATT_QK_WIDTH:2 * ATT_QK_WIDTH], ktab_ref, k_ref)

    sc = jnp.dot(h, wm_ref[:, 0:768], preferred_element_type=F32)
    cv_ref[:, 0:256] = sc[:, 0:256].astype(BF16)
    cv_ref[:, 256:512] = (sc[:, 256:512] * sc[:, 512:768]).astype(BF16)

    hg = jnp.dot(h, wm_ref[:, 768:2048], preferred_element_type=F32)
    hp_ref[:, 0:256] = jax.nn.silu(hg[:, 0:256]).astype(BF16)
    hp_ref[:, 256:512] = hg[:, 256:512].astype(BF16)
    for d in range(2):
        z = hg[:, 512 + 256 * d:768 + 256 * d]
        lb = lb_ref[d:d + 1, :]
        f = lb + (1.0 - lb) * jax.nn.sigmoid(z)
        hp_ref[:, 512 + 256 * d:768 + 256 * d] = ((1.0 - lb) * jax.nn.sigmoid(-z)).astype(BF16)
        lg_ref[:, 256 * d:256 * d + 256] = jnp.log(f)
    sg_ref[...] = jax.nn.silu(hg[:, 1024:1280]).astype(BF16)

    gate_ref[...] = jax.nn.sigmoid(
        jnp.dot(h, wm_ref[:, 2048:5120], preferred_element_type=F32)).astype(BF16)


def _inproj(x, gain, w_main, w_vt, w_qk, qtab, ktab, lb, seq, tm):
    n = x.shape[0]
    row = lambda w: pl.BlockSpec((tm, w), lambda i: (i, 0))
    out_shape = (
        jax.ShapeDtypeStruct((n, 1024), BF16),
        jax.ShapeDtypeStruct((n, 1024), BF16),
        jax.ShapeDtypeStruct((512, n), BF16),
        jax.ShapeDtypeStruct((n, 512), BF16),
        jax.ShapeDtypeStruct((n, 1024), BF16),
        jax.ShapeDtypeStruct((n, 512), F32),
        jax.ShapeDtypeStruct((n, 256), BF16),
        jax.ShapeDtypeStruct((n, 3072), BF16),
    )
    kern = functools.partial(_inproj_kernel, tm=tm, seq=seq)
    return pl.pallas_call(
        kern,
        grid=(n // tm,),
        in_specs=[row(D_MODEL), _const_spec((1, D_MODEL)), _const_spec(w_main.shape), _const_spec(w_vt.shape),
                  _const_spec(w_qk.shape), _const_spec((8, 1024)), _const_spec((8, 1024)),
                  _const_spec((2, 256))],
        out_specs=(row(1024), row(1024), pl.BlockSpec((512, tm), lambda i: (0, i)), row(512), row(1024),
                   row(512), row(256), row(3072)),
        out_shape=out_shape,
        compiler_params=_params(("parallel",)),
        name="inproj",
    )(x, gain, w_main, w_vt, w_qk, qtab, ktab, lb)


def _alibi_tables(q_gain, k_gain):
    slopes = np.exp2(-8.0 * (np.arange(ATT_HEADS, dtype=np.float64) + 1.0) / ATT_HEADS) * LOG2E
    qc = np.zeros((8, 1024), np.float32)
    kc = np.zeros((8, 1024), np.float32)
    for h in range(ATT_HEADS):
        rest = np.float32(slopes[h])
        pieces = []
        for _ in range(N_SLOPE_PIECES):
            piece = np.asarray(rest, dtype=BF16).astype(np.float32)
            pieces.append(float(piece))
            rest = np.float32(rest - piece)
        for c in range(2):
            base = LANES * (2 * h + c) + ATT_HEAD_DIM
            for i, a in enumerate(pieces):
                qc[1, base + 2 * i] = -a
                qc[1, base + 2 * i + 1] = -a * LANES
                kc[2, base + 2 * i] = 1.0
                kc[3, base + 2 * i + 1] = 1.0
                off = base + 2 * N_SLOPE_PIECES
                qc[2, off + 2 * i] = 1.0
                qc[3, off + 2 * i + 1] = 1.0
                kc[1, off + 2 * i] = a
                kc[1, off + 2 * i + 1] = a * LANES
    lane = np.arange(1024) % LANES
    lane_is_data = lane < ATT_HEAD_DIM
    qg = jnp.where(lane_is_data, jnp.tile(q_gain, 16) * (LOG2E * ATT_HEAD_DIM ** -0.5), 0.0)
    kg = jnp.where(lane_is_data, jnp.tile(k_gain, 16), 0.0)
    bound = 1.02 * LOG2E * ATT_HEAD_DIM ** 0.5 * jnp.max(jnp.abs(q_gain)) * jnp.max(jnp.abs(k_gain))
    qconst = jnp.asarray(qc[1])
    rest = bound.astype(F32)
    bound_lane0 = ATT_HEAD_DIM + 4 * N_SLOPE_PIECES
    for i in range(N_SLOPE_PIECES):
        piece = rest.astype(BF16).astype(F32)
        rest = rest - piece
        qconst = jnp.where(lane == bound_lane0 + i, -piece, qconst)
        kc[1, lane == bound_lane0 + i] = 1.0
    qtab = jnp.asarray(qc).at[0].set(qg).at[1].set(qconst)
    ktab = jnp.asarray(kc).at[0].set(kg)
    return qtab, ktab, bound <= SCORE_BOUND_LIMIT


def _attn_finish(sc_ref, subg_ref, o_ref, l0, l1, acc_ref):
    lam = sc_ref[ATT_HEADS]
    o = acc_ref[0] / l0 - lam * (acc_ref[1] / l1)
    ms = jnp.mean(o * o, axis=0, keepdims=True)
    o = o * lax.rsqrt(ms + EPS) * subg_ref[...]
    o_ref[...] = o.T.astype(BF16)


def _attn_key_tile(qi, st, nk, reach, band):
    if band:
        off = jnp.where(st <= reach, st, reach - st)
        kb = qi + off
        return jnp.clip(kb, 0, nk - 1), off > 0, (kb >= 0) & (kb < nk)
    return (qi + st) % nk, qi + st < nk, st >= 0


def _attn_fast_kernel(sc_ref, q_ref, k_ref, vt_ref, absd_ref, subg_ref, o_ref, qv_ref, l_ref, acc_ref,
                      *, tq, tk, nk, head, reach, band, nsteps):
    h = head
    qi = pl.program_id(1)
    st = pl.program_id(2)
    _, after, exists = _attn_key_tile(qi, st, nk, reach, band)

    @pl.when(st == 0)
    def _():
        lane = lax.broadcasted_iota(jnp.int32, (tq, LANES), 1)
        alibi = (lane >= ATT_HEAD_DIM) & (lane < ATT_HEAD_DIM + 4 * N_SLOPE_PIECES)
        for c in range(2):
            qc = q_ref[:, LANES * c:LANES * (c + 1)]
            qv_ref[c] = qc
            qv_ref[2 + c] = jnp.where(alibi, -qc, qc)
            qv_ref[4 + c] = jnp.where(alibi, jnp.zeros_like(qc), qc)
        l_ref[...] = jnp.zeros(l_ref.shape, F32)
        acc_ref[...] = jnp.zeros(acc_ref.shape, F32)

    def step(diag):
        vt = vt_ref[...]
        for c in range(2):
            kc = k_ref[:, LANES * c:LANES * (c + 1)]
            if diag:
                qsel = qv_ref[4 + c]
            else:
                qsel = qv_ref[jnp.where(after, 0, 2) + c]
            s = lax.dot_general(kc, qsel, _NT, preferred_element_type=F32)
            if diag:
                s = s - sc_ref[h] * absd_ref[...]
            p = jnp.exp2(s)
            l_ref[c] += jnp.sum(p.reshape(tk // 8, 8, tq), axis=0)
            acc_ref[c] += jnp.dot(vt, p.astype(BF16), preferred_element_type=F32)

    @pl.when(st == 0)
    def _():
        step(True)

    @pl.when((st != 0) & exists)
    def _():
        step(False)

    @pl.when(st == nsteps - 1)
    def _():
        _attn_finish(sc_ref, subg_ref, o_ref, jnp.sum(l_ref[0], axis=0, keepdims=True),
                     jnp.sum(l_ref[1], axis=0, keepdims=True), acc_ref)


def _attn_safe_kernel(sc_ref, q_ref, k_ref, vt_ref, subg_ref, o_ref, qm_ref, m_ref, l_ref, acc_ref, *, tq, tk, nk):
    h = pl.program_id(1)
    qi = pl.program_id(2)
    ki = pl.program_id(3)

    @pl.when(ki == 0)
    def _():
        lane = lax.broadcasted_iota(jnp.int32, (tq, LANES), 1)
        for c in range(2):
            qc = q_ref[:, LANES * c:LANES * (c + 1)]
            qm_ref[c] = jnp.where(lane < ATT_HEAD_DIM, qc, jnp.zeros_like(qc))
        m_ref[...] = jnp.full(m_ref.shape, NEG_BIG, F32)
        l_ref[...] = jnp.zeros(l_ref.shape, F32)
        acc_ref[...] = jnp.zeros(acc_ref.shape, F32)

    kpos = ki * tk + lax.broadcasted_iota(jnp.int32, (tk, tq), 0)
    qpos = qi * tq + lax.broadcasted_iota(jnp.int32, (tk, tq), 1)
    bias = sc_ref[h] * jnp.abs(kpos - qpos).astype(F32)
    vt = vt_ref[...]
    for c in range(2):
        kc = k_ref[:, LANES * c:LANES * (c + 1)]
        s = lax.dot_general(kc, qm_ref[c], _NT, preferred_element_type=F32) - bias
        m_prev = m_ref[c]
        m_new = jnp.maximum(m_prev, jnp.max(s, axis=0, keepdims=True))
        alpha = jnp.exp2(m_prev - m_new)
        p = jnp.exp2(s - m_new)
        l_ref[c] = alpha * l_ref[c] + jnp.sum(p, axis=0, keepdims=True)
        acc_ref[c] = alpha * acc_ref[c] + jnp.dot(vt, p.astype(BF16), preferred_element_type=F32)
        m_ref[c] = m_new

    @pl.when(ki == nk - 1)
    def _():
        _attn_finish(sc_ref, subg_ref, o_ref, l_ref[0], l_ref[1], acc_ref)


def _attention(q, k, vt, scal, subg, fast_ok, b, t, tq, tk):
    nq, nk = t // tq, t // tk
    blk = 2 * LANES
    common = dict(
        grid=(b, ATT_HEADS, nq, nk),
        out_specs=pl.BlockSpec((None, tq, LANES), lambda bi, h, qi, ki: (bi, qi, h)),
        out_shape=jax.ShapeDtypeStruct((b, t, ATT_WIDTH), BF16),
        compiler_params=_params(("parallel", "parallel", "parallel", "arbitrary")),
    )
    qkv_specs = [
        pl.BlockSpec((None, tq, blk), lambda bi, h, qi, ki: (bi, qi, h)),
        pl.BlockSpec((None, tk, blk), lambda bi, h, qi, ki: (bi, ki, h)),
        pl.BlockSpec((LANES, tk), lambda bi, h, qi, ki: (h, bi * nk + ki)),
    ]
    smem = pl.BlockSpec(memory_space=pltpu.SMEM)
    idx = np.arange(tk)[:, None] - np.arange(tq)[None, :]
    absd = jnp.asarray(np.abs(idx), dtype=F32)
    slopes2 = np.exp2(-8.0 * (np.arange(ATT_HEADS) + 1.0) / ATT_HEADS) * LOG2E
    window = tuple(int((EXP2_ZERO_ARG / s - 1.0) // tk) + 1 for s in slopes2)
    def fast_head(h, q, k, vt, scal, subg):
        reach = window[h]
        band = 2 * reach + 1 < nk
        nsteps = 2 * reach + 1 if band else nk
        tile = lambda qi, st: _attn_key_tile(qi, st, nk, reach, band)[0]
        return pl.pallas_call(
            functools.partial(_attn_fast_kernel, tq=tq, tk=tk, nk=nk, head=h, reach=reach, band=band,
                              nsteps=nsteps),
            grid=(b, nq, nsteps),
            in_specs=[
                smem,
                pl.BlockSpec((None, tq, blk), lambda bi, qi, st: (bi, qi, h)),
                pl.BlockSpec((None, tk, blk), lambda bi, qi, st: (bi, tile(qi, st), h)),
                pl.BlockSpec((LANES, tk), lambda bi, qi, st: (h, bi * nk + tile(qi, st))),
                _const_spec((tk, tq)), _const_spec((ATT_V_DIM, 1)),
            ],
            out_specs=pl.BlockSpec((None, tq, LANES), lambda bi, qi, st: (bi, qi, 0)),
            out_shape=jax.ShapeDtypeStruct((b, t, LANES), BF16),
            scratch_shapes=[
                pltpu.VMEM((6, tq, LANES), BF16),
                pltpu.VMEM((2, 8, tq), F32),
                pltpu.VMEM((2, ATT_V_DIM, tq), F32),
            ],
            compiler_params=_params(("parallel", "parallel", "arbitrary")),
            name="diff_attn_h%d" % h,
        )(scal, q, k, vt, absd, subg)

    def fast(q, k, vt, scal, subg):
        return tuple(fast_head(h, q, k, vt, scal, subg) for h in range(ATT_HEADS))

    def safe(q, k, vt, scal, subg):
        o = safe_call(q, k, vt, scal, subg)
        return tuple(o[:, :, LANES * h:LANES * (h + 1)] for h in range(ATT_HEADS))

    def safe_call(q, k, vt, scal, subg):
        return pl.pallas_call(
            functools.partial(_attn_safe_kernel, tq=tq, tk=tk, nk=nk),
            in_specs=[smem] + qkv_specs + [_const_spec((ATT_V_DIM, 1))],
            scratch_shapes=[
                pltpu.VMEM((2, tq, LANES), BF16),
                pltpu.VMEM((2, 1, tq), F32),
                pltpu.VMEM((2, 1, tq), F32),
                pltpu.VMEM((2, ATT_V_DIM, tq), F32),
            ],
            name="diff_attn_safe",
            **common,
        )(scal, q, k, vt, subg)

    return lax.cond(fast_ok, fast, safe, q, k, vt, scal, subg)


N_LEVELS = 6


def _hgrn_constants(tt):
    c = HG_CHUNK
    t = np.arange(tt)
    same_chunk = (t[:, None] // c) == (t[None, :] // c)
    mats, masks = [], []
    for j in range(N_LEVELS):
        m = 1 << j
        blk = t // m
        same_blk = blk[:, None] == blk[None, :]
        odd = (blk % 2) == 1
        incl = same_blk & (t[None, :] <= t[:, None])
        excl_rev = same_blk & (t[None, :] > t[:, None])
        mats.append(np.where(odd[:, None], incl, excl_rev))
        same_2m = (t[:, None] // (2 * m)) == (t[None, :] // (2 * m))
        masks.append((same_2m & odd[:, None] & (~odd)[None, :]).astype(np.float32))
    mats.append(same_chunk & (t[None, :] <= t[:, None]))
    mats.append(same_chunk & (t[None, :] > t[:, None]))
    mats = np.stack(mats).astype(np.float32)
    masks = np.stack(masks)
    mats_b = mats[:, ::-1, ::-1]
    masks_b = masks[:, ::-1, ::-1]
    mst = np.concatenate([mats.reshape(-1, tt), mats_b.reshape(-1, tt)], axis=0)
    amask = np.concatenate([masks, masks_b], axis=0)
    return jnp.asarray(mst, dtype=BF16), jnp.asarray(amask, dtype=F32)


def _hgrn_kernel(hpf_ref, hpb_ref, lgf_ref, lgb_ref, mst_ref, amask_ref, g256_ref, of_ref, ob_ref, s_ref, *, tt):
    i = pl.program_id(1)
    nlev = N_LEVELS
    nmat = N_LEVELS + 2
    nchunk = tt // HG_CHUNK

    @pl.when(i == 0)
    def _():
        s_ref[...] = jnp.zeros(s_ref.shape, F32)

    lane256 = lax.broadcasted_iota(jnp.int32, (1, 256), 1)
    lane128 = lax.broadcasted_iota(jnp.int32, (1, LANES), 1)
    r128 = lax.broadcasted_iota(jnp.int32, (LANES, LANES), 0)
    c128 = lax.broadcasted_iota(jnp.int32, (LANES, LANES), 1)
    bdmask = ((r128 // HG_HEAD_DIM) == (c128 // HG_HEAD_DIM)).astype(F32)

    def direction(d, hp_ref, lg_ref, o_ref):
        qs = hp_ref[:, 0:256].astype(F32)
        v = hp_ref[:, 256:512]
        kk = hp_ref[:, 512 + 256 * d:768 + 256 * d].astype(F32)
        eall = jnp.dot(mst_ref[d * nmat * tt:(d + 1) * nmat * tt, :], lg_ref[...].astype(BF16),
                       preferred_element_type=F32)

        def expo(j):
            return jnp.exp(eall[j * tt:(j + 1) * tt])

        a = [None] * 4
        for lev in range(nlev):
            xx = expo(lev)
            qt, kt = (qs * xx).astype(BF16), (kk * xx).astype(BF16)
            am = amask_ref[d * nlev + lev]
            for pr in range(2):
                qp = qt[:, LANES * pr:LANES * (pr + 1)]
                kp = kt[:, LANES * pr:LANES * (pr + 1)]
                for hh in range(2):
                    sel = (lane128 // HG_HEAD_DIM) == hh
                    qm = jnp.where(sel, qp, jnp.zeros_like(qp))
                    p = lax.dot_general(qm, kp, _NT, preferred_element_type=F32) * am
                    idx = 2 * pr + hh
                    a[idx] = p if a[idx] is None else a[idx] + p

        o = jnp.dot((qs * kk).astype(BF16), g256_ref[...], preferred_element_type=F32) * v.astype(F32)
        for idx in range(4):
            vm = jnp.where((lane256 // HG_HEAD_DIM) == idx, v, jnp.zeros_like(v))
            contrib = jnp.dot(a[idx].astype(BF16), vm, preferred_element_type=F32)
            o = o + contrib
        o_ref[...] = o

        xq = expo(N_LEVELS)
        xk = expo(N_LEVELS + 1)
        qc = (qs * xq).astype(BF16)
        kc = (kk * xk).astype(BF16)
        order = range(nchunk) if d == 0 else range(nchunk - 1, -1, -1)
        for c in order:
            r0 = c * HG_CHUNK
            rows = slice(r0, r0 + HG_CHUNK)
            drow = r0 + HG_CHUNK - 1 if d == 0 else r0
            for pr in range(2):
                cols = slice(LANES * pr, LANES * (pr + 1))
                st = s_ref[d, pr]
                inter = lax.dot_general(qc[rows, cols], st.astype(BF16), _NT, preferred_element_type=F32)
                o_ref[rows, cols] += inter
                ut = lax.dot_general(v[rows, cols], kc[rows, cols], _TN, preferred_element_type=F32)
                dec = xq[drow:drow + 1, cols]
                s_ref[d, pr] = st * dec + ut * bdmask

    direction(0, hpf_ref, lgf_ref, of_ref)
    direction(1, hpb_ref, lgb_ref, ob_ref)


def _hgrn(hp, lg, b, t, tt):
    nt = t // tt
    mst, amask = _hgrn_constants(tt)
    kern = functools.partial(_hgrn_kernel, tt=tt)
    fwd = lambda w, cb: pl.BlockSpec((None, tt, w), lambda bi, i: (bi, i, cb))
    bwd = lambda w, cb: pl.BlockSpec((None, tt, w), lambda bi, i: (bi, nt - 1 - i, cb))
    return pl.pallas_call(
        kern,
        grid=(b, nt),
        in_specs=[fwd(1024, 0), bwd(1024, 0), fwd(256, 0), bwd(256, 1),
                  _const_spec(mst.shape), _const_spec(amask.shape), _const_spec((256, 256))],
        out_specs=(fwd(256, 0), bwd(256, 0)),
        out_shape=(jax.ShapeDtypeStruct((b, t, HG_WIDTH), F32), jax.ShapeDtypeStruct((b, t, HG_WIDTH), F32)),
        scratch_shapes=[pltpu.VMEM((2, 2, LANES, LANES), F32)],
        compiler_params=_params(("parallel", "arbitrary")),
        name="hgrn2",
    )(hp, hp, lg, lg, mst, amask, _block_ones(256, HG_HEAD_DIM))


def _merge_kernel(*refs, tm, seq, with_router):
    oa_refs = refs[:ATT_HEADS]
    refs = refs[ATT_HEADS:]
    (x_ref, cv_ref, cvp_ref, cvn_ref, of_ref, ob_ref, sg_ref, gate_ref, cw_ref, hgn_ref, g256_ref,
     wa_ref, wb_ref, wc_ref, wo_ref, g2_ref) = refs[:16]
    if with_router:
        wr_ref, xo_ref, h2_ref, lo_ref = refs[16:]
    else:
        xo_ref, h2_ref = refs[16:]
    i = pl.program_id(0)
    tile_start = (i * tm) % seq

    u = cv_ref[:, 256:512].astype(F32)
    row = lax.broadcasted_iota(jnp.int32, (tm, 1), 0)
    prev_row = jnp.where(tile_start == 0, 0.0, cvp_ref[7:8, 256:512].astype(F32))
    next_row = jnp.where(tile_start + tm == seq, 0.0, cvn_ref[0:1, 256:512].astype(F32))
    u_m1 = jnp.where(row == 0, prev_row, pltpu.roll(u, 1, axis=0))
    u_p1 = jnp.where(row == tm - 1, next_row, pltpu.roll(u, tm - 1, axis=0))
    conv = cw_ref[0:1, :] * u_m1 + cw_ref[1:2, :] * u + cw_ref[2:3, :] * u_p1
    yb_in = (cv_ref[:, 0:256].astype(F32) * conv).astype(BF16)

    o = of_ref[...] + ob_ref[...]
    ss = _group_sum(o * o, g256_ref[...])
    yc_in = (o * lax.rsqrt(ss * (1.0 / HG_HEAD_DIM) + EPS) * hgn_ref[...] * sg_ref[...].astype(F32)).astype(BF16)

    oa = jnp.concatenate([r[...] for r in oa_refs], axis=1)
    ya = jnp.dot(oa, wa_ref[...], preferred_element_type=F32)
    yb = jnp.dot(yb_in, wb_ref[...], preferred_element_type=F32)
    yc = jnp.dot(yc_in, wc_ref[...], preferred_element_type=F32)
    merged = (gate_ref[:, 0:1024].astype(F32) * ya + gate_ref[:, 1024:2048].astype(F32) * yb
              + gate_ref[:, 2048:3072].astype(F32) * yc)
    xn = x_ref[...] + jnp.dot(merged.astype(BF16), wo_ref[...], preferred_element_type=F32)
    xo_ref[...] = xn

    ms = jnp.mean(xn * xn, axis=-1, keepdims=True)
    h2 = xn * lax.rsqrt(ms + EPS) * g2_ref[...]
    h_hi = h2.astype(BF16)
    h2_ref[...] = h_hi
    if with_router:
        h_lo = (h2 - h_hi.astype(F32)).astype(BF16)
        t1 = lax.dot_general(wr_ref[0:16, :], h_hi, _NT, preferred_element_type=F32)
        t2 = lax.dot_general(wr_ref[16:32, :], h_lo, _NT, preferred_element_type=F32)
        lo_ref[...] = t1[0:N_EXPERTS] + t1[N_EXPERTS:2 * N_EXPERTS] + t2[0:N_EXPERTS]


def _merge(x, oa, cv, of, ob, sg, gates, conv_w, hg_norm, wa, wb, wc, wo, g2, w_router, seq, tm):
    n = x.shape[0]
    nb8 = n // 8
    step8 = tm // 8
    row = lambda w: pl.BlockSpec((tm, w), lambda i: (i, 0))
    in_specs = [
        row(LANES), row(LANES), row(LANES), row(LANES), row(D_MODEL), row(512),
        pl.BlockSpec((8, 512), lambda i: (jnp.maximum(i * step8 - 1, 0), 0)),
        pl.BlockSpec((8, 512), lambda i: (jnp.minimum((i + 1) * step8, nb8 - 1), 0)),
        row(256), row(256), row(256), row(3072),
        _const_spec((8, 256)), _const_spec((1, 256)), _const_spec((256, 256)),
        _const_spec(wa.shape), _const_spec(wb.shape), _const_spec(wc.shape), _const_spec(wo.shape),
        _const_spec((1, D_MODEL)),
    ]
    args = list(oa) + [x, cv, cv, cv, of, ob, sg, gates, conv_w, hg_norm, _block_ones(256, HG_HEAD_DIM),
            wa, wb, wc, wo, g2]
    out_specs = [row(D_MODEL), row(D_MODEL)]
    out_shape = [jax.ShapeDtypeStruct((n, D_MODEL), F32), jax.ShapeDtypeStruct((n, D_MODEL), BF16)]
    with_router = w_router is not None
    if with_router:
        in_specs.append(_const_spec(w_router.shape))
        args.append(w_router)
        out_specs.append(pl.BlockSpec((N_EXPERTS, tm), lambda i: (0, i)))
        out_shape.append(jax.ShapeDtypeStruct((N_EXPERTS, n), F32))
    kern = functools.partial(_merge_kernel, tm=tm, seq=seq, with_router=with_router)
    return pl.pallas_call(
        kern,
        grid=(n // tm,),
        in_specs=in_specs,
        out_specs=tuple(out_specs),
        out_shape=tuple(out_shape),
        compiler_params=_params(("parallel",)),
        name="merge_router" if with_router else "merge",
    )(*args)


def _ffn_kernel(x_ref, h_ref, wg_ref, wu_ref, wd_ref, o_ref):
    h = h_ref[...]
    g = jnp.dot(h, wg_ref[...], preferred_element_type=F32)
    u = jnp.dot(h, wu_ref[...], preferred_element_type=F32)
    act = (jax.nn.silu(g) * u).astype(BF16)
    o_ref[...] = x_ref[...] + jnp.dot(act, wd_ref[...], preferred_element_type=F32)


def _ffn(x, h2, wg, wu, wd, tm):
    n = x.shape[0]
    row = lambda: pl.BlockSpec((tm, D_MODEL), lambda i: (i, 0))
    return pl.pallas_call(
        _ffn_kernel,
        grid=(n // tm,),
        in_specs=[row(), row(), _const_spec(wg.shape), _const_spec(wu.shape), _const_spec(wd.shape)],
        out_specs=row(),
        out_shape=jax.ShapeDtypeStruct((n, D_MODEL), F32),
        compiler_params=_params(("parallel",)),
        name="ffn_dense",
    )(x, h2, wg, wu, wd)


MOE_ROWS = 288
MOE_TAIL_ROWS = 288


def _moe_expert_kernel(lo_ref, h_ref, y_ref, tri_ref, wg_ref, wu_ref, wd_ref, o_ref, *, expert, tm):
    lg = lo_ref[...]
    sub = lax.broadcasted_iota(jnp.int32, lg.shape, 0)
    m1 = jnp.max(lg, axis=0, keepdims=True)
    i1 = jnp.min(jnp.where(lg == m1, sub, N_EXPERTS), axis=0, keepdims=True)
    lg2 = jnp.where(sub == i1, NEG_BIG, lg)
    m2 = jnp.max(lg2, axis=0, keepdims=True)
    i2 = jnp.min(jnp.where(lg2 == m2, sub, N_EXPERTS), axis=0, keepdims=True)
    e2 = jnp.exp(m2 - m1)
    w_row = jnp.where(i1 == expert, 1.0 / (1.0 + e2), 0.0) + jnp.where(i2 == expert, e2 / (1.0 + e2), 0.0)
    routed = (i1 == expert) | (i2 == expert)
    routed_f = routed.astype(F32)
    nblk = tm // LANES
    by_block = jnp.concatenate([routed_f[:, LANES * j:LANES * (j + 1)] for j in range(nblk)], axis=0)
    within = jnp.dot(by_block.astype(BF16), tri_ref[...], preferred_element_type=F32)
    totals = jnp.broadcast_to(within[:, LANES - 1:LANES], (nblk, LANES))
    blk_id = lax.broadcasted_iota(jnp.int32, (nblk, LANES), 0)
    before = jnp.zeros((nblk, LANES), F32)
    for j in range(nblk - 1):
        before = before + jnp.where(blk_id > j, totals[j:j + 1, :], 0.0)
    rank2d = (within + before).astype(jnp.int32) - 1
    rank = jnp.concatenate([rank2d[j:j + 1, :] for j in range(nblk)], axis=1)
    count = jnp.sum(routed_f).astype(jnp.int32)

    h = h_ref[...]

    def expert_rows(first, nrows):
        rows = first + lax.broadcasted_iota(jnp.int32, (nrows, tm), 0)
        sel = (rank == rows) & routed
        sel_b = sel.astype(F32).astype(BF16)
        xe = jnp.dot(sel_b, h, preferred_element_type=F32).astype(BF16)
        g = jnp.dot(xe, wg_ref[...], preferred_element_type=F32)
        u = jnp.dot(xe, wu_ref[...], preferred_element_type=F32)
        act = (jax.nn.silu(g) * u).astype(BF16)
        ye = jnp.dot(act, wd_ref[...], preferred_element_type=F32)
        w_sel = jnp.sum(jnp.where(sel, w_row, 0.0), axis=1, keepdims=True)
        ye = (ye * w_sel).astype(BF16)
        return lax.dot_general(sel_b, ye, _TN, preferred_element_type=F32)

    o_ref[...] = y_ref[...] + expert_rows(0, MOE_ROWS)

    def body(blk, carry):
        o_ref[...] += expert_rows(MOE_ROWS + blk * MOE_TAIL_ROWS, MOE_TAIL_ROWS)
        return carry

    lax.fori_loop(0, (count - MOE_ROWS + MOE_TAIL_ROWS - 1) // MOE_TAIL_ROWS, body, 0)


def _moe(x, h2, logits_t, wg, wu, wd, tm):
    n = x.shape[0]
    row = lambda: pl.BlockSpec((tm, D_MODEL), lambda i: (i, 0))
    t = np.arange(LANES)
    tri = jnp.asarray((t[:, None] <= t[None, :]).astype(np.float32), dtype=BF16)
    expert_w = lambda w, e: pl.BlockSpec((None,) + w.shape[1:], lambda i: (e, 0, 0), pipeline_mode=pl.Buffered(1))
    y = x
    for e in range(N_EXPERTS):
        y = pl.pallas_call(
            functools.partial(_moe_expert_kernel, expert=e, tm=tm),
            grid=(n // tm,),
            in_specs=[pl.BlockSpec((N_EXPERTS, tm), lambda i: (0, i)), row(), row(),
                      _const_spec((LANES, LANES)),
                      expert_w(wg, e), expert_w(wu, e), expert_w(wd, e)],
            out_specs=row(),
            out_shape=jax.ShapeDtypeStruct((n, D_MODEL), F32),
            input_output_aliases={2: 0},
            compiler_params=_params(("parallel",)),
            name="moe_expert",
        )(logits_t, h2, y, tri, wg, wu, wd)
    return y


def _pick(limit, total):
    return min(limit, total)


def kernel(x_prompt, x_sample, norm_mix, w_in, q_norm, k_norm, lambda_q1, lambda_k1, lambda_q2, lambda_k2, sub_norm, conv_w, hg_lower, hg_norm, w_up_a, w_up_b, w_up_c, w_out, norm_ffn, w_gate_dense, w_up_dense, w_down_dense, w_router, w_gate_moe, w_up_moe, w_down_moe):
    depth = norm_mix.shape[0]
    lb_all = jnp.cumsum(jax.nn.softmax(hg_lower.astype(F32), axis=0), axis=0)
    lb_all = lb_all - lb_all[0]
    slopes = jnp.exp2(-8.0 * (jnp.arange(ATT_HEADS, dtype=F32) + 1.0) / ATT_HEADS)

    layers = []
    for l in range(depth):
        lam_init = 0.8 - 0.6 * math.exp(-0.3 * l)
        lam = (jnp.exp(jnp.sum(lambda_q1[l].astype(F32) * lambda_k1[l].astype(F32)))
               - jnp.exp(jnp.sum(lambda_q2[l].astype(F32) * lambda_k2[l].astype(F32))) + lam_init)
        wl = w_in[l]
        qtab, ktab, fast_ok = _alibi_tables(q_norm[l].astype(F32), k_norm[l].astype(F32))
        p = dict(
            gain=norm_mix[l].reshape(1, D_MODEL),
            w_main=wl[:, 1536:].astype(BF16),
            w_vt=wl[:, 1024:1536].T.astype(BF16),
            w_qk=wl[:, 0:2 * ATT_QK_WIDTH].astype(BF16),
            qtab=qtab, ktab=ktab, fast_ok=fast_ok,
            lb=lb_all[l],
            scal=jnp.concatenate([slopes * LOG2E, lam.reshape(1), jnp.zeros((3,), F32)]),
            subg=(sub_norm[l] * (1.0 - lam_init)).reshape(ATT_V_DIM, 1),
            conv_w=jnp.concatenate([conv_w[l], jnp.zeros((5, SC_WIDTH), F32)], axis=0),
            hg_norm=hg_norm[l].reshape(1, HG_WIDTH),
            wa=w_up_a[l].astype(BF16), wb=w_up_b[l].astype(BF16), wc=w_up_c[l].astype(BF16),
            wo=w_out[l].astype(BF16),
            g2=norm_ffn[l].reshape(1, D_MODEL),
        )
        j = l // 2
        if l % 2 == 0:
            p.update(wg=w_gate_dense[j].astype(BF16), wu=w_up_dense[j].astype(BF16),
                     wd=w_down_dense[j].astype(BF16))
        else:
            wr_hi, wr_lo = _split_bf16(w_router[j].T.astype(F32))
            p.update(w_router=jnp.concatenate([wr_hi, wr_lo, wr_hi, jnp.zeros_like(wr_hi)], axis=0),
                     wg=w_gate_moe[j].astype(BF16), wu=w_up_moe[j].astype(BF16), wd=w_down_moe[j].astype(BF16))
        layers.append(p)

    def trunk(x3):
        b, t, _ = x3.shape
        n = b * t
        tm = _pick(512, t)
        tq = _pick(1024, t)
        tt = _pick(256, t)
        x = x3.reshape(n, D_MODEL)
        for l, p in enumerate(layers):
            q, k, vt, cv, hp, lg, sg, gates = _inproj(x, p["gain"], p["w_main"], p["w_vt"], p["w_qk"],
                                                      p["qtab"], p["ktab"], p["lb"], t, tm)
            oa = _attention(q.reshape(b, t, 1024), k.reshape(b, t, 1024), vt, p["scal"], p["subg"], p["fast_ok"],
                            b, t, tq, tq)
            of, ob = _hgrn(hp.reshape(b, t, 1024), lg.reshape(b, t, 512), b, t, tt)
            res = _merge(x, [o.reshape(n, LANES) for o in oa], cv, of.reshape(n, 256), ob.reshape(n, 256), sg, gates,
                         p["conv_w"], p["hg_norm"], p["wa"], p["wb"], p["wc"], p["wo"], p["g2"],
                         p.get("w_router"), t, _pick(1024, t))
            if l % 2 == 0:
                xn, h2 = res
                x = _ffn(xn, h2, p["wg"], p["wu"], p["wd"], tm)
            else:
                xn, h2, logits = res
                x = _moe(xn, h2, logits, p["wg"], p["wu"], p["wd"], _pick(1024, t))
        return x.reshape(b, t, D_MODEL)

    return (trunk(x_prompt), trunk(x_sample))
```

```python
import functools
import math

import numpy as np
import jax
import jax.numpy as jnp
from jax import lax
from jax.experimental import pallas as pl
from jax.experimental.pallas import tpu as pltpu

F32 = jnp.float32
BF16 = jnp.bfloat16

D_MODEL = 1024
ATT_HEADS = 4
ATT_HEAD_DIM = 64
ATT_V_DIM = 128
ATT_QK_WIDTH = 512
ATT_WIDTH = 512
SC_WIDTH = 256
HG_WIDTH = 256
HG_HEAD_DIM = 64
HG_CHUNK = 64
N_BRANCH = 3
N_EXPERTS = 8
EPS = 1e-6

LOG2E = math.log2(math.e)
N_SLOPE_PIECES = 3
SCORE_BOUND_LIMIT = 48.0
EXP2_ZERO_ARG = 151.0

LANES = 128
POS_SHIFT = 7
VMEM_LIMIT = 56 * 1024 * 1024
NEG_BIG = -1e30

_NT = (((1,), (1,)), ((), ()))
_TN = (((0,), (0,)), ((), ()))


def _const_spec(shape):
    nd = len(shape)
    return pl.BlockSpec(shape, lambda *_: (0,) * nd, pipeline_mode=pl.Buffered(1))


def _params(sem):
    return pltpu.CompilerParams(dimension_semantics=sem, vmem_limit_bytes=VMEM_LIMIT)


def _split_bf16(x):
    hi = x.astype(BF16)
    lo = (x - hi.astype(F32)).astype(BF16)
    return hi, lo


def _group_sum(x, gmat):
    hi, lo = _split_bf16(x)
    return (jnp.dot(hi, gmat, preferred_element_type=F32) + jnp.dot(lo, gmat, preferred_element_type=F32))


def _block_ones(width, group):
    idx = np.arange(width) // group
    return jnp.asarray((idx[:, None] == idx[None, :]).astype(np.float32), dtype=BF16)


def _inproj_kernel(x_ref, g_ref, wm_ref, wvt_ref, wqk_ref, qtab_ref, ktab_ref, lb_ref,
                   q_ref, k_ref, vt_ref, cv_ref, hp_ref, lg_ref, sg_ref, gate_ref, *, tm, seq):
    x = x_ref[...]
    ms = jnp.mean(x * x, axis=-1, keepdims=True)
    h = (x * lax.rsqrt(ms + EPS) * g_ref[...]).astype(BF16)

    vt_ref[...] = lax.dot_general(wvt_ref[...], h, _NT, preferred_element_type=F32).astype(BF16)

    pos = (pl.program_id(0) * tm) % seq + lax.broadcasted_iota(jnp.int32, (tm, 1), 0)
    pos_lo = (pos & (LANES - 1)).astype(F32)
    pos_hi = (pos >> POS_SHIFT).astype(F32)

    first_half = lax.broadcasted_iota(jnp.int32, (1, LANES), 1) < ATT_HEAD_DIM

    def head_norm(raw, tab_ref, out_ref):
        for hd in range(ATT_HEADS):
            slab = raw[:, LANES * hd:LANES * (hd + 1)]
            sq = slab * slab
            ss_all = jnp.sum(sq, axis=-1, keepdims=True)
            ss_0 = jnp.sum(jnp.where(first_half, sq, 0.0), axis=-1, keepdims=True)
            ss = jnp.where(first_half, ss_0, ss_all - ss_0)
            normed = slab * lax.rsqrt(ss * (1.0 / ATT_HEAD_DIM) + EPS)
            for c in range(2):
                cols = slice(LANES * (2 * hd + c), LANES * (2 * hd + c + 1))
                data = normed if c == 0 else pltpu.roll(normed, ATT_HEAD_DIM, axis=1)
                aux = tab_ref[1:2, cols] + tab_ref[2:3, cols] * pos_lo + tab_ref[3:4, cols] * pos_hi
                out_ref[:, cols] = jnp.where(first_half, data * tab_ref[0:1, cols], aux).astype(BF16)

    qk = jnp.dot(h, wqk_ref[...], preferred_element_type=F32)
    head_norm(qk[:, 0:ATT_QK_WIDTH], qtab_ref, q_ref)
    head_norm(qk[:, ATT_QK_WIDTH:2 * ATT_QK_WIDTH], ktab_ref, k_ref)

    sc = jnp.dot(h, wm_ref[:, 0:768], preferred_element_type=F32)
    cv_ref[:, 0:256] = sc[:, 0:256].astype(BF16)
    cv_ref[:, 256:512] = (sc[:, 256:512] * sc[:, 512:768]).astype(BF16)

    hg = jnp.dot(h, wm_ref[:, 768:2048], preferred_element_type=F32)
    hp_ref[:, 0:256] = jax.nn.silu(hg[:, 0:256]).astype(BF16)
    hp_ref[:, 256:512] = hg[:, 256:512].astype(BF16)
    for d in range(2):
        z = hg[:, 512 + 256 * d:768 + 256 * d]
        lb = lb_ref[d:d + 1, :]
        f = lb + (1.0 - lb) * jax.nn.sigmoid(z)
        hp_ref[:, 512 + 256 * d:768 + 256 * d] = ((1.0 - lb) * jax.nn.sigmoid(-z)).astype(BF16)
        lg_ref[:, 256 * d:256 * d + 256] = jnp.log(f)
    sg_ref[...] = jax.nn.silu(hg[:, 1024:1280]).astype(BF16)

    gate_ref[...] = jax.nn.sigmoid(
        jnp.dot(h, wm_ref[:, 2048:5120], preferred_element_type=F32)).astype(BF16)


def _inproj(x, gain, w_main, w_vt, w_qk, qtab, ktab, lb, seq, tm):
    n = x.shape[0]
    row = lambda w: pl.BlockSpec((tm, w), lambda i: (i, 0))
    out_shape = (
        jax.ShapeDtypeStruct((n, 1024), BF16),
        jax.ShapeDtypeStruct((n, 1024), BF16),
        jax.ShapeDtypeStruct((512, n), BF16),
        jax.ShapeDtypeStruct((n, 512), BF16),
        jax.ShapeDtypeStruct((n, 1024), BF16),
        jax.ShapeDtypeStruct((n, 512), F32),
        jax.ShapeDtypeStruct((n, 256), BF16),
        jax.ShapeDtypeStruct((n, 3072), BF16),
    )
    kern = functools.partial(_inproj_kernel, tm=tm, seq=seq)
    return pl.pallas_call(
        kern,
        grid=(n // tm,),
        in_specs=[row(D_MODEL), _const_spec((1, D_MODEL)), _const_spec(w_main.shape), _const_spec(w_vt.shape),
                  _const_spec(w_qk.shape), _const_spec((8, 1024)), _const_spec((8, 1024)),
                  _const_spec((2, 256))],
        out_specs=(row(1024), row(1024), pl.BlockSpec((512, tm), lambda i: (0, i)), row(512), row(1024),
                   row(512), row(256), row(3072)),
        out_shape=out_shape,
        compiler_params=_params(("parallel",)),
        name="inproj",
    )(x, gain, w_main, w_vt, w_qk, qtab, ktab, lb)


def _alibi_tables(q_gain, k_gain):
    slopes = np.exp2(-8.0 * (np.arange(ATT_HEADS, dtype=np.float64) + 1.0) / ATT_HEADS) * LOG2E
    qc = np.zeros((8, 1024), np.float32)
    kc = np.zeros((8, 1024), np.float32)
    for h in range(ATT_HEADS):
        rest = np.float32(slopes[h])
        pieces = []
        for _ in range(N_SLOPE_PIECES):
            piece = np.asarray(rest, dtype=BF16).astype(np.float32)
            pieces.append(float(piece))
            rest = np.float32(rest - piece)
        for c in range(2):
            base = LANES * (2 * h + c) + ATT_HEAD_DIM
            for i, a in enumerate(pieces):
                qc[1, base + 2 * i] = -a
                qc[1, base + 2 * i + 1] = -a * LANES
                kc[2, base + 2 * i] = 1.0
                kc[3, base + 2 * i + 1] = 1.0
                off = base + 2 * N_SLOPE_PIECES
                qc[2, off + 2 * i] = 1.0
                qc[3, off + 2 * i + 1] = 1.0
                kc[1, off + 2 * i] = a
                kc[1, off + 2 * i + 1] = a * LANES
    lane = np.arange(1024) % LANES
    lane_is_data = lane < ATT_HEAD_DIM
    qg = jnp.where(lane_is_data, jnp.tile(q_gain, 16) * (LOG2E * ATT_HEAD_DIM ** -0.5), 0.0)
    kg = jnp.where(lane_is_data, jnp.tile(k_gain, 16), 0.0)
    bound = 1.02 * LOG2E * ATT_HEAD_DIM ** 0.5 * jnp.max(jnp.abs(q_gain)) * jnp.max(jnp.abs(k_gain))
    qconst = jnp.asarray(qc[1])
    rest = bound.astype(F32)
    bound_lane0 = ATT_HEAD_DIM + 4 * N_SLOPE_PIECES
    for i in range(N_SLOPE_PIECES):
        piece = rest.astype(BF16).astype(F32)
        rest = rest - piece
        qconst = jnp.where(lane == bound_lane0 + i, -piece, qconst)
        kc[1, lane == bound_lane0 + i] = 1.0
    qtab = jnp.asarray(qc).at[0].set(qg).at[1].set(qconst)
    ktab = jnp.asarray(kc).at[0].set(kg)
    return qtab, ktab, bound <= SCORE_BOUND_LIMIT


def _attn_finish(sc_ref, subg_ref, o_ref, l0, l1, acc_ref):
    lam = sc_ref[ATT_HEADS]
    o = acc_ref[0] / l0 - lam * (acc_ref[1] / l1)
    ms = jnp.mean(o * o, axis=0, keepdims=True)
    o = o * lax.rsqrt(ms + EPS) * subg_ref[...]
    o_ref[...] = o.T.astype(BF16)


def _attn_key_tile(qi, st, nk, reach, band):
    if band:
        off = jnp.where(st <= reach, st, reach - st)
        kb = qi + off
        return jnp.clip(kb, 0, nk - 1), off > 0, (kb >= 0) & (kb < nk)
    return (qi + st) % nk, qi + st < nk, st >= 0


def _attn_fast_kernel(sc_ref, q_ref, k_ref, vt_ref, absd_ref, subg_ref, o_ref, qv_ref, l_ref, acc_ref,
                      *, tq, tk, nk, head, reach, band, nsteps):
    h = head
    qi = pl.program_id(1)
    st = pl.program_id(2)
    _, after, exists = _attn_key_tile(qi, st, nk, reach, band)

    @pl.when(st == 0)
    def _():
        lane = lax.broadcasted_iota(jnp.int32, (tq, LANES), 1)
        alibi = (lane >= ATT_HEAD_DIM) & (lane < ATT_HEAD_DIM + 4 * N_SLOPE_PIECES)
        for c in range(2):
            qc = q_ref[:, LANES * c:LANES * (c + 1)]
            qv_ref[c] = qc
            qv_ref[2 + c] = jnp.where(alibi, -qc, qc)
            qv_ref[4 + c] = jnp.where(alibi, jnp.zeros_like(qc), qc)
        l_ref[...] = jnp.zeros(l_ref.shape, F32)
        acc_ref[...] = jnp.zeros(acc_ref.shape, F32)

    def step(diag):
        vt = vt_ref[...]
        for c in range(2):
            kc = k_ref[:, LANES * c:LANES * (c + 1)]
            if diag:
                qsel = qv_ref[4 + c]
            else:
                qsel = qv_ref[jnp.where(after, 0, 2) + c]
            s = lax.dot_general(kc, qsel, _NT, preferred_element_type=F32)
            if diag:
                s = s - sc_ref[h] * absd_ref[...]
            p = jnp.exp2(s)
            l_ref[c] += jnp.sum(p.reshape(tk // 8, 8, tq), axis=0)
            acc_ref[c] += jnp.dot(vt, p.astype(BF16), preferred_element_type=F32)

    @pl.when(st == 0)
    def _():
        step(True)

    @pl.when((st != 0) & exists)
    def _():
        step(False)

    @pl.when(st == nsteps - 1)
    def _():
        _attn_finish(sc_ref, subg_ref, o_ref, jnp.sum(l_ref[0], axis=0, keepdims=True),
                     jnp.sum(l_ref[1], axis=0, keepdims=True), acc_ref)


def _attn_safe_kernel(sc_ref, q_ref, k_ref, vt_ref, subg_ref, o_ref, qm_ref, m_ref, l_ref, acc_ref, *, tq, tk, nk):
    h = pl.program_id(1)
    qi = pl.program_id(2)
    ki = pl.program_id(3)

    @pl.when(ki == 0)
    def _():
        lane = lax.broadcasted_iota(jnp.int32, (tq, LANES), 1)
        for c in range(2):
            qc = q_ref[:, LANES * c:LANES * (c + 1)]
            qm_ref[c] = jnp.where(lane < ATT_HEAD_DIM, qc, jnp.zeros_like(qc))
        m_ref[...] = jnp.full(m_ref.shape, NEG_BIG, F32)
        l_ref[...] = jnp.zeros(l_ref.shape, F32)
        acc_ref[...] = jnp.zeros(acc_ref.shape, F32)

    kpos = ki * tk + lax.broadcasted_iota(jnp.int32, (tk, tq), 0)
    qpos = qi * tq + lax.broadcasted_iota(jnp.int32, (tk, tq), 1)
    bias = sc_ref[h] * jnp.abs(kpos - qpos).astype(F32)
    vt = vt_ref[...]
    for c in range(2):
        kc = k_ref[:, LANES * c:LANES * (c + 1)]
        s = lax.dot_general(kc, qm_ref[c], _NT, preferred_element_type=F32) - bias
        m_prev = m_ref[c]
        m_new = jnp.maximum(m_prev, jnp.max(s, axis=0, keepdims=True))
        alpha = jnp.exp2(m_prev - m_new)
        p = jnp.exp2(s - m_new)
        l_ref[c] = alpha * l_ref[c] + jnp.sum(p, axis=0, keepdims=True)
        acc_ref[c] = alpha * acc_ref[c] + jnp.dot(vt, p.astype(BF16), preferred_element_type=F32)
        m_ref[c] = m_new

    @pl.when(ki == nk - 1)
    def _():
        _attn_finish(sc_ref, subg_ref, o_ref, l_ref[0], l_ref[1], acc_ref)


def _attention(q, k, vt, scal, subg, fast_ok, b, t, tq, tk):
    nq, nk = t // tq, t // tk
    blk = 2 * LANES
    common = dict(
        grid=(b, ATT_HEADS, nq, nk),
        out_specs=pl.BlockSpec((None, tq, LANES), lambda bi, h, qi, ki: (bi, qi, h)),
        out_shape=jax.ShapeDtypeStruct((b, t, ATT_WIDTH), BF16),
        compiler_params=_params(("parallel", "parallel", "parallel", "arbitrary")),
    )
    qkv_specs = [
        pl.BlockSpec((None, tq, blk), lambda bi, h, qi, ki: (bi, qi, h)),
        pl.BlockSpec((None, tk, blk), lambda bi, h, qi, ki: (bi, ki, h)),
        pl.BlockSpec((LANES, tk), lambda bi, h, qi, ki: (h, bi * nk + ki)),
    ]
    smem = pl.BlockSpec(memory_space=pltpu.SMEM)
    idx = np.arange(tk)[:, None] - np.arange(tq)[None, :]
    absd = jnp.asarray(np.abs(idx), dtype=F32)
    slopes2 = np.exp2(-8.0 * (np.arange(ATT_HEADS) + 1.0) / ATT_HEADS) * LOG2E
    window = tuple(int((EXP2_ZERO_ARG / s - 1.0) // tk) + 1 for s in slopes2)
    def fast_head(h, q, k, vt, scal, subg):
        reach = window[h]
        band = 2 * reach + 1 < nk
        nsteps = 2 * reach + 1 if band else nk
        tile = lambda qi, st: _attn_key_tile(qi, st, nk, reach, band)[0]
        return pl.pallas_call(
            functools.partial(_attn_fast_kernel, tq=tq, tk=tk, nk=nk, head=h, reach=reach, band=band,
                              nsteps=nsteps),
            grid=(b, nq, nsteps),
            in_specs=[
                smem,
                pl.BlockSpec((None, tq, blk), lambda bi, qi, st: (bi, qi, h)),
                pl.BlockSpec((None, tk, blk), lambda bi, qi, st: (bi, tile(qi, st), h)),
                pl.BlockSpec((LANES, tk), lambda bi, qi, st: (h, bi * nk + tile(qi, st))),
                _const_spec((tk, tq)), _const_spec((ATT_V_DIM, 1)),
            ],
            out_specs=pl.BlockSpec((None, tq, LANES), lambda bi, qi, st: (bi, qi, 0)),
            out_shape=jax.ShapeDtypeStruct((b, t, LANES), BF16),
            scratch_shapes=[
                pltpu.VMEM((6, tq, LANES), BF16),
                pltpu.VMEM((2, 8, tq), F32),
                pltpu.VMEM((2, ATT_V_DIM, tq), F32),
            ],
            compiler_params=_params(("parallel", "parallel", "arbitrary")),
            name="diff_attn_h%d" % h,
        )(scal, q, k, vt, absd, subg)

    def fast(q, k, vt, scal, subg):
        return tuple(fast_head(h, q, k, vt, scal, subg) for h in range(ATT_HEADS))

    def safe(q, k, vt, scal, subg):
        o = safe_call(q, k, vt, scal, subg)
        return tuple(o[:, :, LANES * h:LANES * (h + 1)] for h in range(ATT_HEADS))

    def safe_call(q, k, vt, scal, subg):
        return pl.pallas_call(
            functools.partial(_attn_safe_kernel, tq=tq, tk=tk, nk=nk),
            in_specs=[smem] + qkv_specs + [_const_spec((ATT_V_DIM, 1))],
            scratch_shapes=[
                pltpu.VMEM((2, tq, LANES), BF16),
                pltpu.VMEM((2, 1, tq), F32),
                pltpu.VMEM((2, 1, tq), F32),
                pltpu.VMEM((2, ATT_V_DIM, tq), F32),
            ],
            name="diff_attn_safe",
            **common,
        )(scal, q, k, vt, subg)

    return lax.cond(fast_ok, fast, safe, q, k, vt, scal, subg)


N_LEVELS = 6


def _hgrn_constants(tt):
    c = HG_CHUNK
    t = np.arange(tt)
    same_chunk = (t[:, None] // c) == (t[None, :] // c)
    mats, masks = [], []
    for j in range(N_LEVELS):
        m = 1 << j
        blk = t // m
        same_blk = blk[:, None] == blk[None, :]
        odd = (blk % 2) == 1
        incl = same_blk & (t[None, :] <= t[:, None])
        excl_rev = same_blk & (t[None, :] > t[:, None])
        mats.append(np.where(odd[:, None], incl, excl_rev))
        same_2m = (t[:, None] // (2 * m)) == (t[None, :] // (2 * m))
        masks.append((same_2m & odd[:, None] & (~odd)[None, :]).astype(np.float32))
    mats.append(same_chunk & (t[None, :] <= t[:, None]))
    mats.append(same_chunk & (t[None, :] > t[:, None]))
    mats = np.stack(mats).astype(np.float32)
    masks = np.stack(masks)
    mats_b = mats[:, ::-1, ::-1]
    masks_b = masks[:, ::-1, ::-1]
    mst = np.concatenate([mats.reshape(-1, tt), mats_b.reshape(-1, tt)], axis=0)
    amask = np.concatenate([masks, masks_b], axis=0)
    return jnp.asarray(mst, dtype=BF16), jnp.asarray(amask, dtype=F32)


def _hgrn_kernel(hpf_ref, hpb_ref, lgf_ref, lgb_ref, mst_ref, amask_ref, g256_ref, of_ref, ob_ref, s_ref, *, tt):
    i = pl.program_id(1)
    nlev = N_LEVELS
    nmat = N_LEVELS + 2
    nchunk = tt // HG_CHUNK

    @pl.when(i == 0)
    def _():
        s_ref[...] = jnp.zeros(s_ref.shape, F32)

    lane256 = lax.broadcasted_iota(jnp.int32, (1, 256), 1)
    lane128 = lax.broadcasted_iota(jnp.int32, (1, LANES), 1)
    r128 = lax.broadcasted_iota(jnp.int32, (LANES, LANES), 0)
    c128 = lax.broadcasted_iota(jnp.int32, (LANES, LANES), 1)
    bdmask = ((r128 // HG_HEAD_DIM) == (c128 // HG_HEAD_DIM)).astype(F32)

    def direction(d, hp_ref, lg_ref, o_ref):
        qs = hp_ref[:, 0:256].astype(F32)
        v = hp_ref[:, 256:512]
        kk = hp_ref[:, 512 + 256 * d:768 + 256 * d].astype(F32)
        eall = jnp.dot(mst_ref[d * nmat * tt:(d + 1) * nmat * tt, :], lg_ref[...].astype(BF16),
                       preferred_element_type=F32)

        def expo(j):
            return jnp.exp(eall[j * tt:(j + 1) * tt])

        a = [None] * 4
        for lev in range(nlev):
            xx = expo(lev)
            qt, kt = (qs * xx).astype(BF16), (kk * xx).astype(BF16)
            am = amask_ref[d * nlev + lev]
            for pr in range(2):
                qp = qt[:, LANES * pr:LANES * (pr + 1)]
                kp = kt[:, LANES * pr:LANES * (pr + 1)]
                for hh in range(2):
                    sel = (lane128 // HG_HEAD_DIM) == hh
                    qm = jnp.where(sel, qp, jnp.zeros_like(qp))
                    p = lax.dot_general(qm, kp, _NT, preferred_element_type=F32) * am
                    idx = 2 * pr + hh
                    a[idx] = p if a[idx] is None else a[idx] + p

        o = jnp.dot((qs * kk).astype(BF16), g256_ref[...], preferred_element_type=F32) * v.astype(F32)
        for idx in range(4):
            vm = jnp.where((lane256 // HG_HEAD_DIM) == idx, v, jnp.zeros_like(v))
            contrib = jnp.dot(a[idx].astype(BF16), vm, preferred_element_type=F32)
            o = o + contrib
        o_ref[...] = o

        xq = expo(N_LEVELS)
        xk = expo(N_LEVELS + 1)
        qc = (qs * xq).astype(BF16)
        kc = (kk * xk).astype(BF16)
        order = range(nchunk) if d == 0 else range(nchunk - 1, -1, -1)
        for c in order:
            r0 = c * HG_CHUNK
            rows = slice(r0, r0 + HG_CHUNK)
            drow = r0 + HG_CHUNK - 1 if d == 0 else r0
            for pr in range(2):
                cols = slice(LANES * pr, LANES * (pr + 1))
                st = s_ref[d, pr]
                inter = lax.dot_general(qc[rows, cols], st.astype(BF16), _NT, preferred_element_type=F32)
                o_ref[rows, cols] += inter
                ut = lax.dot_general(v[rows, cols], kc[rows, cols], _TN, preferred_element_type=F32)
                dec = xq[drow:drow + 1, cols]
                s_ref[d, pr] = st * dec + ut * bdmask

    direction(0, hpf_ref, lgf_ref, of_ref)
    direction(1, hpb_ref, lgb_ref, ob_ref)


def _hgrn(hp, lg, b, t, tt):
    nt = t // tt
    mst, amask = _hgrn_constants(tt)
    kern = functools.partial(_hgrn_kernel, tt=tt)
    fwd = lambda w, cb: pl.BlockSpec((None, tt, w), lambda bi, i: (bi, i, cb))
    bwd = lambda w, cb: pl.BlockSpec((None, tt, w), lambda bi, i: (bi, nt - 1 - i, cb))
    return pl.pallas_call(
        kern,
        grid=(b, nt),
        in_specs=[fwd(1024, 0), bwd(1024, 0), fwd(256, 0), bwd(256, 1),
                  _const_spec(mst.shape), _const_spec(amask.shape), _const_spec((256, 256))],
        out_specs=(fwd(256, 0), bwd(256, 0)),
        out_shape=(jax.ShapeDtypeStruct((b, t, HG_WIDTH), F32), jax.ShapeDtypeStruct((b, t, HG_WIDTH), F32)),
        scratch_shapes=[pltpu.VMEM((2, 2, LANES, LANES), F32)],
        compiler_params=_params(("parallel", "arbitrary")),
        name="hgrn2",
    )(hp, hp, lg, lg, mst, amask, _block_ones(256, HG_HEAD_DIM))


def _merge_kernel(*refs, tm, seq, with_router):
    oa_refs = refs[:ATT_HEADS]
    refs = refs[ATT_HEADS:]
    (x_ref, cv_ref, cvp_ref, cvn_ref, of_ref, ob_ref, sg_ref, gate_ref, cw_ref, hgn_ref, g256_ref,
     wa_ref, wb_ref, wc_ref, wo_ref, g2_ref) = refs[:16]
    if with_router:
        wr_ref, tri_ref, xo_ref, h2_ref, lo_ref = refs[16:]
    else:
        xo_ref, h2_ref = refs[16:]
    i = pl.program_id(0)
    tile_start = (i * tm) % seq

    u = cv_ref[:, 256:512].astype(F32)
    row = lax.broadcasted_iota(jnp.int32, (tm, 1), 0)
    prev_row = jnp.where(tile_start == 0, 0.0, cvp_ref[7:8, 256:512].astype(F32))
    next_row = jnp.where(tile_start + tm == seq, 0.0, cvn_ref[0:1, 256:512].astype(F32))
    u_m1 = jnp.where(row == 0, prev_row, pltpu.roll(u, 1, axis=0))
    u_p1 = jnp.where(row == tm - 1, next_row, pltpu.roll(u, tm - 1, axis=0))
    conv = cw_ref[0:1, :] * u_m1 + cw_ref[1:2, :] * u + cw_ref[2:3, :] * u_p1
    yb_in = (cv_ref[:, 0:256].astype(F32) * conv).astype(BF16)

    o = of_ref[...] + ob_ref[...]
    ss = _group_sum(o * o, g256_ref[...])
    yc_in = (o * lax.rsqrt(ss * (1.0 / HG_HEAD_DIM) + EPS) * hgn_ref[...] * sg_ref[...].astype(F32)).astype(BF16)

    oa = jnp.concatenate([r[...] for r in oa_refs], axis=1)
    ya = jnp.dot(oa, wa_ref[...], preferred_element_type=F32)
    yb = jnp.dot(yb_in, wb_ref[...], preferred_element_type=F32)
    yc = jnp.dot(yc_in, wc_ref[...], preferred_element_type=F32)
    merged = (gate_ref[:, 0:1024].astype(F32) * ya + gate_ref[:, 1024:2048].astype(F32) * yb
              + gate_ref[:, 2048:3072].astype(F32) * yc)
    xn = x_ref[...] + jnp.dot(merged.astype(BF16), wo_ref[...], preferred_element_type=F32)
    xo_ref[...] = xn

    ms = jnp.mean(xn * xn, axis=-1, keepdims=True)
    h2 = xn * lax.rsqrt(ms + EPS) * g2_ref[...]
    h_hi = h2.astype(BF16)
    h2_ref[...] = h_hi
    if with_router:
        h_lo = (h2 - h_hi.astype(F32)).astype(BF16)
        t1 = lax.dot_general(wr_ref[0:16, :], h_hi, _NT, preferred_element_type=F32)
        t2 = lax.dot_general(wr_ref[16:32, :], h_lo, _NT, preferred_element_type=F32)
        lg = t1[0:N_EXPERTS] + t1[N_EXPERTS:2 * N_EXPERTS] + t2[0:N_EXPERTS]

        sub = lax.broadcasted_iota(jnp.int32, lg.shape, 0)
        m1 = jnp.max(lg, axis=0, keepdims=True)
        i1 = jnp.min(jnp.where(lg == m1, sub, N_EXPERTS), axis=0, keepdims=True)
        lg2 = jnp.where(sub == i1, NEG_BIG, lg)
        m2 = jnp.max(lg2, axis=0, keepdims=True)
        i2 = jnp.min(jnp.where(lg2 == m2, sub, N_EXPERTS), axis=0, keepdims=True)
        e2 = jnp.exp(m2 - m1)
        weights = jnp.where(sub == i1, 1.0 / (1.0 + e2), 0.0) + jnp.where(sub == i2, e2 / (1.0 + e2), 0.0)
        routed = (sub == i1) | (sub == i2)
        routed_f = routed.astype(F32)
        nblk = tm // LANES
        by_block = jnp.concatenate([routed_f[:, LANES * j:LANES * (j + 1)] for j in range(nblk)], axis=0)
        within = jnp.dot(by_block.astype(BF16), tri_ref[...], preferred_element_type=F32)
        before = jnp.zeros((N_EXPERTS, LANES), F32)
        ranks = []
        for j in range(nblk):
            wj = within[N_EXPERTS * j:N_EXPERTS * (j + 1)]
            ranks.append(wj + before - 1.0)
            before = before + wj[:, LANES - 1:LANES]
        rank = jnp.concatenate(ranks, axis=1)
        lo_ref[0:N_EXPERTS, :] = jnp.where(routed, rank, -1.0)
        lo_ref[N_EXPERTS:2 * N_EXPERTS, :] = weights


def _merge(x, oa, cv, of, ob, sg, gates, conv_w, hg_norm, wa, wb, wc, wo, g2, w_router, seq, tm):
    n = x.shape[0]
    nb8 = n // 8
    step8 = tm // 8
    row = lambda w: pl.BlockSpec((tm, w), lambda i: (i, 0))
    in_specs = [
        row(LANES), row(LANES), row(LANES), row(LANES), row(D_MODEL), row(512),
        pl.BlockSpec((8, 512), lambda i: (jnp.maximum(i * step8 - 1, 0), 0)),
        pl.BlockSpec((8, 512), lambda i: (jnp.minimum((i + 1) * step8, nb8 - 1), 0)),
        row(256), row(256), row(256), row(3072),
        _const_spec((8, 256)), _const_spec((1, 256)), _const_spec((256, 256)),
        _const_spec(wa.shape), _const_spec(wb.shape), _const_spec(wc.shape), _const_spec(wo.shape),
        _const_spec((1, D_MODEL)),
    ]
    args = list(oa) + [x, cv, cv, cv, of, ob, sg, gates, conv_w, hg_norm, _block_ones(256, HG_HEAD_DIM),
            wa, wb, wc, wo, g2]
    out_specs = [row(D_MODEL), row(D_MODEL)]
    out_shape = [jax.ShapeDtypeStruct((n, D_MODEL), F32), jax.ShapeDtypeStruct((n, D_MODEL), BF16)]
    with_router = w_router is not None
    if with_router:
        lane = np.arange(LANES)
        tri = jnp.asarray((lane[:, None] <= lane[None, :]).astype(np.float32), dtype=BF16)
        in_specs += [_const_spec(w_router.shape), _const_spec((LANES, LANES))]
        args += [w_router, tri]
        out_specs.append(pl.BlockSpec((2 * N_EXPERTS, tm), lambda i: (0, i)))
        out_shape.append(jax.ShapeDtypeStruct((2 * N_EXPERTS, n), F32))
    kern = functools.partial(_merge_kernel, tm=tm, seq=seq, with_router=with_router)
    return pl.pallas_call(
        kern,
        grid=(n // tm,),
        in_specs=in_specs,
        out_specs=tuple(out_specs),
        out_shape=tuple(out_shape),
        compiler_params=_params(("parallel",)),
        name="merge_router" if with_router else "merge",
    )(*args)


def _ffn_kernel(x_ref, h_ref, wg_ref, wu_ref, wd_ref, o_ref):
    h = h_ref[...]
    g = jnp.dot(h, wg_ref[...], preferred_element_type=F32)
    u = jnp.dot(h, wu_ref[...], preferred_element_type=F32)
    act = (jax.nn.silu(g) * u).astype(BF16)
    o_ref[...] = x_ref[...] + jnp.dot(act, wd_ref[...], preferred_element_type=F32)


def _ffn(x, h2, wg, wu, wd, tm):
    n = x.shape[0]
    row = lambda: pl.BlockSpec((tm, D_MODEL), lambda i: (i, 0))
    return pl.pallas_call(
        _ffn_kernel,
        grid=(n // tm,),
        in_specs=[row(), row(), _const_spec(wg.shape), _const_spec(wu.shape), _const_spec(wd.shape)],
        out_specs=row(),
        out_shape=jax.ShapeDtypeStruct((n, D_MODEL), F32),
        compiler_params=_params(("parallel",)),
        name="ffn_dense",
    )(x, h2, wg, wu, wd)


MOE_ROWS = 144


def _moe_expert_kernel(rt_ref, h_ref, y_ref, wg_ref, wu_ref, wd_ref, o_ref, *, expert, tm):
    rank = rt_ref[expert:expert + 1, :].astype(jnp.int32)
    w_row = rt_ref[N_EXPERTS + expert:N_EXPERTS + expert + 1, :]
    count = jnp.max(rank) + 1

    h = h_ref[...]

    def expert_rows(blk):
        rows = blk * MOE_ROWS + lax.broadcasted_iota(jnp.int32, (MOE_ROWS, tm), 0)
        sel = rank == rows
        sel_b = sel.astype(F32).astype(BF16)
        xe = jnp.dot(sel_b, h, preferred_element_type=F32).astype(BF16)
        g = jnp.dot(xe, wg_ref[...], preferred_element_type=F32)
        u = jnp.dot(xe, wu_ref[...], preferred_element_type=F32)
        act = (jax.nn.silu(g) * u).astype(BF16)
        ye = jnp.dot(act, wd_ref[...], preferred_element_type=F32)
        w_sel = jnp.sum(jnp.where(sel, w_row, 0.0), axis=1, keepdims=True)
        ye = (ye * w_sel).astype(BF16)
        return lax.dot_general(sel_b, ye, _TN, preferred_element_type=F32)

    o_ref[...] = y_ref[...] + expert_rows(0)

    def body(blk, carry):
        o_ref[...] += expert_rows(blk)
        return carry

    lax.fori_loop(1, (count + MOE_ROWS - 1) // MOE_ROWS, body, 0)


def _moe(x, h2, routing, wg, wu, wd, tm):
    n = x.shape[0]
    row = lambda: pl.BlockSpec((tm, D_MODEL), lambda i: (i, 0))
    expert_w = lambda w, e: pl.BlockSpec((None,) + w.shape[1:], lambda i: (e, 0, 0), pipeline_mode=pl.Buffered(1))
    y = x
    for e in range(N_EXPERTS):
        y = pl.pallas_call(
            functools.partial(_moe_expert_kernel, expert=e, tm=tm),
            grid=(n // tm,),
            in_specs=[pl.BlockSpec((2 * N_EXPERTS, tm), lambda i: (0, i)), row(), row(),
                      expert_w(wg, e), expert_w(wu, e), expert_w(wd, e)],
            out_specs=row(),
            out_shape=jax.ShapeDtypeStruct((n, D_MODEL), F32),
            input_output_aliases={2: 0},
            compiler_params=_params(("parallel",)),
            name="moe_expert",
        )(routing, h2, y, wg, wu, wd)
    return y


def _pick(limit, total):
    return min(limit, total)


def kernel(x_prompt, x_sample, norm_mix, w_in, q_norm, k_norm, lambda_q1, lambda_k1, lambda_q2, lambda_k2, sub_norm, conv_w, hg_lower, hg_norm, w_up_a, w_up_b, w_up_c, w_out, norm_ffn, w_gate_dense, w_up_dense, w_down_dense, w_router, w_gate_moe, w_up_moe, w_down_moe):
    depth = norm_mix.shape[0]
    lb_all = jnp.cumsum(jax.nn.softmax(hg_lower.astype(F32), axis=0), axis=0)
    lb_all = lb_all - lb_all[0]
    slopes = jnp.exp2(-8.0 * (jnp.arange(ATT_HEADS, dtype=F32) + 1.0) / ATT_HEADS)

    layers = []
    for l in range(depth):
        lam_init = 0.8 - 0.6 * math.exp(-0.3 * l)
        lam = (jnp.exp(jnp.sum(lambda_q1[l].astype(F32) * lambda_k1[l].astype(F32)))
               - jnp.exp(jnp.sum(lambda_q2[l].astype(F32) * lambda_k2[l].astype(F32))) + lam_init)
        wl = w_in[l]
        qtab, ktab, fast_ok = _alibi_tables(q_norm[l].astype(F32), k_norm[l].astype(F32))
        p = dict(
            gain=norm_mix[l].reshape(1, D_MODEL),
            w_main=wl[:, 1536:].astype(BF16),
            w_vt=wl[:, 1024:1536].T.astype(BF16),
            w_qk=wl[:, 0:2 * ATT_QK_WIDTH].astype(BF16),
            qtab=qtab, ktab=ktab, fast_ok=fast_ok,
            lb=lb_all[l],
            scal=jnp.concatenate([slopes * LOG2E, lam.reshape(1), jnp.zeros((3,), F32)]),
            subg=(sub_norm[l] * (1.0 - lam_init)).reshape(ATT_V_DIM, 1),
            conv_w=jnp.concatenate([conv_w[l], jnp.zeros((5, SC_WIDTH), F32)], axis=0),
            hg_norm=hg_norm[l].reshape(1, HG_WIDTH),
            wa=w_up_a[l].astype(BF16), wb=w_up_b[l].astype(BF16), wc=w_up_c[l].astype(BF16),
            wo=w_out[l].astype(BF16),
            g2=norm_ffn[l].reshape(1, D_MODEL),
        )
        j = l // 2
        if l % 2 == 0:
            p.update(wg=w_gate_dense[j].astype(BF16), wu=w_up_dense[j].astype(BF16),
                     wd=w_down_dense[j].astype(BF16))
        else:
            wr_hi, wr_lo = _split_bf16(w_router[j].T.astype(F32))
            p.update(w_router=jnp.concatenate([wr_hi, wr_lo, wr_hi, jnp.zeros_like(wr_hi)], axis=0),
                     wg=w_gate_moe[j].astype(BF16), wu=w_up_moe[j].astype(BF16), wd=w_down_moe[j].astype(BF16))
        layers.append(p)

    def trunk(x3):
        b, t, _ = x3.shape
        n = b * t
        tm = _pick(512, t)
        tq = _pick(1024, t)
        tr = _pick(1024, t)
        tt = _pick(256, t)
        x = x3.reshape(n, D_MODEL)
        for l, p in enumerate(layers):
            q, k, vt, cv, hp, lg, sg, gates = _inproj(x, p["gain"], p["w_main"], p["w_vt"], p["w_qk"],
                                                      p["qtab"], p["ktab"], p["lb"], t, tm)
            oa = _attention(q.reshape(b, t, 1024), k.reshape(b, t, 1024), vt, p["scal"], p["subg"], p["fast_ok"],
                            b, t, tq, tq)
            of, ob = _hgrn(hp.reshape(b, t, 1024), lg.reshape(b, t, 512), b, t, tt)
            res = _merge(x, [o.reshape(n, LANES) for o in oa], cv, of.reshape(n, 256), ob.reshape(n, 256), sg, gates,
                         p["conv_w"], p["hg_norm"], p["wa"], p["wb"], p["wc"], p["wo"], p["g2"],
                         p.get("w_router"), t, tr)
            if l % 2 == 0:
                xn, h2 = res
                x = _ffn(xn, h2, p["wg"], p["wu"], p["wd"], tm)
            else:
                xn, h2, routing = res
                x = _moe(xn, h2, routing, p["wg"], p["wu"], p["wd"], tr)
        return x.reshape(b, t, D_MODEL)

    return (trunk(x_prompt), trunk(x_sample))
```

```python
import functools
import math

import numpy as np
import jax
import jax.numpy as jnp
from jax import lax
from jax.experimental import pallas as pl
from jax.experimental.pallas import tpu as pltpu

F32 = jnp.float32
BF16 = jnp.bfloat16

D_MODEL = 1024
ATT_HEADS = 4
ATT_HEAD_DIM = 64
ATT_V_DIM = 128
ATT_QK_WIDTH = 512
ATT_WIDTH = 512
SC_WIDTH = 256
HG_WIDTH = 256
HG_HEAD_DIM = 64
HG_CHUNK = 64
N_BRANCH = 3
N_EXPERTS = 8
EPS = 1e-6

LOG2E = math.log2(math.e)
N_SLOPE_PIECES = 3
SCORE_BOUND_LIMIT = 48.0
EXP2_ZERO_ARG = 151.0
ATT_TILES_PER_STEP = 2

LANES = 128
POS_SHIFT = 7
VMEM_LIMIT = 56 * 1024 * 1024
NEG_BIG = -1e30

_NT = (((1,), (1,)), ((), ()))
_TN = (((0,), (0,)), ((), ()))


def _const_spec(shape):
    nd = len(shape)
    return pl.BlockSpec(shape, lambda *_: (0,) * nd, pipeline_mode=pl.Buffered(1))


def _params(sem):
    return pltpu.CompilerParams(dimension_semantics=sem, vmem_limit_bytes=VMEM_LIMIT)


def _split_bf16(x):
    hi = x.astype(BF16)
    lo = (x - hi.astype(F32)).astype(BF16)
    return hi, lo


def _group_sum(x, gmat):
    hi, lo = _split_bf16(x)
    return (jnp.dot(hi, gmat, preferred_element_type=F32) + jnp.dot(lo, gmat, preferred_element_type=F32))


def _block_ones(width, group):
    idx = np.arange(width) // group
    return jnp.asarray((idx[:, None] == idx[None, :]).astype(np.float32), dtype=BF16)


def _inproj_kernel(x_ref, g_ref, wm_ref, wvt_ref, wqk_ref, qtab_ref, ktab_ref, lb_ref,
                   q_ref, k_ref, vt_ref, cv_ref, hp_ref, lg_ref, sg_ref, gate_ref, *, tm, seq):
    x = x_ref[...]
    ms = jnp.mean(x * x, axis=-1, keepdims=True)
    h = (x * lax.rsqrt(ms + EPS) * g_ref[...]).astype(BF16)

    vt_ref[...] = lax.dot_general(wvt_ref[...], h, _NT, preferred_element_type=F32).astype(BF16)

    pos = (pl.program_id(0) * tm) % seq + lax.broadcasted_iota(jnp.int32, (tm, 1), 0)
    pos_lo = (pos & (LANES - 1)).astype(F32)
    pos_hi = (pos >> POS_SHIFT).astype(F32)

    first_half = lax.broadcasted_iota(jnp.int32, (1, LANES), 1) < ATT_HEAD_DIM

    def head_norm(raw, tab_ref, out_ref):
        for hd in range(ATT_HEADS):
            slab = raw[:, LANES * hd:LANES * (hd + 1)]
            sq = slab * slab
            ss_all = jnp.sum(sq, axis=-1, keepdims=True)
            ss_0 = jnp.sum(jnp.where(first_half, sq, 0.0), axis=-1, keepdims=True)
            ss = jnp.where(first_half, ss_0, ss_all - ss_0)
            normed = slab * lax.rsqrt(ss * (1.0 / ATT_HEAD_DIM) + EPS)
            for c in range(2):
                cols = slice(LANES * (2 * hd + c), LANES * (2 * hd + c + 1))
                data = normed if c == 0 else pltpu.roll(normed, ATT_HEAD_DIM, axis=1)
                aux = tab_ref[1:2, cols] + tab_ref[2:3, cols] * pos_lo + tab_ref[3:4, cols] * pos_hi
                out_ref[:, cols] = jnp.where(first_half, data * tab_ref[0:1, cols], aux).astype(BF16)

    qk = jnp.dot(h, wqk_ref[...], preferred_element_type=F32)
    head_norm(qk[:, 0:ATT_QK_WIDTH], qtab_ref, q_ref)
    head_norm(qk[:, ATT_QK_WIDTH:2 * ATT_QK_WIDTH], ktab_ref, k_ref)

    sc = jnp.dot(h, wm_ref[:, 0:768], preferred_element_type=F32)
    cv_ref[:, 0:256] = sc[:, 0:256].astype(BF16)
    cv_ref[:, 256:512] = (sc[:, 256:512] * sc[:, 512:768]).astype(BF16)

    hg = jnp.dot(h, wm_ref[:, 768:2048], preferred_element_type=F32)
    hp_ref[:, 0:256] = jax.nn.silu(hg[:, 0:256]).astype(BF16)
    hp_ref[:, 256:512] = hg[:, 256:512].astype(BF16)
    for d in range(2):
        z = hg[:, 512 + 256 * d:768 + 256 * d]
        lb = lb_ref[d:d + 1, :]
        f = lb + (1.0 - lb) * jax.nn.sigmoid(z)
        hp_ref[:, 512 + 256 * d:768 + 256 * d] = ((1.0 - lb) * jax.nn.sigmoid(-z)).astype(BF16)
        lg_ref[:, 256 * d:256 * d + 256] = jnp.log(f)
    sg_ref[...] = jax.nn.silu(hg[:, 1024:1280]).astype(BF16)

    gate_ref[...] = jax.nn.sigmoid(
        jnp.dot(h, wm_ref[:, 2048:5120], preferred_element_type=F32)).astype(BF16)


def _inproj(x, gain, w_main, w_vt, w_qk, qtab, ktab, lb, seq, tm):
    n = x.shape[0]
    row = lambda w: pl.BlockSpec((tm, w), lambda i: (i, 0))
    out_shape = (
        jax.ShapeDtypeStruct((n, 1024), BF16),
        jax.ShapeDtypeStruct((n, 1024), BF16),
        jax.ShapeDtypeStruct((512, n), BF16),
        jax.ShapeDtypeStruct((n, 512), BF16),
        jax.ShapeDtypeStruct((n, 1024), BF16),
        jax.ShapeDtypeStruct((n, 512), F32),
        jax.ShapeDtypeStruct((n, 256), BF16),
        jax.ShapeDtypeStruct((n, 3072), BF16),
    )
    kern = functools.partial(_inproj_kernel, tm=tm, seq=seq)
    return pl.pallas_call(
        kern,
        grid=(n // tm,),
        in_specs=[row(D_MODEL), _const_spec((1, D_MODEL)), _const_spec(w_main.shape), _const_spec(w_vt.shape),
                  _const_spec(w_qk.shape), _const_spec((8, 1024)), _const_spec((8, 1024)),
                  _const_spec((2, 256))],
        out_specs=(row(1024), row(1024), pl.BlockSpec((512, tm), lambda i: (0, i)), row(512), row(1024),
                   row(512), row(256), row(3072)),
        out_shape=out_shape,
        compiler_params=_params(("parallel",)),
        name="inproj",
    )(x, gain, w_main, w_vt, w_qk, qtab, ktab, lb)


def _alibi_tables(q_gain, k_gain):
    slopes = np.exp2(-8.0 * (np.arange(ATT_HEADS, dtype=np.float64) + 1.0) / ATT_HEADS) * LOG2E
    qc = np.zeros((8, 1024), np.float32)
    kc = np.zeros((8, 1024), np.float32)
    for h in range(ATT_HEADS):
        rest = np.float32(slopes[h])
        pieces = []
        for _ in range(N_SLOPE_PIECES):
            piece = np.asarray(rest, dtype=BF16).astype(np.float32)
            pieces.append(float(piece))
            rest = np.float32(rest - piece)
        for c in range(2):
            base = LANES * (2 * h + c) + ATT_HEAD_DIM
            for i, a in enumerate(pieces):
                qc[1, base + 2 * i] = -a
                qc[1, base + 2 * i + 1] = -a * LANES
                kc[2, base + 2 * i] = 1.0
                kc[3, base + 2 * i + 1] = 1.0
                off = base + 2 * N_SLOPE_PIECES
                qc[2, off + 2 * i] = 1.0
                qc[3, off + 2 * i + 1] = 1.0
                kc[1, off + 2 * i] = a
                kc[1, off + 2 * i + 1] = a * LANES
    lane = np.arange(1024) % LANES
    lane_is_data = lane < ATT_HEAD_DIM
    qg = jnp.where(lane_is_data, jnp.tile(q_gain, 16) * (LOG2E * ATT_HEAD_DIM ** -0.5), 0.0)
    kg = jnp.where(lane_is_data, jnp.tile(k_gain, 16), 0.0)
    bound = 1.02 * LOG2E * ATT_HEAD_DIM ** 0.5 * jnp.max(jnp.abs(q_gain)) * jnp.max(jnp.abs(k_gain))
    qconst = jnp.asarray(qc[1])
    rest = bound.astype(F32)
    bound_lane0 = ATT_HEAD_DIM + 4 * N_SLOPE_PIECES
    for i in range(N_SLOPE_PIECES):
        piece = rest.astype(BF16).astype(F32)
        rest = rest - piece
        qconst = jnp.where(lane == bound_lane0 + i, -piece, qconst)
        kc[1, lane == bound_lane0 + i] = 1.0
    qtab = jnp.asarray(qc).at[0].set(qg).at[1].set(qconst)
    ktab = jnp.asarray(kc).at[0].set(kg)
    return qtab, ktab, bound <= SCORE_BOUND_LIMIT


def _attn_finish(sc_ref, subg_ref, o_ref, l0, l1, acc_ref):
    lam = sc_ref[ATT_HEADS]
    o = acc_ref[0] / l0 - lam * (acc_ref[1] / l1)
    ms = jnp.mean(o * o, axis=0, keepdims=True)
    o = o * lax.rsqrt(ms + EPS) * subg_ref[...]
    o_ref[...] = o.T.astype(BF16)


def _attn_key_tile(qi, st, nk, reach, band):
    if band:
        off = jnp.where(st <= reach, st, reach - st)
        kb = qi + off
        return jnp.clip(kb, 0, nk - 1), off > 0, (kb >= 0) & (kb < nk)
    return (qi + st) % nk, qi + st < nk, st >= 0


def _attn_fast_kernel(sc_ref, q_ref, *refs, tq, tk, nk, head, reach, band, nvisits, group):
    k_refs, vt_refs = refs[:group], refs[group:2 * group]
    absd_ref, subg_ref, o_ref, qv_ref, l_ref, acc_ref = refs[2 * group:]
    h = head
    qi = pl.program_id(1)
    st = pl.program_id(2)
    nsteps = -(-nvisits // group)

    @pl.when(st == 0)
    def _():
        lane = lax.broadcasted_iota(jnp.int32, (tq, LANES), 1)
        alibi = (lane >= ATT_HEAD_DIM) & (lane < ATT_HEAD_DIM + 4 * N_SLOPE_PIECES)
        for c in range(2):
            qc = q_ref[:, LANES * c:LANES * (c + 1)]
            qv_ref[c] = qc
            qv_ref[2 + c] = jnp.where(alibi, -qc, qc)
            qv_ref[4 + c] = jnp.where(alibi, jnp.zeros_like(qc), qc)
        l_ref[...] = jnp.zeros(l_ref.shape, F32)
        acc_ref[...] = jnp.zeros(acc_ref.shape, F32)

    def step(diag, k_ref, vt_ref, after):
        vt = vt_ref[...]
        for c in range(2):
            kc = k_ref[:, LANES * c:LANES * (c + 1)]
            if diag:
                qsel = qv_ref[4 + c]
            else:
                qsel = qv_ref[jnp.where(after, 0, 2) + c]
            s = lax.dot_general(kc, qsel, _NT, preferred_element_type=F32)
            if diag:
                s = s - sc_ref[h] * absd_ref[...]
            p = jnp.exp2(s)
            l_ref[c] += jnp.sum(p.reshape(tk // 8, 8, tq), axis=0)
            acc_ref[c] += jnp.dot(vt, p.astype(BF16), preferred_element_type=F32)

    for g in range(group):
        visit = group * st + g
        _, after, exists = _attn_key_tile(qi, visit, nk, reach, band)
        exists = exists & (visit < nvisits)
        if g == 0:
            pl.when(st == 0)(functools.partial(step, True, k_refs[0], vt_refs[0], after))
            exists = exists & (st != 0)
        pl.when(exists)(functools.partial(step, False, k_refs[g], vt_refs[g], after))

    @pl.when(st == nsteps - 1)
    def _():
        _attn_finish(sc_ref, subg_ref, o_ref, jnp.sum(l_ref[0], axis=0, keepdims=True),
                     jnp.sum(l_ref[1], axis=0, keepdims=True), acc_ref)


def _attn_safe_kernel(sc_ref, q_ref, k_ref, vt_ref, subg_ref, o_ref, qm_ref, m_ref, l_ref, acc_ref, *, tq, tk, nk):
    h = pl.program_id(1)
    qi = pl.program_id(2)
    ki = pl.program_id(3)

    @pl.when(ki == 0)
    def _():
        lane = lax.broadcasted_iota(jnp.int32, (tq, LANES), 1)
        for c in range(2):
            qc = q_ref[:, LANES * c:LANES * (c + 1)]
            qm_ref[c] = jnp.where(lane < ATT_HEAD_DIM, qc, jnp.zeros_like(qc))
        m_ref[...] = jnp.full(m_ref.shape, NEG_BIG, F32)
        l_ref[...] = jnp.zeros(l_ref.shape, F32)
        acc_ref[...] = jnp.zeros(acc_ref.shape, F32)

    kpos = ki * tk + lax.broadcasted_iota(jnp.int32, (tk, tq), 0)
    qpos = qi * tq + lax.broadcasted_iota(jnp.int32, (tk, tq), 1)
    bias = sc_ref[h] * jnp.abs(kpos - qpos).astype(F32)
    vt = vt_ref[...]
    for c in range(2):
        kc = k_ref[:, LANES * c:LANES * (c + 1)]
        s = lax.dot_general(kc, qm_ref[c], _NT, preferred_element_type=F32) - bias
        m_prev = m_ref[c]
        m_new = jnp.maximum(m_prev, jnp.max(s, axis=0, keepdims=True))
        alpha = jnp.exp2(m_prev - m_new)
        p = jnp.exp2(s - m_new)
        l_ref[c] = alpha * l_ref[c] + jnp.sum(p, axis=0, keepdims=True)
        acc_ref[c] = alpha * acc_ref[c] + jnp.dot(vt, p.astype(BF16), preferred_element_type=F32)
        m_ref[c] = m_new

    @pl.when(ki == nk - 1)
    def _():
        _attn_finish(sc_ref, subg_ref, o_ref, l_ref[0], l_ref[1], acc_ref)


def _attention(q, k, vt, scal, subg, fast_ok, b, t, tq, tk):
    nq, nk = t // tq, t // tk
    blk = 2 * LANES
    common = dict(
        grid=(b, ATT_HEADS, nq, nk),
        out_specs=pl.BlockSpec((None, tq, LANES), lambda bi, h, qi, ki: (bi, qi, h)),
        out_shape=jax.ShapeDtypeStruct((b, t, ATT_WIDTH), BF16),
        compiler_params=_params(("parallel", "parallel", "parallel", "arbitrary")),
    )
    qkv_specs = [
        pl.BlockSpec((None, tq, blk), lambda bi, h, qi, ki: (bi, qi, h)),
        pl.BlockSpec((None, tk, blk), lambda bi, h, qi, ki: (bi, ki, h)),
        pl.BlockSpec((LANES, tk), lambda bi, h, qi, ki: (h, bi * nk + ki)),
    ]
    smem = pl.BlockSpec(memory_space=pltpu.SMEM)
    idx = np.arange(tk)[:, None] - np.arange(tq)[None, :]
    absd = jnp.asarray(np.abs(idx), dtype=F32)
    slopes2 = np.exp2(-8.0 * (np.arange(ATT_HEADS) + 1.0) / ATT_HEADS) * LOG2E
    window = tuple(int((EXP2_ZERO_ARG / s - 1.0) // tk) + 1 for s in slopes2)
    def fast_head(h, q, k, vt, scal, subg):
        reach = window[h]
        band = 2 * reach + 1 < nk
        nvisits = 2 * reach + 1 if band else nk
        group = min(ATT_TILES_PER_STEP, nvisits)
        nsteps = -(-nvisits // group)
        tile = lambda qi, visit: _attn_key_tile(qi, visit, nk, reach, band)[0]
        k_specs = [pl.BlockSpec((None, tk, blk), lambda bi, qi, st, g=g: (bi, tile(qi, group * st + g), h))
                   for g in range(group)]
        vt_specs = [pl.BlockSpec((LANES, tk), lambda bi, qi, st, g=g: (h, bi * nk + tile(qi, group * st + g)))
                    for g in range(group)]
        return pl.pallas_call(
            functools.partial(_attn_fast_kernel, tq=tq, tk=tk, nk=nk, head=h, reach=reach, band=band,
                              nvisits=nvisits, group=group),
            grid=(b, nq, nsteps),
            in_specs=[smem, pl.BlockSpec((None, tq, blk), lambda bi, qi, st: (bi, qi, h))] + k_specs + vt_specs
            + [_const_spec((tk, tq)), _const_spec((ATT_V_DIM, 1))],
            out_specs=pl.BlockSpec((None, tq, LANES), lambda bi, qi, st: (bi, qi, 0)),
            out_shape=jax.ShapeDtypeStruct((b, t, LANES), BF16),
            scratch_shapes=[
                pltpu.VMEM((6, tq, LANES), BF16),
                pltpu.VMEM((2, 8, tq), F32),
                pltpu.VMEM((2, ATT_V_DIM, tq), F32),
            ],
            compiler_params=_params(("parallel", "parallel", "arbitrary")),
            name="diff_attn_h%d" % h,
        )(scal, q, *([k] * group), *([vt] * group), absd, subg)

    def fast(q, k, vt, scal, subg):
        return tuple(fast_head(h, q, k, vt, scal, subg) for h in range(ATT_HEADS))

    def safe(q, k, vt, scal, subg):
        o = safe_call(q, k, vt, scal, subg)
        return tuple(o[:, :, LANES * h:LANES * (h + 1)] for h in range(ATT_HEADS))

    def safe_call(q, k, vt, scal, subg):
        return pl.pallas_call(
            functools.partial(_attn_safe_kernel, tq=tq, tk=tk, nk=nk),
            in_specs=[smem] + qkv_specs + [_const_spec((ATT_V_DIM, 1))],
            scratch_shapes=[
                pltpu.VMEM((2, tq, LANES), BF16),
                pltpu.VMEM((2, 1, tq), F32),
                pltpu.VMEM((2, 1, tq), F32),
                pltpu.VMEM((2, ATT_V_DIM, tq), F32),
            ],
            name="diff_attn_safe",
            **common,
        )(scal, q, k, vt, subg)

    return lax.cond(fast_ok, fast, safe, q, k, vt, scal, subg)


N_LEVELS = 6


def _hgrn_constants(tt):
    c = HG_CHUNK
    t = np.arange(tt)
    same_chunk = (t[:, None] // c) == (t[None, :] // c)
    mats, masks = [], []
    for j in range(N_LEVELS):
        m = 1 << j
        blk = t // m
        same_blk = blk[:, None] == blk[None, :]
        odd = (blk % 2) == 1
        incl = same_blk & (t[None, :] <= t[:, None])
        excl_rev = same_blk & (t[None, :] > t[:, None])
        mats.append(np.where(odd[:, None], incl, excl_rev))
        same_2m = (t[:, None] // (2 * m)) == (t[None, :] // (2 * m))
        masks.append((same_2m & odd[:, None] & (~odd)[None, :]).astype(np.float32))
    mats.append(same_chunk & (t[None, :] <= t[:, None]))
    mats.append(same_chunk & (t[None, :] > t[:, None]))
    mats = np.stack(mats).astype(np.float32)
    masks = np.stack(masks)
    mats_b = mats[:, ::-1, ::-1]
    masks_b = masks[:, ::-1, ::-1]
    mst = np.concatenate([mats.reshape(-1, tt), mats_b.reshape(-1, tt)], axis=0)
    amask = np.concatenate([masks, masks_b], axis=0)
    return jnp.asarray(mst, dtype=BF16), jnp.asarray(amask, dtype=F32)


def _hgrn_kernel(hpf_ref, hpb_ref, lgf_ref, lgb_ref, mst_ref, amask_ref, g256_ref, of_ref, ob_ref, s_ref, *, tt):
    i = pl.program_id(1)
    nlev = N_LEVELS
    nmat = N_LEVELS + 2
    nchunk = tt // HG_CHUNK

    @pl.when(i == 0)
    def _():
        s_ref[...] = jnp.zeros(s_ref.shape, F32)

    lane256 = lax.broadcasted_iota(jnp.int32, (1, 256), 1)
    lane128 = lax.broadcasted_iota(jnp.int32, (1, LANES), 1)
    r128 = lax.broadcasted_iota(jnp.int32, (LANES, LANES), 0)
    c128 = lax.broadcasted_iota(jnp.int32, (LANES, LANES), 1)
    bdmask = ((r128 // HG_HEAD_DIM) == (c128 // HG_HEAD_DIM)).astype(F32)

    def direction(d, hp_ref, lg_ref, o_ref):
        qs = hp_ref[:, 0:256].astype(F32)
        v = hp_ref[:, 256:512]
        kk = hp_ref[:, 512 + 256 * d:768 + 256 * d].astype(F32)
        eall = jnp.dot(mst_ref[d * nmat * tt:(d + 1) * nmat * tt, :], lg_ref[...].astype(BF16),
                       preferred_element_type=F32)

        def expo(j):
            return jnp.exp(eall[j * tt:(j + 1) * tt])

        a = [None] * 4
        for lev in range(nlev):
            xx = expo(lev)
            qt, kt = (qs * xx).astype(BF16), (kk * xx).astype(BF16)
            am = amask_ref[d * nlev + lev]
            for pr in range(2):
                qp = qt[:, LANES * pr:LANES * (pr + 1)]
                kp = kt[:, LANES * pr:LANES * (pr + 1)]
                for hh in range(2):
                    sel = (lane128 // HG_HEAD_DIM) == hh
                    qm = jnp.where(sel, qp, jnp.zeros_like(qp))
                    p = lax.dot_general(qm, kp, _NT, preferred_element_type=F32) * am
                    idx = 2 * pr + hh
                    a[idx] = p if a[idx] is None else a[idx] + p

        o = jnp.dot((qs * kk).astype(BF16), g256_ref[...], preferred_element_type=F32) * v.astype(F32)
        for idx in range(4):
            vm = jnp.where((lane256 // HG_HEAD_DIM) == idx, v, jnp.zeros_like(v))
            contrib = jnp.dot(a[idx].astype(BF16), vm, preferred_element_type=F32)
            o = o + contrib
        o_ref[...] = o

        xq = expo(N_LEVELS)
        xk = expo(N_LEVELS + 1)
        qc = (qs * xq).astype(BF16)
        kc = (kk * xk).astype(BF16)
        order = range(nchunk) if d == 0 else range(nchunk - 1, -1, -1)
        for c in order:
            r0 = c * HG_CHUNK
            rows = slice(r0, r0 + HG_CHUNK)
            drow = r0 + HG_CHUNK - 1 if d == 0 else r0
            for pr in range(2):
                cols = slice(LANES * pr, LANES * (pr + 1))
                st = s_ref[d, pr]
                inter = lax.dot_general(qc[rows, cols], st.astype(BF16), _NT, preferred_element_type=F32)
                o_ref[rows, cols] += inter
                ut = lax.dot_general(v[rows, cols], kc[rows, cols], _TN, preferred_element_type=F32)
                dec = xq[drow:drow + 1, cols]
                s_ref[d, pr] = st * dec + ut * bdmask

    direction(0, hpf_ref, lgf_ref, of_ref)
    direction(1, hpb_ref, lgb_ref, ob_ref)


def _hgrn(hp, lg, b, t, tt):
    nt = t // tt
    mst, amask = _hgrn_constants(tt)
    kern = functools.partial(_hgrn_kernel, tt=tt)
    fwd = lambda w, cb: pl.BlockSpec((None, tt, w), lambda bi, i: (bi, i, cb))
    bwd = lambda w, cb: pl.BlockSpec((None, tt, w), lambda bi, i: (bi, nt - 1 - i, cb))
    return pl.pallas_call(
        kern,
        grid=(b, nt),
        in_specs=[fwd(1024, 0), bwd(1024, 0), fwd(256, 0), bwd(256, 1),
                  _const_spec(mst.shape), _const_spec(amask.shape), _const_spec((256, 256))],
        out_specs=(fwd(256, 0), bwd(256, 0)),
        out_shape=(jax.ShapeDtypeStruct((b, t, HG_WIDTH), F32), jax.ShapeDtypeStruct((b, t, HG_WIDTH), F32)),
        scratch_shapes=[pltpu.VMEM((2, 2, LANES, LANES), F32)],
        compiler_params=_params(("parallel", "arbitrary")),
        name="hgrn2",
    )(hp, hp, lg, lg, mst, amask, _block_ones(256, HG_HEAD_DIM))


def _merge_kernel(*refs, tm, seq, with_router):
    oa_refs = refs[:ATT_HEADS]
    refs = refs[ATT_HEADS:]
    (x_ref, cv_ref, cvp_ref, cvn_ref, of_ref, ob_ref, sg_ref, gate_ref, cw_ref, hgn_ref, g256_ref,
     wa_ref, wb_ref, wc_ref, wo_ref, g2_ref) = refs[:16]
    if with_router:
        wr_ref, tri_ref, xo_ref, h2_ref, lo_ref = refs[16:]
    else:
        xo_ref, h2_ref = refs[16:]
    i = pl.program_id(0)
    tile_start = (i * tm) % seq

    u = cv_ref[:, 256:512].astype(F32)
    row = lax.broadcasted_iota(jnp.int32, (tm, 1), 0)
    prev_row = jnp.where(tile_start == 0, 0.0, cvp_ref[7:8, 256:512].astype(F32))
    next_row = jnp.where(tile_start + tm == seq, 0.0, cvn_ref[0:1, 256:512].astype(F32))
    u_m1 = jnp.where(row == 0, prev_row, pltpu.roll(u, 1, axis=0))
    u_p1 = jnp.where(row == tm - 1, next_row, pltpu.roll(u, tm - 1, axis=0))
    conv = cw_ref[0:1, :] * u_m1 + cw_ref[1:2, :] * u + cw_ref[2:3, :] * u_p1
    yb_in = (cv_ref[:, 0:256].astype(F32) * conv).astype(BF16)

    o = of_ref[...] + ob_ref[...]
    ss = _group_sum(o * o, g256_ref[...])
    yc_in = (o * lax.rsqrt(ss * (1.0 / HG_HEAD_DIM) + EPS) * hgn_ref[...] * sg_ref[...].astype(F32)).astype(BF16)

    oa = jnp.concatenate([r[...] for r in oa_refs], axis=1)
    ya = jnp.dot(oa, wa_ref[...], preferred_element_type=F32)
    yb = jnp.dot(yb_in, wb_ref[...], preferred_element_type=F32)
    yc = jnp.dot(yc_in, wc_ref[...], preferred_element_type=F32)
    merged = (gate_ref[:, 0:1024].astype(F32) * ya + gate_ref[:, 1024:2048].astype(F32) * yb
              + gate_ref[:, 2048:3072].astype(F32) * yc)
    xn = x_ref[...] + jnp.dot(merged.astype(BF16), wo_ref[...], preferred_element_type=F32)
    xo_ref[...] = xn

    ms = jnp.mean(xn * xn, axis=-1, keepdims=True)
    h2 = xn * lax.rsqrt(ms + EPS) * g2_ref[...]
    h_hi = h2.astype(BF16)
    h2_ref[...] = h_hi
    if with_router:
        h_lo = (h2 - h_hi.astype(F32)).astype(BF16)
        t1 = lax.dot_general(wr_ref[0:16, :], h_hi, _NT, preferred_element_type=F32)
        t2 = lax.dot_general(wr_ref[16:32, :], h_lo, _NT, preferred_element_type=F32)
        lg = t1[0:N_EXPERTS] + t1[N_EXPERTS:2 * N_EXPERTS] + t2[0:N_EXPERTS]

        sub = lax.broadcasted_iota(jnp.int32, lg.shape, 0)
        m1 = jnp.max(lg, axis=0, keepdims=True)
        i1 = jnp.min(jnp.where(lg == m1, sub, N_EXPERTS), axis=0, keepdims=True)
        lg2 = jnp.where(sub == i1, NEG_BIG, lg)
        m2 = jnp.max(lg2, axis=0, keepdims=True)
        i2 = jnp.min(jnp.where(lg2 == m2, sub, N_EXPERTS), axis=0, keepdims=True)
        e2 = jnp.exp(m2 - m1)
        weights = jnp.where(sub == i1, 1.0 / (1.0 + e2), 0.0) + jnp.where(sub == i2, e2 / (1.0 + e2), 0.0)
        routed = (sub == i1) | (sub == i2)
        routed_f = routed.astype(F32)
        nblk = tm // LANES
        by_block = jnp.concatenate([routed_f[:, LANES * j:LANES * (j + 1)] for j in range(nblk)], axis=0)
        within = jnp.dot(by_block.astype(BF16), tri_ref[...], preferred_element_type=F32)
        before = jnp.zeros((N_EXPERTS, LANES), F32)
        ranks = []
        for j in range(nblk):
            wj = within[N_EXPERTS * j:N_EXPERTS * (j + 1)]
            ranks.append(wj + before - 1.0)
            before = before + wj[:, LANES - 1:LANES]
        rank = jnp.concatenate(ranks, axis=1)
        lo_ref[0:N_EXPERTS, :] = jnp.where(routed, rank, -1.0)
        lo_ref[N_EXPERTS:2 * N_EXPERTS, :] = weights


def _merge(x, oa, cv, of, ob, sg, gates, conv_w, hg_norm, wa, wb, wc, wo, g2, w_router, seq, tm):
    n = x.shape[0]
    nb8 = n // 8
    step8 = tm // 8
    row = lambda w: pl.BlockSpec((tm, w), lambda i: (i, 0))
    in_specs = [
        row(LANES), row(LANES), row(LANES), row(LANES), row(D_MODEL), row(512),
        pl.BlockSpec((8, 512), lambda i: (jnp.maximum(i * step8 - 1, 0), 0)),
        pl.BlockSpec((8, 512), lambda i: (jnp.minimum((i + 1) * step8, nb8 - 1), 0)),
        row(256), row(256), row(256), row(3072),
        _const_spec((8, 256)), _const_spec((1, 256)), _const_spec((256, 256)),
        _const_spec(wa.shape), _const_spec(wb.shape), _const_spec(wc.shape), _const_spec(wo.shape),
        _const_spec((1, D_MODEL)),
    ]
    args = list(oa) + [x, cv, cv, cv, of, ob, sg, gates, conv_w, hg_norm, _block_ones(256, HG_HEAD_DIM),
            wa, wb, wc, wo, g2]
    out_specs = [row(D_MODEL), row(D_MODEL)]
    out_shape = [jax.ShapeDtypeStruct((n, D_MODEL), F32), jax.ShapeDtypeStruct((n, D_MODEL), BF16)]
    with_router = w_router is not None
    if with_router:
        lane = np.arange(LANES)
        tri = jnp.asarray((lane[:, None] <= lane[None, :]).astype(np.float32), dtype=BF16)
        in_specs += [_const_spec(w_router.shape), _const_spec((LANES, LANES))]
        args += [w_router, tri]
        out_specs.append(pl.BlockSpec((2 * N_EXPERTS, tm), lambda i: (0, i)))
        out_shape.append(jax.ShapeDtypeStruct((2 * N_EXPERTS, n), F32))
    kern = functools.partial(_merge_kernel, tm=tm, seq=seq, with_router=with_router)
    return pl.pallas_call(
        kern,
        grid=(n // tm,),
        in_specs=in_specs,
        out_specs=tuple(out_specs),
        out_shape=tuple(out_shape),
        compiler_params=_params(("parallel",)),
        name="merge_router" if with_router else "merge",
    )(*args)


def _ffn_kernel(x_ref, h_ref, wg_ref, wu_ref, wd_ref, o_ref):
    h = h_ref[...]
    g = jnp.dot(h, wg_ref[...], preferred_element_type=F32)
    u = jnp.dot(h, wu_ref[...], preferred_element_type=F32)
    act = (jax.nn.silu(g) * u).astype(BF16)
    o_ref[...] = x_ref[...] + jnp.dot(act, wd_ref[...], preferred_element_type=F32)


def _ffn(x, h2, wg, wu, wd, tm):
    n = x.shape[0]
    row = lambda: pl.BlockSpec((tm, D_MODEL), lambda i: (i, 0))
    return pl.pallas_call(
        _ffn_kernel,
        grid=(n // tm,),
        in_specs=[row(), row(), _const_spec(wg.shape), _const_spec(wu.shape), _const_spec(wd.shape)],
        out_specs=row(),
        out_shape=jax.ShapeDtypeStruct((n, D_MODEL), F32),
        compiler_params=_params(("parallel",)),
        name="ffn_dense",
    )(x, h2, wg, wu, wd)


MOE_ROWS = 144


def _moe_expert_kernel(rt_ref, h_ref, y_ref, wg_ref, wu_ref, wd_ref, o_ref, *, expert, tm):
    rank = rt_ref[expert:expert + 1, :].astype(jnp.int32)
    w_row = rt_ref[N_EXPERTS + expert:N_EXPERTS + expert + 1, :]
    count = jnp.max(rank) + 1

    h = h_ref[...]

    def expert_rows(blk):
        rows = blk * MOE_ROWS + lax.broadcasted_iota(jnp.int32, (MOE_ROWS, tm), 0)
        sel = rank == rows
        sel_b = sel.astype(F32).astype(BF16)
        xe = jnp.dot(sel_b, h, preferred_element_type=F32).astype(BF16)
        g = jnp.dot(xe, wg_ref[...], preferred_element_type=F32)
        u = jnp.dot(xe, wu_ref[...], preferred_element_type=F32)
        act = (jax.nn.silu(g) * u).astype(BF16)
        ye = jnp.dot(act, wd_ref[...], preferred_element_type=F32)
        w_sel = jnp.sum(jnp.where(sel, w_row, 0.0), axis=1, keepdims=True)
        ye = (ye * w_sel).astype(BF16)
        return lax.dot_general(sel_b, ye, _TN, preferred_element_type=F32)

    o_ref[...] = y_ref[...] + expert_rows(0)

    def body(blk, carry):
        o_ref[...] += expert_rows(blk)
        return carry

    lax.fori_loop(1, (count + MOE_ROWS - 1) // MOE_ROWS, body, 0)


def _moe(x, h2, routing, wg, wu, wd, tm):
    n = x.shape[0]
    row = lambda: pl.BlockSpec((tm, D_MODEL), lambda i: (i, 0))
    expert_w = lambda w, e: pl.BlockSpec((None,) + w.shape[1:], lambda i: (e, 0, 0), pipeline_mode=pl.Buffered(1))
    y = x
    for e in range(N_EXPERTS):
        y = pl.pallas_call(
            functools.partial(_moe_expert_kernel, expert=e, tm=tm),
            grid=(n // tm,),
            in_specs=[pl.BlockSpec((2 * N_EXPERTS, tm), lambda i: (0, i)), row(), row(),
                      expert_w(wg, e), expert_w(wu, e), expert_w(wd, e)],
            out_specs=row(),
            out_shape=jax.ShapeDtypeStruct((n, D_MODEL), F32),
            input_output_aliases={2: 0},
            compiler_params=_params(("parallel",)),
            name="moe_expert",
        )(routing, h2, y, wg, wu, wd)
    return y


def _pick(limit, total):
    return min(limit, total)


def kernel(x_prompt, x_sample, norm_mix, w_in, q_norm, k_norm, lambda_q1, lambda_k1, lambda_q2, lambda_k2, sub_norm, conv_w, hg_lower, hg_norm, w_up_a, w_up_b, w_up_c, w_out, norm_ffn, w_gate_dense, w_up_dense, w_down_dense, w_router, w_gate_moe, w_up_moe, w_down_moe):
    depth = norm_mix.shape[0]
    lb_all = jnp.cumsum(jax.nn.softmax(hg_lower.astype(F32), axis=0), axis=0)
    lb_all = lb_all - lb_all[0]
    slopes = jnp.exp2(-8.0 * (jnp.arange(ATT_HEADS, dtype=F32) + 1.0) / ATT_HEADS)

    layers = []
    for l in range(depth):
        lam_init = 0.8 - 0.6 * math.exp(-0.3 * l)
        lam = (jnp.exp(jnp.sum(lambda_q1[l].astype(F32) * lambda_k1[l].astype(F32)))
               - jnp.exp(jnp.sum(lambda_q2[l].astype(F32) * lambda_k2[l].astype(F32))) + lam_init)
        wl = w_in[l]
        qtab, ktab, fast_ok = _alibi_tables(q_norm[l].astype(F32), k_norm[l].astype(F32))
        p = dict(
            gain=norm_mix[l].reshape(1, D_MODEL),
            w_main=wl[:, 1536:].astype(BF16),
            w_vt=wl[:, 1024:1536].T.astype(BF16),
            w_qk=wl[:, 0:2 * ATT_QK_WIDTH].astype(BF16),
            qtab=qtab, ktab=ktab, fast_ok=fast_ok,
            lb=lb_all[l],
            scal=jnp.concatenate([slopes * LOG2E, lam.reshape(1), jnp.zeros((3,), F32)]),
            subg=(sub_norm[l] * (1.0 - lam_init)).reshape(ATT_V_DIM, 1),
            conv_w=jnp.concatenate([conv_w[l], jnp.zeros((5, SC_WIDTH), F32)], axis=0),
            hg_norm=hg_norm[l].reshape(1, HG_WIDTH),
            wa=w_up_a[l].astype(BF16), wb=w_up_b[l].astype(BF16), wc=w_up_c[l].astype(BF16),
            wo=w_out[l].astype(BF16),
            g2=norm_ffn[l].reshape(1, D_MODEL),
        )
        j = l // 2
        if l % 2 == 0:
            p.update(wg=w_gate_dense[j].astype(BF16), wu=w_up_dense[j].astype(BF16),
                     wd=w_down_dense[j].astype(BF16))
        else:
            wr_hi, wr_lo = _split_bf16(w_router[j].T.astype(F32))
            p.update(w_router=jnp.concatenate([wr_hi, wr_lo, wr_hi, jnp.zeros_like(wr_hi)], axis=0),
                     wg=w_gate_moe[j].astype(BF16), wu=w_up_moe[j].astype(BF16), wd=w_down_moe[j].astype(BF16))
        layers.append(p)

    def trunk(x3):
        b, t, _ = x3.shape
        n = b * t
        tm = _pick(512, t)
        tq = _pick(1024, t)
        tr = _pick(1024, t)
        tt = _pick(256, t)
        x = x3.reshape(n, D_MODEL)
        for l, p in enumerate(layers):
            q, k, vt, cv, hp, lg, sg, gates = _inproj(x, p["gain"], p["w_main"], p["w_vt"], p["w_qk"],
                                                      p["qtab"], p["ktab"], p["lb"], t, tm)
            oa = _attention(q.reshape(b, t, 1024), k.reshape(b, t, 1024), vt, p["scal"], p["subg"], p["fast_ok"],
                            b, t, tq, tq)
            of, ob = _hgrn(hp.reshape(b, t, 1024), lg.reshape(b, t, 512), b, t, tt)
            res = _merge(x, [o.reshape(n, LANES) for o in oa], cv, of.reshape(n, 256), ob.reshape(n, 256), sg, gates,
                         p["conv_w"], p["hg_norm"], p["wa"], p["wb"], p["wc"], p["wo"], p["g2"],
                         p.get("w_router"), t, tr)
            if l % 2 == 0:
                xn, h2 = res
                x = _ffn(xn, h2, p["wg"], p["wu"], p["wd"], tm)
            else:
                xn, h2, routing = res
                x = _moe(xn, h2, routing, p["wg"], p["wu"], p["wd"], tr)
        return x.reshape(b, t, D_MODEL)

    return (trunk(x_prompt), trunk(x_sample))
```

```python
import functools
import math

import numpy as np
import jax
import jax.numpy as jnp
from jax import lax
from jax.experimental import pallas as pl
from jax.experimental.pallas import tpu as pltpu

F32 = jnp.float32
BF16 = jnp.bfloat16

D_MODEL = 1024
ATT_HEADS = 4
ATT_HEAD_DIM = 64
ATT_V_DIM = 128
ATT_QK_WIDTH = 512
ATT_WIDTH = 512
SC_WIDTH = 256
HG_WIDTH = 256
HG_HEAD_DIM = 64
HG_CHUNK = 64
N_BRANCH = 3
N_EXPERTS = 8
EPS = 1e-6

LOG2E = math.log2(math.e)
N_SLOPE_PIECES = 3
SCORE_BOUND_LIMIT = 48.0
EXP2_ZERO_ARG = 151.0
ATT_TILES_PER_STEP = 2

LANES = 128
POS_SHIFT = 7
VMEM_LIMIT = 56 * 1024 * 1024
NEG_BIG = -1e30

_NT = (((1,), (1,)), ((), ()))
_TN = (((0,), (0,)), ((), ()))


def _const_spec(shape):
    nd = len(shape)
    return pl.BlockSpec(shape, lambda *_: (0,) * nd, pipeline_mode=pl.Buffered(1))


def _params(sem):
    return pltpu.CompilerParams(dimension_semantics=sem, vmem_limit_bytes=VMEM_LIMIT)


def _split_bf16(x):
    hi = x.astype(BF16)
    lo = (x - hi.astype(F32)).astype(BF16)
    return hi, lo


def _group_sum(x, gmat):
    hi, lo = _split_bf16(x)
    return (jnp.dot(hi, gmat, preferred_element_type=F32) + jnp.dot(lo, gmat, preferred_element_type=F32))


def _block_ones(width, group):
    idx = np.arange(width) // group
    return jnp.asarray((idx[:, None] == idx[None, :]).astype(np.float32), dtype=BF16)


def _inproj_kernel(x_ref, g_ref, wm_ref, wvt_ref, wqk_ref, qtab_ref, ktab_ref, lb_ref,
                   q_ref, k_ref, vt_ref, cv_ref, hp_ref, lg_ref, sg_ref, gate_ref, *, tm, seq):
    x = x_ref[...]
    ms = jnp.mean(x * x, axis=-1, keepdims=True)
    h = (x * lax.rsqrt(ms + EPS) * g_ref[...]).astype(BF16)

    vt_ref[...] = lax.dot_general(wvt_ref[...], h, _NT, preferred_element_type=F32).astype(BF16)

    pos = (pl.program_id(0) * tm) % seq + lax.broadcasted_iota(jnp.int32, (tm, 1), 0)
    pos_lo = (pos & (LANES - 1)).astype(F32)
    pos_hi = (pos >> POS_SHIFT).astype(F32)

    first_half = lax.broadcasted_iota(jnp.int32, (1, LANES), 1) < ATT_HEAD_DIM

    def head_norm(raw, tab_ref, out_ref):
        for hd in range(ATT_HEADS):
            slab = raw[:, LANES * hd:LANES * (hd + 1)]
            sq = slab * slab
            ss_all = jnp.sum(sq, axis=-1, keepdims=True)
            ss_0 = jnp.sum(jnp.where(first_half, sq, 0.0), axis=-1, keepdims=True)
            ss = jnp.where(first_half, ss_0, ss_all - ss_0)
            normed = slab * lax.rsqrt(ss * (1.0 / ATT_HEAD_DIM) + EPS)
            for c in range(2):
                cols = slice(LANES * (2 * hd + c), LANES * (2 * hd + c + 1))
                data = normed if c == 0 else pltpu.roll(normed, ATT_HEAD_DIM, axis=1)
                aux = tab_ref[1:2, cols] + tab_ref[2:3, cols] * pos_lo + tab_ref[3:4, cols] * pos_hi
                out_ref[:, cols] = jnp.where(first_half, data * tab_ref[0:1, cols], aux).astype(BF16)

    qk = jnp.dot(h, wqk_ref[...], preferred_element_type=F32)
    head_norm(qk[:, 0:ATT_QK_WIDTH], qtab_ref, q_ref)
    head_norm(qk[:, ATT_QK_WIDTH:2 * ATT_QK_WIDTH], ktab_ref, k_ref)

    sc = jnp.dot(h, wm_ref[:, 0:768], preferred_element_type=F32)
    cv_ref[:, 0:256] = sc[:, 0:256].astype(BF16)
    cv_ref[:, 256:512] = (sc[:, 256:512] * sc[:, 512:768]).astype(BF16)

    hg = jnp.dot(h, wm_ref[:, 768:2048], preferred_element_type=F32)
    hp_ref[:, 0:256] = jax.nn.silu(hg[:, 0:256]).astype(BF16)
    hp_ref[:, 256:512] = hg[:, 256:512].astype(BF16)
    for d in range(2):
        z = hg[:, 512 + 256 * d:768 + 256 * d]
        lb = lb_ref[d:d + 1, :]
        f = lb + (1.0 - lb) * jax.nn.sigmoid(z)
        hp_ref[:, 512 + 256 * d:768 + 256 * d] = ((1.0 - lb) * jax.nn.sigmoid(-z)).astype(BF16)
        lg_ref[:, 256 * d:256 * d + 256] = jnp.log(f)
    sg_ref[...] = jax.nn.silu(hg[:, 1024:1280]).astype(BF16)

    gate_ref[...] = jax.nn.sigmoid(
        jnp.dot(h, wm_ref[:, 2048:5120], preferred_element_type=F32)).astype(BF16)


def _inproj(x, gain, w_main, w_vt, w_qk, qtab, ktab, lb, seq, tm):
    n = x.shape[0]
    row = lambda w: pl.BlockSpec((tm, w), lambda i: (i, 0))
    out_shape = (
        jax.ShapeDtypeStruct((n, 1024), BF16),
        jax.ShapeDtypeStruct((n, 1024), BF16),
        jax.ShapeDtypeStruct((512, n), BF16),
        jax.ShapeDtypeStruct((n, 512), BF16),
        jax.ShapeDtypeStruct((n, 1024), BF16),
        jax.ShapeDtypeStruct((n, 512), F32),
        jax.ShapeDtypeStruct((n, 256), BF16),
        jax.ShapeDtypeStruct((n, 3072), BF16),
    )
    kern = functools.partial(_inproj_kernel, tm=tm, seq=seq)
    return pl.pallas_call(
        kern,
        grid=(n // tm,),
        in_specs=[row(D_MODEL), _const_spec((1, D_MODEL)), _const_spec(w_main.shape), _const_spec(w_vt.shape),
                  _const_spec(w_qk.shape), _const_spec((8, 1024)), _const_spec((8, 1024)),
                  _const_spec((2, 256))],
        out_specs=(row(1024), row(1024), pl.BlockSpec((512, tm), lambda i: (0, i)), row(512), row(1024),
                   row(512), row(256), row(3072)),
        out_shape=out_shape,
        compiler_params=_params(("parallel",)),
        name="inproj",
    )(x, gain, w_main, w_vt, w_qk, qtab, ktab, lb)


def _alibi_tables(q_gain, k_gain):
    slopes = np.exp2(-8.0 * (np.arange(ATT_HEADS, dtype=np.float64) + 1.0) / ATT_HEADS) * LOG2E
    qc = np.zeros((8, 1024), np.float32)
    kc = np.zeros((8, 1024), np.float32)
    for h in range(ATT_HEADS):
        rest = np.float32(slopes[h])
        pieces = []
        for _ in range(N_SLOPE_PIECES):
            piece = np.asarray(rest, dtype=BF16).astype(np.float32)
            pieces.append(float(piece))
            rest = np.float32(rest - piece)
        for c in range(2):
            base = LANES * (2 * h + c) + ATT_HEAD_DIM
            for i, a in enumerate(pieces):
                qc[1, base + 2 * i] = -a
                qc[1, base + 2 * i + 1] = -a * LANES
                kc[2, base + 2 * i] = 1.0
                kc[3, base + 2 * i + 1] = 1.0
                off = base + 2 * N_SLOPE_PIECES
                qc[2, off + 2 * i] = 1.0
                qc[3, off + 2 * i + 1] = 1.0
                kc[1, off + 2 * i] = a
                kc[1, off + 2 * i + 1] = a * LANES
    lane = np.arange(1024) % LANES
    lane_is_data = lane < ATT_HEAD_DIM
    qg = jnp.where(lane_is_data, jnp.tile(q_gain, 16) * (LOG2E * ATT_HEAD_DIM ** -0.5), 0.0)
    kg = jnp.where(lane_is_data, jnp.tile(k_gain, 16), 0.0)
    bound = 1.02 * LOG2E * ATT_HEAD_DIM ** 0.5 * jnp.max(jnp.abs(q_gain)) * jnp.max(jnp.abs(k_gain))
    qconst = jnp.asarray(qc[1])
    rest = bound.astype(F32)
    bound_lane0 = ATT_HEAD_DIM + 4 * N_SLOPE_PIECES
    for i in range(N_SLOPE_PIECES):
        piece = rest.astype(BF16).astype(F32)
        rest = rest - piece
        qconst = jnp.where(lane == bound_lane0 + i, -piece, qconst)
        kc[1, lane == bound_lane0 + i] = 1.0
    qtab = jnp.asarray(qc).at[0].set(qg).at[1].set(qconst)
    ktab = jnp.asarray(kc).at[0].set(kg)
    return qtab, ktab, bound <= SCORE_BOUND_LIMIT


def _attn_finish(sc_ref, subg_ref, o_ref, l0, l1, acc_ref):
    lam = sc_ref[ATT_HEADS]
    o = acc_ref[0] / l0 - lam * (acc_ref[1] / l1)
    ms = jnp.mean(o * o, axis=0, keepdims=True)
    o = o * lax.rsqrt(ms + EPS) * subg_ref[...]
    o_ref[...] = o.T.astype(BF16)


def _attn_key_tile(qi, st, nk, reach, band):
    if band:
        off = jnp.where(st <= reach, st, reach - st)
        kb = qi + off
        return jnp.clip(kb, 0, nk - 1), off > 0, (kb >= 0) & (kb < nk)
    return (qi + st) % nk, qi + st < nk, st >= 0


def _attn_fast_kernel(sc_ref, q_ref, *refs, tq, tk, nk, head, reach, band, nvisits, group):
    k_refs, vt_refs = refs[:group], refs[group:2 * group]
    absd_ref, subg_ref, o_ref, qv_ref, l_ref, acc_ref = refs[2 * group:]
    h = head
    qi = pl.program_id(1)
    st = pl.program_id(2)
    nsteps = -(-nvisits // group)

    @pl.when(st == 0)
    def _():
        lane = lax.broadcasted_iota(jnp.int32, (tq, LANES), 1)
        alibi = (lane >= ATT_HEAD_DIM) & (lane < ATT_HEAD_DIM + 4 * N_SLOPE_PIECES)
        for c in range(2):
            qc = q_ref[:, LANES * c:LANES * (c + 1)]
            qv_ref[c] = qc
            qv_ref[2 + c] = jnp.where(alibi, -qc, qc)
            qv_ref[4 + c] = jnp.where(alibi, jnp.zeros_like(qc), qc)
        l_ref[...] = jnp.zeros(l_ref.shape, F32)
        acc_ref[...] = jnp.zeros(acc_ref.shape, F32)

    def step(diag, k_ref, vt_ref, after):
        vt = vt_ref[...]
        for c in range(2):
            kc = k_ref[:, LANES * c:LANES * (c + 1)]
            if diag:
                qsel = qv_ref[4 + c]
            else:
                qsel = qv_ref[jnp.where(after, 0, 2) + c]
            s = lax.dot_general(kc, qsel, _NT, preferred_element_type=F32)
            if diag:
                s = s - sc_ref[h] * absd_ref[...]
            p = jnp.exp2(s)
            l_ref[c] += jnp.sum(p.reshape(tk // 8, 8, tq), axis=0)
            acc_ref[c] += jnp.dot(vt, p.astype(BF16), preferred_element_type=F32)

    for g in range(group):
        visit = group * st + g
        _, after, exists = _attn_key_tile(qi, visit, nk, reach, band)
        exists = exists & (visit < nvisits)
        if g == 0:
            pl.when(st == 0)(functools.partial(step, True, k_refs[0], vt_refs[0], after))
            exists = exists & (st != 0)
        pl.when(exists)(functools.partial(step, False, k_refs[g], vt_refs[g], after))

    @pl.when(st == nsteps - 1)
    def _():
        _attn_finish(sc_ref, subg_ref, o_ref, jnp.sum(l_ref[0], axis=0, keepdims=True),
                     jnp.sum(l_ref[1], axis=0, keepdims=True), acc_ref)


def _attn_safe_kernel(sc_ref, q_ref, k_ref, vt_ref, subg_ref, o_ref, qm_ref, m_ref, l_ref, acc_ref, *, tq, tk, nk):
    h = pl.program_id(1)
    qi = pl.program_id(2)
    ki = pl.program_id(3)

    @pl.when(ki == 0)
    def _():
        lane = lax.broadcasted_iota(jnp.int32, (tq, LANES), 1)
        for c in range(2):
            qc = q_ref[:, LANES * c:LANES * (c + 1)]
            qm_ref[c] = jnp.where(lane < ATT_HEAD_DIM, qc, jnp.zeros_like(qc))
        m_ref[...] = jnp.full(m_ref.shape, NEG_BIG, F32)
        l_ref[...] = jnp.zeros(l_ref.shape, F32)
        acc_ref[...] = jnp.zeros(acc_ref.shape, F32)

    kpos = ki * tk + lax.broadcasted_iota(jnp.int32, (tk, tq), 0)
    qpos = qi * tq + lax.broadcasted_iota(jnp.int32, (tk, tq), 1)
    bias = sc_ref[h] * jnp.abs(kpos - qpos).astype(F32)
    vt = vt_ref[...]
    for c in range(2):
        kc = k_ref[:, LANES * c:LANES * (c + 1)]
        s = lax.dot_general(kc, qm_ref[c], _NT, preferred_element_type=F32) - bias
        m_prev = m_ref[c]
        m_new = jnp.maximum(m_prev, jnp.max(s, axis=0, keepdims=True))
        alpha = jnp.exp2(m_prev - m_new)
        p = jnp.exp2(s - m_new)
        l_ref[c] = alpha * l_ref[c] + jnp.sum(p, axis=0, keepdims=True)
        acc_ref[c] = alpha * acc_ref[c] + jnp.dot(vt, p.astype(BF16), preferred_element_type=F32)
        m_ref[c] = m_new

    @pl.when(ki == nk - 1)
    def _():
        _attn_finish(sc_ref, subg_ref, o_ref, l_ref[0], l_ref[1], acc_ref)


def _attention(q, k, vt, scal, subg, fast_ok, b, t, tq, tk):
    nq, nk = t // tq, t // tk
    blk = 2 * LANES
    common = dict(
        grid=(b, ATT_HEADS, nq, nk),
        out_specs=pl.BlockSpec((None, tq, LANES), lambda bi, h, qi, ki: (bi, qi, h)),
        out_shape=jax.ShapeDtypeStruct((b, t, ATT_WIDTH), BF16),
        compiler_params=_params(("parallel", "parallel", "parallel", "arbitrary")),
    )
    qkv_specs = [
        pl.BlockSpec((None, tq, blk), lambda bi, h, qi, ki: (bi, qi, h)),
        pl.BlockSpec((None, tk, blk), lambda bi, h, qi, ki: (bi, ki, h)),
        pl.BlockSpec((LANES, tk), lambda bi, h, qi, ki: (h, bi * nk + ki)),
    ]
    smem = pl.BlockSpec(memory_space=pltpu.SMEM)
    idx = np.arange(tk)[:, None] - np.arange(tq)[None, :]
    absd = jnp.asarray(np.abs(idx), dtype=F32)
    slopes2 = np.exp2(-8.0 * (np.arange(ATT_HEADS) + 1.0) / ATT_HEADS) * LOG2E
    window = tuple(int((EXP2_ZERO_ARG / s - 1.0) // tk) + 1 for s in slopes2)
    def fast_head(h, q, k, vt, scal, subg):
        reach = window[h]
        band = 2 * reach + 1 < nk
        nvisits = 2 * reach + 1 if band else nk
        group = min(ATT_TILES_PER_STEP, nvisits)
        nsteps = -(-nvisits // group)
        tile = lambda qi, visit: _attn_key_tile(qi, visit, nk, reach, band)[0]
        k_specs = [pl.BlockSpec((None, tk, blk), lambda bi, qi, st, g=g: (bi, tile(qi, group * st + g), h))
                   for g in range(group)]
        vt_specs = [pl.BlockSpec((LANES, tk), lambda bi, qi, st, g=g: (h, bi * nk + tile(qi, group * st + g)))
                    for g in range(group)]
        return pl.pallas_call(
            functools.partial(_attn_fast_kernel, tq=tq, tk=tk, nk=nk, head=h, reach=reach, band=band,
                              nvisits=nvisits, group=group),
            grid=(b, nq, nsteps),
            in_specs=[smem, pl.BlockSpec((None, tq, blk), lambda bi, qi, st: (bi, qi, h))] + k_specs + vt_specs
            + [_const_spec((tk, tq)), _const_spec((ATT_V_DIM, 1))],
            out_specs=pl.BlockSpec((None, tq, LANES), lambda bi, qi, st: (bi, qi, 0)),
            out_shape=jax.ShapeDtypeStruct((b, t, LANES), BF16),
            scratch_shapes=[
                pltpu.VMEM((6, tq, LANES), BF16),
                pltpu.VMEM((2, 8, tq), F32),
                pltpu.VMEM((2, ATT_V_DIM, tq), F32),
            ],
            compiler_params=_params(("parallel", "parallel", "arbitrary")),
            name="diff_attn_h%d" % h,
        )(scal, q, *([k] * group), *([vt] * group), absd, subg)

    def fast(q, k, vt, scal, subg):
        return tuple(fast_head(h, q, k, vt, scal, subg) for h in range(ATT_HEADS))

    def safe(q, k, vt, scal, subg):
        o = safe_call(q, k, vt, scal, subg)
        return tuple(o[:, :, LANES * h:LANES * (h + 1)] for h in range(ATT_HEADS))

    def safe_call(q, k, vt, scal, subg):
        return pl.pallas_call(
            functools.partial(_attn_safe_kernel, tq=tq, tk=tk, nk=nk),
            in_specs=[smem] + qkv_specs + [_const_spec((ATT_V_DIM, 1))],
            scratch_shapes=[
                pltpu.VMEM((2, tq, LANES), BF16),
                pltpu.VMEM((2, 1, tq), F32),
                pltpu.VMEM((2, 1, tq), F32),
                pltpu.VMEM((2, ATT_V_DIM, tq), F32),
            ],
            name="diff_attn_safe",
            **common,
        )(scal, q, k, vt, subg)

    return lax.cond(fast_ok, fast, safe, q, k, vt, scal, subg)


N_LEVELS = 6


def _hgrn_constants(tt):
    c = HG_CHUNK
    t = np.arange(tt)
    same_chunk = (t[:, None] // c) == (t[None, :] // c)
    mats, masks = [], []
    for j in range(N_LEVELS):
        m = 1 << j
        blk = t // m
        same_blk = blk[:, None] == blk[None, :]
        odd = (blk % 2) == 1
        incl = same_blk & (t[None, :] <= t[:, None])
        excl_rev = same_blk & (t[None, :] > t[:, None])
        mats.append(np.where(odd[:, None], incl, excl_rev))
        same_2m = (t[:, None] // (2 * m)) == (t[None, :] // (2 * m))
        masks.append((same_2m & odd[:, None] & (~odd)[None, :]).astype(np.float32))
    mats.append(same_chunk & (t[None, :] <= t[:, None]))
    mats.append(same_chunk & (t[None, :] > t[:, None]))
    mats = np.stack(mats).astype(np.float32)
    masks = np.stack(masks)
    mats_b = mats[:, ::-1, ::-1]
    masks_b = masks[:, ::-1, ::-1]
    mst = np.concatenate([mats.reshape(-1, tt), mats_b.reshape(-1, tt)], axis=0)
    amask = np.concatenate([masks, masks_b], axis=0)
    return jnp.asarray(mst, dtype=BF16), jnp.asarray(amask, dtype=F32)


def _hgrn_kernel(hpf_ref, hpb_ref, lgf_ref, lgb_ref, mst_ref, amask_ref, g256_ref, of_ref, ob_ref, s_ref, *, tt):
    i = pl.program_id(1)
    nlev = N_LEVELS
    nmat = N_LEVELS + 2
    nchunk = tt // HG_CHUNK

    @pl.when(i == 0)
    def _():
        s_ref[...] = jnp.zeros(s_ref.shape, F32)

    lane256 = lax.broadcasted_iota(jnp.int32, (1, 256), 1)
    lane128 = lax.broadcasted_iota(jnp.int32, (1, LANES), 1)
    r128 = lax.broadcasted_iota(jnp.int32, (LANES, LANES), 0)
    c128 = lax.broadcasted_iota(jnp.int32, (LANES, LANES), 1)
    bdmask = ((r128 // HG_HEAD_DIM) == (c128 // HG_HEAD_DIM)).astype(F32)

    def direction(d, hp_ref, lg_ref, o_ref):
        qs = hp_ref[:, 0:256].astype(F32)
        v = hp_ref[:, 256:512]
        kk = hp_ref[:, 512 + 256 * d:768 + 256 * d].astype(F32)
        eall = jnp.dot(mst_ref[d * nmat * tt:(d + 1) * nmat * tt, :], lg_ref[...].astype(BF16),
                       preferred_element_type=F32)

        def expo(j):
            return jnp.exp(eall[j * tt:(j + 1) * tt])

        a = [None] * 4
        for lev in range(nlev):
            xx = expo(lev)
            qt, kt = (qs * xx).astype(BF16), (kk * xx).astype(BF16)
            am = amask_ref[d * nlev + lev]
            for pr in range(2):
                qp = qt[:, LANES * pr:LANES * (pr + 1)]
                kp = kt[:, LANES * pr:LANES * (pr + 1)]
                for hh in range(2):
                    sel = (lane128 // HG_HEAD_DIM) == hh
                    qm = jnp.where(sel, qp, jnp.zeros_like(qp))
                    p = lax.dot_general(qm, kp, _NT, preferred_element_type=F32) * am
                    idx = 2 * pr + hh
                    a[idx] = p if a[idx] is None else a[idx] + p

        o = jnp.dot((qs * kk).astype(BF16), g256_ref[...], preferred_element_type=F32) * v.astype(F32)
        for idx in range(4):
            vm = jnp.where((lane256 // HG_HEAD_DIM) == idx, v, jnp.zeros_like(v))
            contrib = jnp.dot(a[idx].astype(BF16), vm, preferred_element_type=F32)
            o = o + contrib
        o_ref[...] = o

        xq = expo(N_LEVELS)
        xk = expo(N_LEVELS + 1)
        qc = (qs * xq).astype(BF16)
        kc = (kk * xk).astype(BF16)
        order = range(nchunk) if d == 0 else range(nchunk - 1, -1, -1)
        for c in order:
            r0 = c * HG_CHUNK
            rows = slice(r0, r0 + HG_CHUNK)
            drow = r0 + HG_CHUNK - 1 if d == 0 else r0
            for pr in range(2):
                cols = slice(LANES * pr, LANES * (pr + 1))
                st = s_ref[d, pr]
                inter = lax.dot_general(qc[rows, cols], st.astype(BF16), _NT, preferred_element_type=F32)
                o_ref[rows, cols] += inter
                ut = lax.dot_general(v[rows, cols], kc[rows, cols], _TN, preferred_element_type=F32)
                dec = xq[drow:drow + 1, cols]
                s_ref[d, pr] = st * dec + ut * bdmask

    direction(0, hpf_ref, lgf_ref, of_ref)
    direction(1, hpb_ref, lgb_ref, ob_ref)


def _hgrn(hp, lg, b, t, tt):
    nt = t // tt
    mst, amask = _hgrn_constants(tt)
    kern = functools.partial(_hgrn_kernel, tt=tt)
    fwd = lambda w, cb: pl.BlockSpec((None, tt, w), lambda bi, i: (bi, i, cb))
    bwd = lambda w, cb: pl.BlockSpec((None, tt, w), lambda bi, i: (bi, nt - 1 - i, cb))
    return pl.pallas_call(
        kern,
        grid=(b, nt),
        in_specs=[fwd(1024, 0), bwd(1024, 0), fwd(256, 0), bwd(256, 1),
                  _const_spec(mst.shape), _const_spec(amask.shape), _const_spec((256, 256))],
        out_specs=(fwd(256, 0), bwd(256, 0)),
        out_shape=(jax.ShapeDtypeStruct((b, t, HG_WIDTH), F32), jax.ShapeDtypeStruct((b, t, HG_WIDTH), F32)),
        scratch_shapes=[pltpu.VMEM((2, 2, LANES, LANES), F32)],
        compiler_params=_params(("parallel", "arbitrary")),
        name="hgrn2",
    )(hp, hp, lg, lg, mst, amask, _block_ones(256, HG_HEAD_DIM))


def _merge_kernel(*refs, tm, seq, with_router):
    oa_refs = refs[:ATT_HEADS]
    refs = refs[ATT_HEADS:]
    (x_ref, cv_ref, cvp_ref, cvn_ref, of_ref, ob_ref, sg_ref, gate_ref, cw_ref, hgn_ref, g256_ref,
     wa_ref, wb_ref, wc_ref, wo_ref, g2_ref) = refs[:16]
    if with_router:
        wr_ref, tri_ref, xo_ref, h2_ref, lo_ref = refs[16:]
    else:
        xo_ref, h2_ref = refs[16:]
    i = pl.program_id(0)
    tile_start = (i * tm) % seq

    u = cv_ref[:, 256:512].astype(F32)
    row = lax.broadcasted_iota(jnp.int32, (tm, 1), 0)
    prev_row = jnp.where(tile_start == 0, 0.0, cvp_ref[7:8, 256:512].astype(F32))
    next_row = jnp.where(tile_start + tm == seq, 0.0, cvn_ref[0:1, 256:512].astype(F32))
    u_m1 = jnp.where(row == 0, prev_row, pltpu.roll(u, 1, axis=0))
    u_p1 = jnp.where(row == tm - 1, next_row, pltpu.roll(u, tm - 1, axis=0))
    conv = cw_ref[0:1, :] * u_m1 + cw_ref[1:2, :] * u + cw_ref[2:3, :] * u_p1
    yb_in = (cv_ref[:, 0:256].astype(F32) * conv).astype(BF16)

    o = of_ref[...] + ob_ref[...]
    ss = _group_sum(o * o, g256_ref[...])
    yc_in = (o * lax.rsqrt(ss * (1.0 / HG_HEAD_DIM) + EPS) * hgn_ref[...] * sg_ref[...].astype(F32)).astype(BF16)

    oa = jnp.concatenate([r[...] for r in oa_refs], axis=1)
    ya = jnp.dot(oa, wa_ref[...], preferred_element_type=F32)
    yb = jnp.dot(yb_in, wb_ref[...], preferred_element_type=F32)
    yc = jnp.dot(yc_in, wc_ref[...], preferred_element_type=F32)
    merged = (gate_ref[:, 0:1024].astype(F32) * ya + gate_ref[:, 1024:2048].astype(F32) * yb
              + gate_ref[:, 2048:3072].astype(F32) * yc)
    xn = x_ref[...] + jnp.dot(merged.astype(BF16), wo_ref[...], preferred_element_type=F32)
    xo_ref[...] = xn

    ms = jnp.mean(xn * xn, axis=-1, keepdims=True)
    h2 = xn * lax.rsqrt(ms + EPS) * g2_ref[...]
    h_hi = h2.astype(BF16)
    h2_ref[...] = h_hi
    if with_router:
        h_lo = (h2 - h_hi.astype(F32)).astype(BF16)
        t1 = lax.dot_general(wr_ref[0:16, :], h_hi, _NT, preferred_element_type=F32)
        t2 = lax.dot_general(wr_ref[16:32, :], h_lo, _NT, preferred_element_type=F32)
        lg = t1[0:N_EXPERTS] + t1[N_EXPERTS:2 * N_EXPERTS] + t2[0:N_EXPERTS]

        sub = lax.broadcasted_iota(jnp.int32, lg.shape, 0)
        m1 = jnp.max(lg, axis=0, keepdims=True)
        i1 = jnp.min(jnp.where(lg == m1, sub, N_EXPERTS), axis=0, keepdims=True)
        lg2 = jnp.where(sub == i1, NEG_BIG, lg)
        m2 = jnp.max(lg2, axis=0, keepdims=True)
        i2 = jnp.min(jnp.where(lg2 == m2, sub, N_EXPERTS), axis=0, keepdims=True)
        e2 = jnp.exp(m2 - m1)
        weights = jnp.where(sub == i1, 1.0 / (1.0 + e2), 0.0) + jnp.where(sub == i2, e2 / (1.0 + e2), 0.0)
        routed = (sub == i1) | (sub == i2)
        routed_f = routed.astype(F32)
        nblk = tm // LANES
        by_block = jnp.concatenate([routed_f[:, LANES * j:LANES * (j + 1)] for j in range(nblk)], axis=0)
        within = jnp.dot(by_block.astype(BF16), tri_ref[...], preferred_element_type=F32)
        before = jnp.zeros((N_EXPERTS, LANES), F32)
        ranks = []
        for j in range(nblk):
            wj = within[N_EXPERTS * j:N_EXPERTS * (j + 1)]
            ranks.append(wj + before - 1.0)
            before = before + wj[:, LANES - 1:LANES]
        rank = jnp.concatenate(ranks, axis=1)
        lo_ref[0:N_EXPERTS, :] = jnp.where(routed, rank, -1.0)
        lo_ref[N_EXPERTS:2 * N_EXPERTS, :] = weights


def _merge(x, oa, cv, of, ob, sg, gates, conv_w, hg_norm, wa, wb, wc, wo, g2, w_router, seq, tm):
    n = x.shape[0]
    nb8 = n // 8
    step8 = tm // 8
    row = lambda w: pl.BlockSpec((tm, w), lambda i: (i, 0))
    in_specs = [
        row(LANES), row(LANES), row(LANES), row(LANES), row(D_MODEL), row(512),
        pl.BlockSpec((8, 512), lambda i: (jnp.maximum(i * step8 - 1, 0), 0)),
        pl.BlockSpec((8, 512), lambda i: (jnp.minimum((i + 1) * step8, nb8 - 1), 0)),
        row(256), row(256), row(256), row(3072),
        _const_spec((8, 256)), _const_spec((1, 256)), _const_spec((256, 256)),
        _const_spec(wa.shape), _const_spec(wb.shape), _const_spec(wc.shape), _const_spec(wo.shape),
        _const_spec((1, D_MODEL)),
    ]
    args = list(oa) + [x, cv, cv, cv, of, ob, sg, gates, conv_w, hg_norm, _block_ones(256, HG_HEAD_DIM),
            wa, wb, wc, wo, g2]
    out_specs = [row(D_MODEL), row(D_MODEL)]
    out_shape = [jax.ShapeDtypeStruct((n, D_MODEL), F32), jax.ShapeDtypeStruct((n, D_MODEL), BF16)]
    with_router = w_router is not None
    if with_router:
        lane = np.arange(LANES)
        tri = jnp.asarray((lane[:, None] <= lane[None, :]).astype(np.float32), dtype=BF16)
        in_specs += [_const_spec(w_router.shape), _const_spec((LANES, LANES))]
        args += [w_router, tri]
        out_specs.append(pl.BlockSpec((2 * N_EXPERTS, tm), lambda i: (0, i)))
        out_shape.append(jax.ShapeDtypeStruct((2 * N_EXPERTS, n), F32))
    kern = functools.partial(_merge_kernel, tm=tm, seq=seq, with_router=with_router)
    return pl.pallas_call(
        kern,
        grid=(n // tm,),
        in_specs=in_specs,
        out_specs=tuple(out_specs),
        out_shape=tuple(out_shape),
        compiler_params=_params(("parallel",)),
        name="merge_router" if with_router else "merge",
    )(*args)


def _ffn_kernel(x_ref, h_ref, wg_ref, wu_ref, wd_ref, o_ref):
    h = h_ref[...]
    g = jnp.dot(h, wg_ref[...], preferred_element_type=F32)
    u = jnp.dot(h, wu_ref[...], preferred_element_type=F32)
    act = (jax.nn.silu(g) * u).astype(BF16)
    o_ref[...] = x_ref[...] + jnp.dot(act, wd_ref[...], preferred_element_type=F32)


def _ffn(x, h2, wg, wu, wd, tm):
    n = x.shape[0]
    row = lambda: pl.BlockSpec((tm, D_MODEL), lambda i: (i, 0))
    return pl.pallas_call(
        _ffn_kernel,
        grid=(n // tm,),
        in_specs=[row(), row(), _const_spec(wg.shape), _const_spec(wu.shape), _const_spec(wd.shape)],
        out_specs=row(),
        out_shape=jax.ShapeDtypeStruct((n, D_MODEL), F32),
        compiler_params=_params(("parallel",)),
        name="ffn_dense",
    )(x, h2, wg, wu, wd)


MOE_ROWS = 144


def _moe_expert_kernel(rt_ref, h_ref, y_ref, wg_ref, wu_ref, wd_ref, o_ref, *, expert, tm):
    rank = rt_ref[expert:expert + 1, :].astype(jnp.int32)
    w_row = rt_ref[N_EXPERTS + expert:N_EXPERTS + expert + 1, :]
    count = jnp.max(rank) + 1

    h = h_ref[...]

    def expert_rows(blk):
        rows = blk * MOE_ROWS + lax.broadcasted_iota(jnp.int32, (MOE_ROWS, tm), 0)
        sel = rank == rows
        sel_b = sel.astype(F32).astype(BF16)
        xe = jnp.dot(sel_b, h, preferred_element_type=F32).astype(BF16)
        g = jnp.dot(xe, wg_ref[...], preferred_element_type=F32)
        u = jnp.dot(xe, wu_ref[...], preferred_element_type=F32)
        act = (jax.nn.silu(g) * u).astype(BF16)
        ye = jnp.dot(act, wd_ref[...], preferred_element_type=F32)
        w_sel = jnp.sum(jnp.where(sel, w_row, 0.0), axis=1, keepdims=True)
        ye = (ye * w_sel).astype(BF16)
        return lax.dot_general(sel_b, ye, _TN, preferred_element_type=F32)

    o_ref[...] = y_ref[...] + expert_rows(0)

    def body(blk, carry):
        o_ref[...] += expert_rows(blk)
        return carry

    lax.fori_loop(1, (count + MOE_ROWS - 1) // MOE_ROWS, body, 0)


def _moe(x, h2, routing, wg, wu, wd, tm):
    n = x.shape[0]
    row = lambda: pl.BlockSpec((tm, D_MODEL), lambda i: (i, 0))
    expert_w = lambda w, e: pl.BlockSpec((None,) + w.shape[1:], lambda i: (e, 0, 0), pipeline_mode=pl.Buffered(1))
    y = x
    for e in range(N_EXPERTS):
        y = pl.pallas_call(
            functools.partial(_moe_expert_kernel, expert=e, tm=tm),
            grid=(n // tm,),
            in_specs=[pl.BlockSpec((2 * N_EXPERTS, tm), lambda i: (0, i)), row(), row(),
                      expert_w(wg, e), expert_w(wu, e), expert_w(wd, e)],
            out_specs=row(),
            out_shape=jax.ShapeDtypeStruct((n, D_MODEL), F32),
            input_output_aliases={2: 0},
            compiler_params=_params(("parallel",)),
            name="moe_expert",
        )(routing, h2, y, wg, wu, wd)
    return y


def _pick(limit, total):
    return min(limit, total)


def kernel(x_prompt, x_sample, norm_mix, w_in, q_norm, k_norm, lambda_q1, lambda_k1, lambda_q2, lambda_k2, sub_norm, conv_w, hg_lower, hg_norm, w_up_a, w_up_b, w_up_c, w_out, norm_ffn, w_gate_dense, w_up_dense, w_down_dense, w_router, w_gate_moe, w_up_moe, w_down_moe):
    depth = norm_mix.shape[0]
    lb_all = jnp.cumsum(jax.nn.softmax(hg_lower.astype(F32), axis=0), axis=0)
    lb_all = lb_all - lb_all[0]
    slopes = jnp.exp2(-8.0 * (jnp.arange(ATT_HEADS, dtype=F32) + 1.0) / ATT_HEADS)

    layers = []
    for l in range(depth):
        lam_init = 0.8 - 0.6 * math.exp(-0.3 * l)
        lam = (jnp.exp(jnp.sum(lambda_q1[l].astype(F32) * lambda_k1[l].astype(F32)))
               - jnp.exp(jnp.sum(lambda_q2[l].astype(F32) * lambda_k2[l].astype(F32))) + lam_init)
        wl = w_in[l]
        qtab, ktab, fast_ok = _alibi_tables(q_norm[l].astype(F32), k_norm[l].astype(F32))
        p = dict(
            gain=norm_mix[l].reshape(1, D_MODEL),
            w_main=wl[:, 1536:].astype(BF16),
            w_vt=wl[:, 1024:1536].T.astype(BF16),
            w_qk=wl[:, 0:2 * ATT_QK_WIDTH].astype(BF16),
            qtab=qtab, ktab=ktab, fast_ok=fast_ok,
            lb=lb_all[l],
            scal=jnp.concatenate([slopes * LOG2E, lam.reshape(1), jnp.zeros((3,), F32)]),
            subg=(sub_norm[l] * (1.0 - lam_init)).reshape(ATT_V_DIM, 1),
            conv_w=jnp.concatenate([conv_w[l], jnp.zeros((5, SC_WIDTH), F32)], axis=0),
            hg_norm=hg_norm[l].reshape(1, HG_WIDTH),
            wa=w_up_a[l].astype(BF16), wb=w_up_b[l].astype(BF16), wc=w_up_c[l].astype(BF16),
            wo=w_out[l].astype(BF16),
            g2=norm_ffn[l].reshape(1, D_MODEL),
        )
        j = l // 2
        if l % 2 == 0:
            p.update(wg=w_gate_dense[j].astype(BF16), wu=w_up_dense[j].astype(BF16),
                     wd=w_down_dense[j].astype(BF16))
        else:
            wr_hi, wr_lo = _split_bf16(w_router[j].T.astype(F32))
            p.update(w_router=jnp.concatenate([wr_hi, wr_lo, wr_hi, jnp.zeros_like(wr_hi)], axis=0),
                     wg=w_gate_moe[j].astype(BF16), wu=w_up_moe[j].astype(BF16), wd=w_down_moe[j].astype(BF16))
        layers.append(p)

    def trunk(x3):
        b, t, _ = x3.shape
        n = b * t
        tm = _pick(512, t)
        tm_in = _pick(256, t)
        tq = _pick(1024, t)
        tr = _pick(1024, t)
        tt = _pick(256, t)
        x = x3.reshape(n, D_MODEL)
        for l, p in enumerate(layers):
            q, k, vt, cv, hp, lg, sg, gates = _inproj(x, p["gain"], p["w_main"], p["w_vt"], p["w_qk"],
                                                      p["qtab"], p["ktab"], p["lb"], t, tm_in)
            oa = _attention(q.reshape(b, t, 1024), k.reshape(b, t, 1024), vt, p["scal"], p["subg"], p["fast_ok"],
                            b, t, tq, tq)
            of, ob = _hgrn(hp.reshape(b, t, 1024), lg.reshape(b, t, 512), b, t, tt)
            res = _merge(x, [o.reshape(n, LANES) for o in oa], cv, of.reshape(n, 256), ob.reshape(n, 256), sg, gates,
                         p["conv_w"], p["hg_norm"], p["wa"], p["wb"], p["wc"], p["wo"], p["g2"],
                         p.get("w_router"), t, tr)
            if l % 2 == 0:
                xn, h2 = res
                x = _ffn(xn, h2, p["wg"], p["wu"], p["wd"], tm)
            else:
                xn, h2, routing = res
                x = _moe(xn, h2, routing, p["wg"], p["wu"], p["wd"], tr)
        return x.reshape(b, t, D_MODEL)

    return (trunk(x_prompt), trunk(x_sample))
```

```python
import functools
import math

import numpy as np
import jax
import jax.numpy as jnp
from jax import lax
from jax.experimental import pallas as pl
from jax.experimental.pallas import tpu as pltpu

F32 = jnp.float32
BF16 = jnp.bfloat16

D_MODEL = 1024
ATT_HEADS = 4
ATT_HEAD_DIM = 64
ATT_V_DIM = 128
ATT_QK_WIDTH = 512
ATT_WIDTH = 512
SC_WIDTH = 256
HG_WIDTH = 256
HG_HEAD_DIM = 64
HG_CHUNK = 64
N_BRANCH = 3
N_EXPERTS = 8
EPS = 1e-6

LOG2E = math.log2(math.e)
N_SLOPE_PIECES = 3
SCORE_BOUND_LIMIT = 48.0
EXP2_ZERO_ARG = 151.0
ATT_TILES_PER_STEP = 2

LANES = 128
POS_SHIFT = 7
VMEM_LIMIT = 56 * 1024 * 1024
NEG_BIG = -1e30

_NT = (((1,), (1,)), ((), ()))
_TN = (((0,), (0,)), ((), ()))


def _const_spec(shape):
    nd = len(shape)
    return pl.BlockSpec(shape, lambda *_: (0,) * nd, pipeline_mode=pl.Buffered(1))


def _params(sem):
    return pltpu.CompilerParams(dimension_semantics=sem, vmem_limit_bytes=VMEM_LIMIT)


def _split_bf16(x):
    hi = x.astype(BF16)
    lo = (x - hi.astype(F32)).astype(BF16)
    return hi, lo


def _group_sum(x, gmat):
    hi, lo = _split_bf16(x)
    return (jnp.dot(hi, gmat, preferred_element_type=F32) + jnp.dot(lo, gmat, preferred_element_type=F32))


def _block_ones(width, group):
    idx = np.arange(width) // group
    return jnp.asarray((idx[:, None] == idx[None, :]).astype(np.float32), dtype=BF16)


def _inproj_kernel(x_ref, g_ref, wm_ref, wvt_ref, wqk_ref, qtab_ref, ktab_ref, lb_ref,
                   q_ref, k_ref, vt_ref, cv_ref, hp_ref, lg_ref, sg_ref, gate_ref, *, tm, seq):
    x = x_ref[...]
    ms = jnp.mean(x * x, axis=-1, keepdims=True)
    h = (x * lax.rsqrt(ms + EPS) * g_ref[...]).astype(BF16)

    vt_ref[...] = lax.dot_general(wvt_ref[...], h, _NT, preferred_element_type=F32).astype(BF16)

    pos = (pl.program_id(0) * tm) % seq + lax.broadcasted_iota(jnp.int32, (tm, 1), 0)
    pos_lo = (pos & (LANES - 1)).astype(F32)
    pos_hi = (pos >> POS_SHIFT).astype(F32)

    first_half = lax.broadcasted_iota(jnp.int32, (1, LANES), 1) < ATT_HEAD_DIM

    def head_norm(raw, tab_ref, out_ref):
        for hd in range(ATT_HEADS):
            slab = raw[:, LANES * hd:LANES * (hd + 1)]
            sq = slab * slab
            ss_all = jnp.sum(sq, axis=-1, keepdims=True)
            ss_0 = jnp.sum(jnp.where(first_half, sq, 0.0), axis=-1, keepdims=True)
            ss = jnp.where(first_half, ss_0, ss_all - ss_0)
            normed = slab * lax.rsqrt(ss * (1.0 / ATT_HEAD_DIM) + EPS)
            for c in range(2):
                cols = slice(LANES * (2 * hd + c), LANES * (2 * hd + c + 1))
                data = normed if c == 0 else pltpu.roll(normed, ATT_HEAD_DIM, axis=1)
                aux = tab_ref[1:2, cols] + tab_ref[2:3, cols] * pos_lo + tab_ref[3:4, cols] * pos_hi
                out_ref[:, cols] = jnp.where(first_half, data * tab_ref[0:1, cols], aux).astype(BF16)

    qk = jnp.dot(h, wqk_ref[...], preferred_element_type=F32)
    head_norm(qk[:, 0:ATT_QK_WIDTH], qtab_ref, q_ref)
    head_norm(qk[:, ATT_QK_WIDTH:2 * ATT_QK_WIDTH], ktab_ref, k_ref)

    sc = jnp.dot(h, wm_ref[:, 0:768], preferred_element_type=F32)
    cv_ref[:, 0:256] = sc[:, 0:256].astype(BF16)
    cv_ref[:, 256:512] = (sc[:, 256:512] * sc[:, 512:768]).astype(BF16)

    hg = jnp.dot(h, wm_ref[:, 768:2048], preferred_element_type=F32)
    hp_ref[:, 0:256] = jax.nn.silu(hg[:, 0:256]).astype(BF16)
    hp_ref[:, 256:512] = hg[:, 256:512].astype(BF16)
    for d in range(2):
        z = hg[:, 512 + 256 * d:768 + 256 * d]
        lb = lb_ref[d:d + 1, :]
        f = lb + (1.0 - lb) * jax.nn.sigmoid(z)
        hp_ref[:, 512 + 256 * d:768 + 256 * d] = ((1.0 - lb) * jax.nn.sigmoid(-z)).astype(BF16)
        lg_ref[:, 256 * d:256 * d + 256] = jnp.log(f)
    sg_ref[...] = jax.nn.silu(hg[:, 1024:1280]).astype(BF16)

    gate_ref[...] = jax.nn.sigmoid(
        jnp.dot(h, wm_ref[:, 2048:5120], preferred_element_type=F32)).astype(BF16)


def _inproj(x, gain, w_main, w_vt, w_qk, qtab, ktab, lb, seq, tm):
    n = x.shape[0]
    row = lambda w: pl.BlockSpec((tm, w), lambda i: (i, 0))
    out_shape = (
        jax.ShapeDtypeStruct((n, 1024), BF16),
        jax.ShapeDtypeStruct((n, 1024), BF16),
        jax.ShapeDtypeStruct((512, n), BF16),
        jax.ShapeDtypeStruct((n, 512), BF16),
        jax.ShapeDtypeStruct((n, 1024), BF16),
        jax.ShapeDtypeStruct((n, 512), F32),
        jax.ShapeDtypeStruct((n, 256), BF16),
        jax.ShapeDtypeStruct((n, 3072), BF16),
    )
    kern = functools.partial(_inproj_kernel, tm=tm, seq=seq)
    return pl.pallas_call(
        kern,
        grid=(n // tm,),
        in_specs=[row(D_MODEL), _const_spec((1, D_MODEL)), _const_spec(w_main.shape), _const_spec(w_vt.shape),
                  _const_spec(w_qk.shape), _const_spec((8, 1024)), _const_spec((8, 1024)),
                  _const_spec((2, 256))],
        out_specs=(row(1024), row(1024), pl.BlockSpec((512, tm), lambda i: (0, i)), row(512), row(1024),
                   row(512), row(256), row(3072)),
        out_shape=out_shape,
        compiler_params=_params(("parallel",)),
        name="inproj",
    )(x, gain, w_main, w_vt, w_qk, qtab, ktab, lb)


def _alibi_tables(q_gain, k_gain):
    slopes = np.exp2(-8.0 * (np.arange(ATT_HEADS, dtype=np.float64) + 1.0) / ATT_HEADS) * LOG2E
    qc = np.zeros((8, 1024), np.float32)
    kc = np.zeros((8, 1024), np.float32)
    for h in range(ATT_HEADS):
        rest = np.float32(slopes[h])
        pieces = []
        for _ in range(N_SLOPE_PIECES):
            piece = np.asarray(rest, dtype=BF16).astype(np.float32)
            pieces.append(float(piece))
            rest = np.float32(rest - piece)
        for c in range(2):
            base = LANES * (2 * h + c) + ATT_HEAD_DIM
            for i, a in enumerate(pieces):
                qc[1, base + 2 * i] = -a
                qc[1, base + 2 * i + 1] = -a * LANES
                kc[2, base + 2 * i] = 1.0
                kc[3, base + 2 * i + 1] = 1.0
                off = base + 2 * N_SLOPE_PIECES
                qc[2, off + 2 * i] = 1.0
                qc[3, off + 2 * i + 1] = 1.0
                kc[1, off + 2 * i] = a
                kc[1, off + 2 * i + 1] = a * LANES
    lane = np.arange(1024) % LANES
    lane_is_data = lane < ATT_HEAD_DIM
    qg = jnp.where(lane_is_data, jnp.tile(q_gain, 16) * (LOG2E * ATT_HEAD_DIM ** -0.5), 0.0)
    kg = jnp.where(lane_is_data, jnp.tile(k_gain, 16), 0.0)
    bound = 1.02 * LOG2E * ATT_HEAD_DIM ** 0.5 * jnp.max(jnp.abs(q_gain)) * jnp.max(jnp.abs(k_gain))
    qconst = jnp.asarray(qc[1])
    rest = bound.astype(F32)
    bound_lane0 = ATT_HEAD_DIM + 4 * N_SLOPE_PIECES
    for i in range(N_SLOPE_PIECES):
        piece = rest.astype(BF16).astype(F32)
        rest = rest - piece
        qconst = jnp.where(lane == bound_lane0 + i, -piece, qconst)
        kc[1, lane == bound_lane0 + i] = 1.0
    qtab = jnp.asarray(qc).at[0].set(qg).at[1].set(qconst)
    ktab = jnp.asarray(kc).at[0].set(kg)
    return qtab, ktab, bound <= SCORE_BOUND_LIMIT


def _attn_finish(sc_ref, subg_ref, o_ref, l0, l1, acc_ref):
    lam = sc_ref[ATT_HEADS]
    o = acc_ref[0] / l0 - lam * (acc_ref[1] / l1)
    ms = jnp.mean(o * o, axis=0, keepdims=True)
    o = o * lax.rsqrt(ms + EPS) * subg_ref[...]
    o_ref[...] = o.T.astype(BF16)


def _attn_key_tile(qi, st, nk, reach, band):
    if band:
        off = jnp.where(st <= reach, st, reach - st)
        kb = qi + off
        return jnp.clip(kb, 0, nk - 1), off > 0, (kb >= 0) & (kb < nk)
    return (qi + st) % nk, qi + st < nk, st >= 0


def _attn_fast_kernel(sc_ref, q_ref, *refs, tq, tk, nk, head, reach, band, nvisits, group):
    k_refs, vt_refs = refs[:group], refs[group:2 * group]
    absd_ref, subg_ref, o_ref, qv_ref, l_ref, acc_ref = refs[2 * group:]
    h = head
    qi = pl.program_id(1)
    st = pl.program_id(2)
    nsteps = -(-nvisits // group)

    @pl.when(st == 0)
    def _():
        lane = lax.broadcasted_iota(jnp.int32, (tq, LANES), 1)
        alibi = (lane >= ATT_HEAD_DIM) & (lane < ATT_HEAD_DIM + 4 * N_SLOPE_PIECES)
        for c in range(2):
            qc = q_ref[:, LANES * c:LANES * (c + 1)]
            qv_ref[c] = qc
            qv_ref[2 + c] = jnp.where(alibi, -qc, qc)
            qv_ref[4 + c] = jnp.where(alibi, jnp.zeros_like(qc), qc)
        l_ref[...] = jnp.zeros(l_ref.shape, F32)
        acc_ref[...] = jnp.zeros(acc_ref.shape, F32)

    def step(diag, k_ref, vt_ref, after):
        vt = vt_ref[...]
        for c in range(2):
            kc = k_ref[:, LANES * c:LANES * (c + 1)]
            if diag:
                qsel = qv_ref[4 + c]
            else:
                qsel = qv_ref[jnp.where(after, 0, 2) + c]
            s = lax.dot_general(kc, qsel, _NT, preferred_element_type=F32)
            if diag:
                s = s - sc_ref[h] * absd_ref[...]
            p = jnp.exp2(s)
            l_ref[c] += jnp.sum(p.reshape(tk // 8, 8, tq), axis=0)
            acc_ref[c] += jnp.dot(vt, p.astype(BF16), preferred_element_type=F32)

    for g in range(group):
        visit = group * st + g
        _, after, exists = _attn_key_tile(qi, visit, nk, reach, band)
        exists = exists & (visit < nvisits)
        if g == 0:
            pl.when(st == 0)(functools.partial(step, True, k_refs[0], vt_refs[0], after))
            exists = exists & (st != 0)
        pl.when(exists)(functools.partial(step, False, k_refs[g], vt_refs[g], after))

    @pl.when(st == nsteps - 1)
    def _():
        _attn_finish(sc_ref, subg_ref, o_ref, jnp.sum(l_ref[0], axis=0, keepdims=True),
                     jnp.sum(l_ref[1], axis=0, keepdims=True), acc_ref)


def _attn_safe_kernel(sc_ref, q_ref, k_ref, vt_ref, subg_ref, o_ref, qm_ref, m_ref, l_ref, acc_ref, *, tq, tk, nk):
    h = pl.program_id(1)
    qi = pl.program_id(2)
    ki = pl.program_id(3)

    @pl.when(ki == 0)
    def _():
        lane = lax.broadcasted_iota(jnp.int32, (tq, LANES), 1)
        for c in range(2):
            qc = q_ref[:, LANES * c:LANES * (c + 1)]
            qm_ref[c] = jnp.where(lane < ATT_HEAD_DIM, qc, jnp.zeros_like(qc))
        m_ref[...] = jnp.full(m_ref.shape, NEG_BIG, F32)
        l_ref[...] = jnp.zeros(l_ref.shape, F32)
        acc_ref[...] = jnp.zeros(acc_ref.shape, F32)

    kpos = ki * tk + lax.broadcasted_iota(jnp.int32, (tk, tq), 0)
    qpos = qi * tq + lax.broadcasted_iota(jnp.int32, (tk, tq), 1)
    bias = sc_ref[h] * jnp.abs(kpos - qpos).astype(F32)
    vt = vt_ref[...]
    for c in range(2):
        kc = k_ref[:, LANES * c:LANES * (c + 1)]
        s = lax.dot_general(kc, qm_ref[c], _NT, preferred_element_type=F32) - bias
        m_prev = m_ref[c]
        m_new = jnp.maximum(m_prev, jnp.max(s, axis=0, keepdims=True))
        alpha = jnp.exp2(m_prev - m_new)
        p = jnp.exp2(s - m_new)
        l_ref[c] = alpha * l_ref[c] + jnp.sum(p, axis=0, keepdims=True)
        acc_ref[c] = alpha * acc_ref[c] + jnp.dot(vt, p.astype(BF16), preferred_element_type=F32)
        m_ref[c] = m_new

    @pl.when(ki == nk - 1)
    def _():
        _attn_finish(sc_ref, subg_ref, o_ref, l_ref[0], l_ref[1], acc_ref)


def _attention(q, k, vt, scal, subg, fast_ok, b, t, tq, tk):
    nq, nk = t // tq, t // tk
    blk = 2 * LANES
    common = dict(
        grid=(b, ATT_HEADS, nq, nk),
        out_specs=pl.BlockSpec((None, tq, LANES), lambda bi, h, qi, ki: (bi, qi, h)),
        out_shape=jax.ShapeDtypeStruct((b, t, ATT_WIDTH), BF16),
        compiler_params=_params(("parallel", "parallel", "parallel", "arbitrary")),
    )
    qkv_specs = [
        pl.BlockSpec((None, tq, blk), lambda bi, h, qi, ki: (bi, qi, h)),
        pl.BlockSpec((None, tk, blk), lambda bi, h, qi, ki: (bi, ki, h)),
        pl.BlockSpec((LANES, tk), lambda bi, h, qi, ki: (h, bi * nk + ki)),
    ]
    smem = pl.BlockSpec(memory_space=pltpu.SMEM)
    idx = np.arange(tk)[:, None] - np.arange(tq)[None, :]
    absd = jnp.asarray(np.abs(idx), dtype=F32)
    slopes2 = np.exp2(-8.0 * (np.arange(ATT_HEADS) + 1.0) / ATT_HEADS) * LOG2E
    window = tuple(int((EXP2_ZERO_ARG / s - 1.0) // tk) + 1 for s in slopes2)
    def fast_head(h, q, k, vt, scal, subg):
        reach = window[h]
        band = 2 * reach + 1 < nk
        nvisits = 2 * reach + 1 if band else nk
        group = min(ATT_TILES_PER_STEP, nvisits)
        nsteps = -(-nvisits // group)
        tile = lambda qi, visit: _attn_key_tile(qi, visit, nk, reach, band)[0]
        k_specs = [pl.BlockSpec((None, tk, blk), lambda bi, qi, st, g=g: (bi, tile(qi, group * st + g), h))
                   for g in range(group)]
        vt_specs = [pl.BlockSpec((LANES, tk), lambda bi, qi, st, g=g: (h, bi * nk + tile(qi, group * st + g)))
                    for g in range(group)]
        return pl.pallas_call(
            functools.partial(_attn_fast_kernel, tq=tq, tk=tk, nk=nk, head=h, reach=reach, band=band,
                              nvisits=nvisits, group=group),
            grid=(b, nq, nsteps),
            in_specs=[smem, pl.BlockSpec((None, tq, blk), lambda bi, qi, st: (bi, qi, h))] + k_specs + vt_specs
            + [_const_spec((tk, tq)), _const_spec((ATT_V_DIM, 1))],
            out_specs=pl.BlockSpec((None, tq, LANES), lambda bi, qi, st: (bi, qi, 0)),
            out_shape=jax.ShapeDtypeStruct((b, t, LANES), BF16),
            scratch_shapes=[
                pltpu.VMEM((6, tq, LANES), BF16),
                pltpu.VMEM((2, 8, tq), F32),
                pltpu.VMEM((2, ATT_V_DIM, tq), F32),
            ],
            compiler_params=_params(("parallel", "parallel", "arbitrary")),
            name="diff_attn_h%d" % h,
        )(scal, q, *([k] * group), *([vt] * group), absd, subg)

    def fast(q, k, vt, scal, subg):
        return tuple(fast_head(h, q, k, vt, scal, subg) for h in range(ATT_HEADS))

    def safe(q, k, vt, scal, subg):
        o = safe_call(q, k, vt, scal, subg)
        return tuple(o[:, :, LANES * h:LANES * (h + 1)] for h in range(ATT_HEADS))

    def safe_call(q, k, vt, scal, subg):
        return pl.pallas_call(
            functools.partial(_attn_safe_kernel, tq=tq, tk=tk, nk=nk),
            in_specs=[smem] + qkv_specs + [_const_spec((ATT_V_DIM, 1))],
            scratch_shapes=[
                pltpu.VMEM((2, tq, LANES), BF16),
                pltpu.VMEM((2, 1, tq), F32),
                pltpu.VMEM((2, 1, tq), F32),
                pltpu.VMEM((2, ATT_V_DIM, tq), F32),
            ],
            name="diff_attn_safe",
            **common,
        )(scal, q, k, vt, subg)

    return lax.cond(fast_ok, fast, safe, q, k, vt, scal, subg)


N_LEVELS = 6


def _hgrn_constants(tt):
    c = HG_CHUNK
    t = np.arange(tt)
    same_chunk = (t[:, None] // c) == (t[None, :] // c)
    mats, masks = [], []
    for j in range(N_LEVELS):
        m = 1 << j
        blk = t // m
        same_blk = blk[:, None] == blk[None, :]
        odd = (blk % 2) == 1
        incl = same_blk & (t[None, :] <= t[:, None])
        excl_rev = same_blk & (t[None, :] > t[:, None])
        mats.append(np.where(odd[:, None], incl, excl_rev))
        same_2m = (t[:, None] // (2 * m)) == (t[None, :] // (2 * m))
        masks.append((same_2m & odd[:, None] & (~odd)[None, :]).astype(np.float32))
    mats.append(same_chunk & (t[None, :] <= t[:, None]))
    mats.append(same_chunk & (t[None, :] > t[:, None]))
    mats = np.stack(mats).astype(np.float32)
    masks = np.stack(masks)
    mats_b = mats[:, ::-1, ::-1]
    masks_b = masks[:, ::-1, ::-1]
    mst = np.concatenate([mats.reshape(-1, tt), mats_b.reshape(-1, tt)], axis=0)
    amask = np.concatenate([masks, masks_b], axis=0)
    return jnp.asarray(mst, dtype=BF16), jnp.asarray(amask, dtype=F32)


def _hgrn_kernel(hpf_ref, hpb_ref, lgf_ref, lgb_ref, mst_ref, amask_ref, g256_ref, of_ref, ob_ref, s_ref, *, tt):
    i = pl.program_id(1)
    nlev = N_LEVELS
    nmat = N_LEVELS + 2
    nchunk = tt // HG_CHUNK

    @pl.when(i == 0)
    def _():
        s_ref[...] = jnp.zeros(s_ref.shape, F32)

    lane256 = lax.broadcasted_iota(jnp.int32, (1, 256), 1)
    lane128 = lax.broadcasted_iota(jnp.int32, (1, LANES), 1)
    r128 = lax.broadcasted_iota(jnp.int32, (LANES, LANES), 0)
    c128 = lax.broadcasted_iota(jnp.int32, (LANES, LANES), 1)
    bdmask = ((r128 // HG_HEAD_DIM) == (c128 // HG_HEAD_DIM)).astype(F32)

    def direction(d, hp_ref, lg_ref, o_ref):
        qs = hp_ref[:, 0:256].astype(F32)
        v = hp_ref[:, 256:512]
        kk = hp_ref[:, 512 + 256 * d:768 + 256 * d].astype(F32)
        eall = jnp.dot(mst_ref[d * nmat * tt:(d + 1) * nmat * tt, :], lg_ref[...].astype(BF16),
                       preferred_element_type=F32)

        def expo(j):
            return jnp.exp(eall[j * tt:(j + 1) * tt])

        a = [None] * 4
        for lev in range(nlev):
            xx = expo(lev)
            qt, kt = (qs * xx).astype(BF16), (kk * xx).astype(BF16)
            am = amask_ref[d * nlev + lev]
            for pr in range(2):
                qp = qt[:, LANES * pr:LANES * (pr + 1)]
                kp = kt[:, LANES * pr:LANES * (pr + 1)]
                for hh in range(2):
                    sel = (lane128 // HG_HEAD_DIM) == hh
                    qm = jnp.where(sel, qp, jnp.zeros_like(qp))
                    p = lax.dot_general(qm, kp, _NT, preferred_element_type=F32) * am
                    idx = 2 * pr + hh
                    a[idx] = p if a[idx] is None else a[idx] + p

        o = jnp.dot((qs * kk).astype(BF16), g256_ref[...], preferred_element_type=F32) * v.astype(F32)
        for idx in range(4):
            vm = jnp.where((lane256 // HG_HEAD_DIM) == idx, v, jnp.zeros_like(v))
            contrib = jnp.dot(a[idx].astype(BF16), vm, preferred_element_type=F32)
            o = o + contrib
        o_ref[...] = o

        xq = expo(N_LEVELS)
        xk = expo(N_LEVELS + 1)
        qc = (qs * xq).astype(BF16)
        kc = (kk * xk).astype(BF16)
        order = range(nchunk) if d == 0 else range(nchunk - 1, -1, -1)
        for c in order:
            r0 = c * HG_CHUNK
            rows = slice(r0, r0 + HG_CHUNK)
            drow = r0 + HG_CHUNK - 1 if d == 0 else r0
            for pr in range(2):
                cols = slice(LANES * pr, LANES * (pr + 1))
                st = s_ref[d, pr]
                inter = lax.dot_general(qc[rows, cols], st.astype(BF16), _NT, preferred_element_type=F32)
                o_ref[rows, cols] += inter
                ut = lax.dot_general(v[rows, cols], kc[rows, cols], _TN, preferred_element_type=F32)
                dec = xq[drow:drow + 1, cols]
                s_ref[d, pr] = st * dec + ut * bdmask

    direction(0, hpf_ref, lgf_ref, of_ref)
    direction(1, hpb_ref, lgb_ref, ob_ref)


def _hgrn(hp, lg, b, t, tt):
    nt = t // tt
    mst, amask = _hgrn_constants(tt)
    kern = functools.partial(_hgrn_kernel, tt=tt)
    fwd = lambda w, cb: pl.BlockSpec((None, tt, w), lambda bi, i: (bi, i, cb))
    bwd = lambda w, cb: pl.BlockSpec((None, tt, w), lambda bi, i: (bi, nt - 1 - i, cb))
    return pl.pallas_call(
        kern,
        grid=(b, nt),
        in_specs=[fwd(1024, 0), bwd(1024, 0), fwd(256, 0), bwd(256, 1),
                  _const_spec(mst.shape), _const_spec(amask.shape), _const_spec((256, 256))],
        out_specs=(fwd(256, 0), bwd(256, 0)),
        out_shape=(jax.ShapeDtypeStruct((b, t, HG_WIDTH), F32), jax.ShapeDtypeStruct((b, t, HG_WIDTH), F32)),
        scratch_shapes=[pltpu.VMEM((2, 2, LANES, LANES), F32)],
        compiler_params=_params(("parallel", "arbitrary")),
        name="hgrn2",
    )(hp, hp, lg, lg, mst, amask, _block_ones(256, HG_HEAD_DIM))


def _merge_kernel(*refs, tm, seq, with_router):
    oa_refs = refs[:ATT_HEADS]
    refs = refs[ATT_HEADS:]
    (x_ref, cv_ref, cvp_ref, cvn_ref, of_ref, ob_ref, sg_ref, gate_ref, cw_ref, hgn_ref, g256_ref,
     wa_ref, wb_ref, wc_ref, wo_ref, g2_ref) = refs[:16]
    if with_router:
        wr_ref, tri_ref, xo_ref, h2_ref, lo_ref = refs[16:]
    else:
        xo_ref, h2_ref = refs[16:]
    i = pl.program_id(0)
    tile_start = (i * tm) % seq

    u = cv_ref[:, 256:512].astype(F32)
    row = lax.broadcasted_iota(jnp.int32, (tm, 1), 0)
    prev_row = jnp.where(tile_start == 0, 0.0, cvp_ref[7:8, 256:512].astype(F32))
    next_row = jnp.where(tile_start + tm == seq, 0.0, cvn_ref[0:1, 256:512].astype(F32))
    u_m1 = jnp.where(row == 0, prev_row, pltpu.roll(u, 1, axis=0))
    u_p1 = jnp.where(row == tm - 1, next_row, pltpu.roll(u, tm - 1, axis=0))
    conv = cw_ref[0:1, :] * u_m1 + cw_ref[1:2, :] * u + cw_ref[2:3, :] * u_p1
    yb_in = (cv_ref[:, 0:256].astype(F32) * conv).astype(BF16)

    o = of_ref[...] + ob_ref[...]
    ss = _group_sum(o * o, g256_ref[...])
    yc_in = (o * lax.rsqrt(ss * (1.0 / HG_HEAD_DIM) + EPS) * hgn_ref[...] * sg_ref[...].astype(F32)).astype(BF16)

    oa = jnp.concatenate([r[...] for r in oa_refs], axis=1)
    ya = jnp.dot(oa, wa_ref[...], preferred_element_type=F32)
    yb = jnp.dot(yb_in, wb_ref[...], preferred_element_type=F32)
    yc = jnp.dot(yc_in, wc_ref[...], preferred_element_type=F32)
    merged = (gate_ref[:, 0:1024].astype(F32) * ya + gate_ref[:, 1024:2048].astype(F32) * yb
              + gate_ref[:, 2048:3072].astype(F32) * yc)
    xn = x_ref[...] + jnp.dot(merged.astype(BF16), wo_ref[...], preferred_element_type=F32)
    xo_ref[...] = xn

    ms = jnp.mean(xn * xn, axis=-1, keepdims=True)
    h2 = xn * lax.rsqrt(ms + EPS) * g2_ref[...]
    h_hi = h2.astype(BF16)
    h2_ref[...] = h_hi
    if with_router:
        h_lo = (h2 - h_hi.astype(F32)).astype(BF16)
        t1 = lax.dot_general(wr_ref[0:16, :], h_hi, _NT, preferred_element_type=F32)
        t2 = lax.dot_general(wr_ref[16:32, :], h_lo, _NT, preferred_element_type=F32)
        lg = t1[0:N_EXPERTS] + t1[N_EXPERTS:2 * N_EXPERTS] + t2[0:N_EXPERTS]

        sub = lax.broadcasted_iota(jnp.int32, lg.shape, 0)
        m1 = jnp.max(lg, axis=0, keepdims=True)
        i1 = jnp.min(jnp.where(lg == m1, sub, N_EXPERTS), axis=0, keepdims=True)
        lg2 = jnp.where(sub == i1, NEG_BIG, lg)
        m2 = jnp.max(lg2, axis=0, keepdims=True)
        i2 = jnp.min(jnp.where(lg2 == m2, sub, N_EXPERTS), axis=0, keepdims=True)
        e2 = jnp.exp(m2 - m1)
        weights = jnp.where(sub == i1, 1.0 / (1.0 + e2), 0.0) + jnp.where(sub == i2, e2 / (1.0 + e2), 0.0)
        routed = (sub == i1) | (sub == i2)
        routed_f = routed.astype(F32)
        nblk = tm // LANES
        by_block = jnp.concatenate([routed_f[:, LANES * j:LANES * (j + 1)] for j in range(nblk)], axis=0)
        within = jnp.dot(by_block.astype(BF16), tri_ref[...], preferred_element_type=F32)
        before = jnp.zeros((N_EXPERTS, LANES), F32)
        ranks = []
        for j in range(nblk):
            wj = within[N_EXPERTS * j:N_EXPERTS * (j + 1)]
            ranks.append(wj + before - 1.0)
            before = before + wj[:, LANES - 1:LANES]
        rank = jnp.concatenate(ranks, axis=1)
        lo_ref[0:N_EXPERTS, :] = jnp.where(routed, rank, -1.0)
        lo_ref[N_EXPERTS:2 * N_EXPERTS, :] = weights


def _merge(x, oa, cv, of, ob, sg, gates, conv_w, hg_norm, wa, wb, wc, wo, g2, w_router, seq, tm):
    n = x.shape[0]
    nb8 = n // 8
    step8 = tm // 8
    row = lambda w: pl.BlockSpec((tm, w), lambda i: (i, 0))
    in_specs = [
        row(LANES), row(LANES), row(LANES), row(LANES), row(D_MODEL), row(512),
        pl.BlockSpec((8, 512), lambda i: (jnp.maximum(i * step8 - 1, 0), 0)),
        pl.BlockSpec((8, 512), lambda i: (jnp.minimum((i + 1) * step8, nb8 - 1), 0)),
        row(256), row(256), row(256), row(3072),
        _const_spec((8, 256)), _const_spec((1, 256)), _const_spec((256, 256)),
        _const_spec(wa.shape), _const_spec(wb.shape), _const_spec(wc.shape), _const_spec(wo.shape),
        _const_spec((1, D_MODEL)),
    ]
    args = list(oa) + [x, cv, cv, cv, of, ob, sg, gates, conv_w, hg_norm, _block_ones(256, HG_HEAD_DIM),
            wa, wb, wc, wo, g2]
    out_specs = [row(D_MODEL), row(D_MODEL)]
    out_shape = [jax.ShapeDtypeStruct((n, D_MODEL), F32), jax.ShapeDtypeStruct((n, D_MODEL), BF16)]
    with_router = w_router is not None
    if with_router:
        lane = np.arange(LANES)
        tri = jnp.asarray((lane[:, None] <= lane[None, :]).astype(np.float32), dtype=BF16)
        in_specs += [_const_spec(w_router.shape), _const_spec((LANES, LANES))]
        args += [w_router, tri]
        out_specs.append(pl.BlockSpec((2 * N_EXPERTS, tm), lambda i: (0, i)))
        out_shape.append(jax.ShapeDtypeStruct((2 * N_EXPERTS, n), F32))
    kern = functools.partial(_merge_kernel, tm=tm, seq=seq, with_router=with_router)
    return pl.pallas_call(
        kern,
        grid=(n // tm,),
        in_specs=in_specs,
        out_specs=tuple(out_specs),
        out_shape=tuple(out_shape),
        compiler_params=_params(("parallel",)),
        name="merge_router" if with_router else "merge",
    )(*args)


def _ffn_kernel(x_ref, h_ref, wg_ref, wu_ref, wd_ref, o_ref):
    h = h_ref[...]
    g = jnp.dot(h, wg_ref[...], preferred_element_type=F32)
    u = jnp.dot(h, wu_ref[...], preferred_element_type=F32)
    act = (jax.nn.silu(g) * u).astype(BF16)
    o_ref[...] = x_ref[...] + jnp.dot(act, wd_ref[...], preferred_element_type=F32)


def _ffn(x, h2, wg, wu, wd, tm):
    n = x.shape[0]
    row = lambda: pl.BlockSpec((tm, D_MODEL), lambda i: (i, 0))
    return pl.pallas_call(
        _ffn_kernel,
        grid=(n // tm,),
        in_specs=[row(), row(), _const_spec(wg.shape), _const_spec(wu.shape), _const_spec(wd.shape)],
        out_specs=row(),
        out_shape=jax.ShapeDtypeStruct((n, D_MODEL), F32),
        compiler_params=_params(("parallel",)),
        name="ffn_dense",
    )(x, h2, wg, wu, wd)


MOE_ROWS = 144


def _moe_expert_kernel(rt_ref, h_ref, y_ref, wg_ref, wu_ref, wd_ref, o_ref, *, expert, tm):
    rank = rt_ref[expert:expert + 1, :].astype(jnp.int32)
    w_row = rt_ref[N_EXPERTS + expert:N_EXPERTS + expert + 1, :]
    count = jnp.max(rank) + 1

    h = h_ref[...]

    def expert_rows(blk):
        rows = blk * MOE_ROWS + lax.broadcasted_iota(jnp.int32, (MOE_ROWS, tm), 0)
        sel = rank == rows
        sel_b = sel.astype(F32).astype(BF16)
        xe = jnp.dot(sel_b, h, preferred_element_type=F32).astype(BF16)
        g = jnp.dot(xe, wg_ref[...], preferred_element_type=F32)
        u = jnp.dot(xe, wu_ref[...], preferred_element_type=F32)
        act = (jax.nn.silu(g) * u).astype(BF16)
        ye = jnp.dot(act, wd_ref[...], preferred_element_type=F32)
        w_sel = jnp.sum(jnp.where(sel, w_row, 0.0), axis=1, keepdims=True)
        ye = (ye * w_sel).astype(BF16)
        return lax.dot_general(sel_b, ye, _TN, preferred_element_type=F32)

    o_ref[...] = y_ref[...] + expert_rows(0)

    def body(blk, carry):
        o_ref[...] += expert_rows(blk)
        return carry

    lax.fori_loop(1, (count + MOE_ROWS - 1) // MOE_ROWS, body, 0)


def _moe(x, h2, routing, wg, wu, wd, tm):
    n = x.shape[0]
    row = lambda: pl.BlockSpec((tm, D_MODEL), lambda i: (i, 0))
    expert_w = lambda w, e: pl.BlockSpec((None,) + w.shape[1:], lambda i: (e, 0, 0), pipeline_mode=pl.Buffered(1))
    y = x
    for e in range(N_EXPERTS):
        y = pl.pallas_call(
            functools.partial(_moe_expert_kernel, expert=e, tm=tm),
            grid=(n // tm,),
            in_specs=[pl.BlockSpec((2 * N_EXPERTS, tm), lambda i: (0, i)), row(), row(),
                      expert_w(wg, e), expert_w(wu, e), expert_w(wd, e)],
            out_specs=row(),
            out_shape=jax.ShapeDtypeStruct((n, D_MODEL), F32),
            input_output_aliases={2: 0},
            compiler_params=_params(("parallel",)),
            name="moe_expert",
        )(routing, h2, y, wg, wu, wd)
    return y


def _pick(limit, total):
    return min(limit, total)


def kernel(x_prompt, x_sample, norm_mix, w_in, q_norm, k_norm, lambda_q1, lambda_k1, lambda_q2, lambda_k2, sub_norm, conv_w, hg_lower, hg_norm, w_up_a, w_up_b, w_up_c, w_out, norm_ffn, w_gate_dense, w_up_dense, w_down_dense, w_router, w_gate_moe, w_up_moe, w_down_moe):
    depth = norm_mix.shape[0]
    lb_all = jnp.cumsum(jax.nn.softmax(hg_lower.astype(F32), axis=0), axis=0)
    lb_all = lb_all - lb_all[0]
    slopes = jnp.exp2(-8.0 * (jnp.arange(ATT_HEADS, dtype=F32) + 1.0) / ATT_HEADS)

    layers = []
    for l in range(depth):
        lam_init = 0.8 - 0.6 * math.exp(-0.3 * l)
        lam = (jnp.exp(jnp.sum(lambda_q1[l].astype(F32) * lambda_k1[l].astype(F32)))
               - jnp.exp(jnp.sum(lambda_q2[l].astype(F32) * lambda_k2[l].astype(F32))) + lam_init)
        wl = w_in[l]
        qtab, ktab, fast_ok = _alibi_tables(q_norm[l].astype(F32), k_norm[l].astype(F32))
        p = dict(
            gain=norm_mix[l].reshape(1, D_MODEL),
            w_main=wl[:, 1536:].astype(BF16),
            w_vt=wl[:, 1024:1536].T.astype(BF16),
            w_qk=wl[:, 0:2 * ATT_QK_WIDTH].astype(BF16),
            qtab=qtab, ktab=ktab, fast_ok=fast_ok,
            lb=lb_all[l],
            scal=jnp.concatenate([slopes * LOG2E, lam.reshape(1), jnp.zeros((3,), F32)]),
            subg=(sub_norm[l] * (1.0 - lam_init)).reshape(ATT_V_DIM, 1),
            conv_w=jnp.concatenate([conv_w[l], jnp.zeros((5, SC_WIDTH), F32)], axis=0),
            hg_norm=hg_norm[l].reshape(1, HG_WIDTH),
            wa=w_up_a[l].astype(BF16), wb=w_up_b[l].astype(BF16), wc=w_up_c[l].astype(BF16),
            wo=w_out[l].astype(BF16),
            g2=norm_ffn[l].reshape(1, D_MODEL),
        )
        j = l // 2
        if l % 2 == 0:
            p.update(wg=w_gate_dense[j].astype(BF16), wu=w_up_dense[j].astype(BF16),
                     wd=w_down_dense[j].astype(BF16))
        else:
            wr_hi, wr_lo = _split_bf16(w_router[j].T.astype(F32))
            p.update(w_router=jnp.concatenate([wr_hi, wr_lo, wr_hi, jnp.zeros_like(wr_hi)], axis=0),
                     wg=w_gate_moe[j].astype(BF16), wu=w_up_moe[j].astype(BF16), wd=w_down_moe[j].astype(BF16))
        layers.append(p)

    def trunk(x3):
        b, t, _ = x3.shape
        n = b * t
        tm = _pick(512, t)
        tm_in = _pick(256, t)
        tq = _pick(1024, t)
        tr = _pick(512, t)
        tt = _pick(256, t)
        x = x3.reshape(n, D_MODEL)
        for l, p in enumerate(layers):
            q, k, vt, cv, hp, lg, sg, gates = _inproj(x, p["gain"], p["w_main"], p["w_vt"], p["w_qk"],
                                                      p["qtab"], p["ktab"], p["lb"], t, tm_in)
            oa = _attention(q.reshape(b, t, 1024), k.reshape(b, t, 1024), vt, p["scal"], p["subg"], p["fast_ok"],
                            b, t, tq, tq)
            of, ob = _hgrn(hp.reshape(b, t, 1024), lg.reshape(b, t, 512), b, t, tt)
            res = _merge(x, [o.reshape(n, LANES) for o in oa], cv, of.reshape(n, 256), ob.reshape(n, 256), sg, gates,
                         p["conv_w"], p["hg_norm"], p["wa"], p["wb"], p["wc"], p["wo"], p["g2"],
                         p.get("w_router"), t, tr)
            if l % 2 == 0:
                xn, h2 = res
                x = _ffn(xn, h2, p["wg"], p["wu"], p["wd"], tm)
            else:
                xn, h2, routing = res
                x = _moe(xn, h2, routing, p["wg"], p["wu"], p["wd"], tr)
        return x.reshape(b, t, D_MODEL)

    return (trunk(x_prompt), trunk(x_sample))
```
